```python
import math
import jax, jax.numpy as jnp
from jax import lax
import numpy as np

D_MODEL = 1024
BATCH = 8
SEQ = 8192
DEPTH = 4
DEC_BATCH = 1
DEC_SEQ = 16384
PAST_LEN = 128

MLA_HEADS = 8
QK_NOPE = 64
QK_ROPE = 32
V_DIM = 64
Q_RANK = 256
KV_RANK = 128
ROPE_THETA = 10000.0
Q_BLOCK = 128
SSD_HEADS = 8
SSD_HEAD_DIM = 64
SSD_GROUPS = 2
SSD_STATE = 64
SSD_CHUNK = 128
CONV_K = 5
SSD_INNER = SSD_HEADS * SSD_HEAD_DIM
SSD_HPG = SSD_HEADS // SSD_GROUPS
CONV_DIM = SSD_INNER + 2 * SSD_GROUPS * SSD_STATE
GLA_HEADS = 4
GLA_KDIM = 64
GLA_VDIM = 128
GLA_GATE_RANK = 16
GLA_TAU = 16.0
GLA_CHUNK = 64
SGU_GROUPS = 4
SGU_CHUNK = 128
SGU_GROUP_DIM = 128
SGU_WIDTH = SGU_GROUPS * SGU_GROUP_DIM
D_FF = 2816
N_EXPERTS = 8
TOP_K = 2
D_FF_EXPERT = 3584
N_EVEN = (DEPTH + 1) // 2
N_ODD = DEPTH // 2
DN_ALPHA = (2 * DEPTH) ** 0.25
DN_BETA = (8 * DEPTH) ** -0.25
EPS = 1e-5
EV_WIDTHS = (Q_RANK, KV_RANK, QK_ROPE, SSD_INNER, CONV_DIM, 2 * SSD_HEADS)
OD_WIDTHS = (GLA_HEADS * GLA_KDIM, GLA_HEADS * GLA_KDIM, GLA_HEADS * GLA_VDIM, GLA_HEADS * GLA_VDIM, 2 * GLA_GATE_RANK, 2 * SGU_WIDTH)
EV_IN = sum(EV_WIDTHS)
OD_IN = sum(OD_WIDTHS)
MIX_EVEN = MLA_HEADS * V_DIM + SSD_INNER
MIX_ODD = GLA_HEADS * GLA_VDIM + SGU_WIDTH

kernel_name = 'hybrid_mla_ssd_gla_sgu_encoder'


def split_cols(x, widths):
    offs = [int(o) for o in np.cumsum(widths)[:-1]]
    return jnp.split(x, offs, axis=-1)


def layernorm(x, g, b):
    xf = x.astype(jnp.float32)
    mu = jnp.mean(xf, -1, keepdims=True)
    var = jnp.mean(jnp.square(xf - mu), -1, keepdims=True)
    return ((xf - mu) * lax.rsqrt(var + EPS)).astype(x.dtype) * g + b


def rms_normalize(x):
    xf = x.astype(jnp.float32)
    return (xf * lax.rsqrt(jnp.mean(jnp.square(xf), -1, keepdims=True) + EPS)).astype(x.dtype)


def swiglu(h, wg, wu, wd):
    return (jax.nn.silu(h @ wg) * (h @ wu)) @ wd


def flip(a):
    return jnp.flip(a, axis=1)


def rope_tables(s, dtype):
    half = QK_ROPE // 2
    inv = jnp.exp(-math.log(ROPE_THETA) * jnp.arange(half, dtype=jnp.float32) / half)
    ang = jnp.arange(s, dtype=jnp.float32)[:, None] * inv[None, :]
    return jnp.cos(ang).astype(dtype), jnp.sin(ang).astype(dtype)


def apply_rope(x, cos, sin):
    x1, x2 = jnp.split(x, 2, axis=-1)
    c = cos[None, :, None, :]
    s = sin[None, :, None, :]
    return jnp.concatenate([x1 * c - x2 * s, x2 * c + x1 * s], axis=-1)


def mla_attention(q, k, v):
    b, s, h, dk = q.shape
    nb = s // Q_BLOCK
    scale = dk ** -0.5
    qb = q.reshape(b, nb, Q_BLOCK, h, dk).transpose(1, 0, 2, 3, 4)

    def block(qi):
        sc = jnp.einsum('bqhd,bkhd->bhqk', qi, k).astype(jnp.float32) * scale
        p = jax.nn.softmax(sc, axis=-1).astype(v.dtype)
        return jnp.einsum('bhqk,bkhd->bqhd', p, v)

    o = lax.map(block, qb)
    return o.transpose(1, 0, 2, 3, 4).reshape(b, s, h * v.shape[-1])


def depthwise_conv(x, w, bias):
    out = lax.conv_general_dilated(x, w[:, None, :], (1,), [(CONV_K // 2, CONV_K // 2)],
                                   dimension_numbers=('NWC', 'WIO', 'NWC'),
                                   feature_group_count=x.shape[-1])
    return out + bias


def ssd_scan(x, dt, a, bm, cm):
    b, s, g, j, p = x.shape
    n = bm.shape[-1]
    L = SSD_CHUNK
    nc = s // L
    x = x.reshape(b, nc, L, g, j, p)
    dt = dt.reshape(b, nc, L, g, j)
    bm = bm.reshape(b, nc, L, g, n)
    cm = cm.reshape(b, nc, L, g, n)
    acs = jnp.cumsum(dt * a, axis=2)
    diff = acs[:, :, :, None] - acs[:, :, None, :]
    mask = jnp.tril(jnp.ones((L, L), dtype=bool))[None, None, :, :, None, None]
    decay = jnp.exp(jnp.where(mask, diff, -jnp.inf))
    cb = jnp.einsum('bclgn,bcsgn->bclsg', cm, bm)
    xdt = x * dt[..., None]
    y_diag = jnp.einsum('bclsgj,bcsgjp->bclgjp', cb[..., None] * decay, xdt)
    decay_states = jnp.exp(acs[:, :, -1:] - acs)
    states = jnp.einsum('bclgn,bclgjp->bcgjpn', bm, xdt * decay_states[..., None])
    chunk_decay = jnp.exp(acs[:, :, -1])

    def step(hs, inp):
        st, dec = inp
        return hs * dec[..., None, None] + st, hs

    h0 = jnp.zeros((b, g, j, p, n), dtype=states.dtype)
    _, prev = lax.scan(step, h0, (jnp.moveaxis(states, 1, 0), jnp.moveaxis(chunk_decay, 1, 0)))
    prev = jnp.moveaxis(prev, 0, 1)
    y_off = jnp.einsum('bclgn,bcgjpn->bclgjp', cm, prev) * jnp.exp(acs)[..., None]
    return (y_diag + y_off).reshape(b, s, g, j, p)


def gla_scan(q, k, v, lg):
    b, s, h, dk = q.shape
    dv = v.shape[-1]
    L = GLA_CHUNK
    nc = s // L
    q = q.reshape(b, nc, L, h, dk)
    k = k.reshape(b, nc, L, h, dk)
    v = v.reshape(b, nc, L, h, dv)
    bc = jnp.cumsum(lg.reshape(b, nc, L, h, dk), axis=2)
    ref = bc[:, :, L // 2:L // 2 + 1]
    qi = q * jnp.exp(bc - ref)
    ki = k * jnp.exp(ref - bc)
    att = jnp.einsum('bclhk,bcshk->bchls', qi, ki)
    mask = jnp.tril(jnp.ones((L, L), dtype=bool))[None, None, None]
    att = jnp.where(mask, att, jnp.zeros_like(att))
    o_intra = jnp.einsum('bchls,bcshv->bclhv', att, v)
    blast = bc[:, :, -1]
    states = jnp.einsum('bclhk,bclhv->bchkv', k * jnp.exp(blast[:, :, None] - bc), v)

    def step(hs, inp):
        st, dec = inp
        return hs * dec[..., None] + st, hs

    h0 = jnp.zeros((b, h, dk, dv), dtype=states.dtype)
    _, prev = lax.scan(step, h0, (jnp.moveaxis(states, 1, 0), jnp.moveaxis(jnp.exp(blast), 1, 0)))
    prev = jnp.moveaxis(prev, 0, 1)
    o_inter = jnp.einsum('bclhk,bchkv->bclhv', q * jnp.exp(bc), prev)
    return (o_intra + o_inter).reshape(b, s, h, dv)


def even_mixer(h, w_in, q_norm, w_uq, kv_norm, w_ukv, conv_w, conv_b, dt_bias, a_log, d_skip, ssm_norm, w_out):
    b, s, _ = h.shape
    c_q, c_kv, k_r, z, xbc, dt = split_cols(h @ w_in, EV_WIDTHS)
    cos, sin = rope_tables(s, h.dtype)
    q = ((rms_normalize(c_q) * q_norm) @ w_uq).reshape(b, s, MLA_HEADS, QK_NOPE + QK_ROPE)
    q = jnp.concatenate([q[..., :QK_NOPE], apply_rope(q[..., QK_NOPE:], cos, sin)], axis=-1)
    kv = ((rms_normalize(c_kv) * kv_norm) @ w_ukv).reshape(b, s, MLA_HEADS, QK_NOPE + V_DIM)
    k_rope = jnp.broadcast_to(apply_rope(k_r[:, :, None, :], cos, sin), (b, s, MLA_HEADS, QK_ROPE))
    k = jnp.concatenate([kv[..., :QK_NOPE], k_rope], axis=-1)
    o_attn = mla_attention(q, k, kv[..., QK_NOPE:])
    xbc = jax.nn.silu(depthwise_conv(xbc, conv_w, conv_b))
    xs, bm, cm = split_cols(xbc, (SSD_INNER, SSD_GROUPS * SSD_STATE, SSD_GROUPS * SSD_STATE))
    xs = xs.reshape(b, s, SSD_GROUPS, SSD_HPG, SSD_HEAD_DIM)
    bm = bm.reshape(b, s, SSD_GROUPS, SSD_STATE)
    cm = cm.reshape(b, s, SSD_GROUPS, SSD_STATE)
    dt = jax.nn.softplus(dt.astype(jnp.float32).reshape(b, s, 2, SSD_HEADS) + dt_bias.astype(jnp.float32))
    dt = dt.astype(h.dtype).reshape(b, s, 2, SSD_GROUPS, SSD_HPG)
    a = (-jnp.exp(a_log.astype(jnp.float32))).astype(h.dtype).reshape(2, SSD_GROUPS, SSD_HPG)
    y_f = ssd_scan(xs, dt[:, :, 0], a[0], bm, cm)
    y_b = flip(ssd_scan(flip(xs), flip(dt[:, :, 1]), a[1], flip(bm), flip(cm)))
    y = y_f + y_b + d_skip.reshape(SSD_GROUPS, SSD_HPG, 1) * xs
    y = rms_normalize(y.reshape(b, s, SSD_INNER) * jax.nn.silu(z)) * ssm_norm
    return jnp.concatenate([o_attn, y], axis=-1) @ w_out


def odd_mixer(h, w_in, gate_w2, gate_b, gla_norm, sgu_ln_g, sgu_ln_b, w_s, b_s, w_out):
    b, s, _ = h.shape
    q, k, v, r, gl, sg = split_cols(h @ w_in, OD_WIDTHS)
    q = q.reshape(b, s, GLA_HEADS, GLA_KDIM) * (GLA_KDIM ** -0.5)
    k = k.reshape(b, s, GLA_HEADS, GLA_KDIM)
    v = v.reshape(b, s, GLA_HEADS, GLA_VDIM)

    def log_gate(lr, w2, bias):
        lgt = jax.nn.log_sigmoid((lr @ w2 + bias).astype(jnp.float32)) / GLA_TAU
        return lgt.astype(h.dtype).reshape(b, s, GLA_HEADS, GLA_KDIM)

    g_f = log_gate(gl[..., :GLA_GATE_RANK], gate_w2[0], gate_b[0])
    g_b = log_gate(gl[..., GLA_GATE_RANK:], gate_w2[1], gate_b[1])
    o = gla_scan(q, k, v, g_f) + flip(gla_scan(flip(q), flip(k), flip(v), flip(g_b)))
    o = rms_normalize(o).reshape(b, s, GLA_HEADS * GLA_VDIM) * gla_norm * jax.nn.silu(r)
    u, sv = jnp.split(jax.nn.gelu(sg), 2, axis=-1)
    sv = layernorm(sv, sgu_ln_g, sgu_ln_b).reshape(b, s // SGU_CHUNK, SGU_CHUNK, SGU_GROUPS, SGU_GROUP_DIM)
    sp = jnp.einsum('gts,bcsgd->bctgd', w_s, sv) + b_s.T[:, :, None]
    o_sgu = u * sp.reshape(b, s, SGU_WIDTH)
    return jnp.concatenate([o, o_sgu], axis=-1) @ w_out


def moe_swiglu(h, w_router, we_gate, we_up, we_down):
    logits = jnp.einsum('bsd,de->bse', h.astype(jnp.float32), w_router.astype(jnp.float32))
    top_v, top_i = lax.top_k(logits, TOP_K)
    gates = jax.nn.softmax(top_v, axis=-1)
    combine = jnp.einsum('bsk,bske->bse', gates, jax.nn.one_hot(top_i, N_EXPERTS, dtype=jnp.float32)).astype(h.dtype)
    y = jnp.zeros_like(h)
    for e in range(N_EXPERTS):
        y = y + combine[..., e:e + 1] * swiglu(h, we_gate[e], we_up[e], we_down[e])
    return y


def even_layer(x, p):
    (w_in, q_norm, w_uq, kv_norm, w_ukv, conv_w, conv_b, dt_bias, a_log, d_skip, ssm_norm, w_out,
     ln1_g, ln1_b, w_gate, w_up, w_down, ln2_g, ln2_b) = p
    mix = even_mixer(x, w_in, q_norm, w_uq, kv_norm, w_ukv, conv_w, conv_b, dt_bias, a_log, d_skip, ssm_norm, w_out)
    x = layernorm(DN_ALPHA * x + mix, ln1_g, ln1_b)
    return layernorm(DN_ALPHA * x + swiglu(x, w_gate, w_up, w_down), ln2_g, ln2_b)


def odd_layer(x, p):
    (w_in, gate_w2, gate_b, gla_norm, sgu_ln_g, sgu_ln_b, w_s, b_s, w_out, ln1_g, ln1_b,
     w_router, we_gate, we_up, we_down, ln2_g, ln2_b) = p
    mix = odd_mixer(x, w_in, gate_w2, gate_b, gla_norm, sgu_ln_g, sgu_ln_b, w_s, b_s, w_out)
    x = layernorm(DN_ALPHA * x + mix, ln1_g, ln1_b)
    return layernorm(DN_ALPHA * x + moe_swiglu(x, w_router, we_gate, we_up, we_down), ln2_g, ln2_b)


def trunk(x, ev, od):
    for i in range(DEPTH):
        if i % 2 == 0:
            x = even_layer(x, tuple(t[i // 2] for t in ev))
        else:
            x = odd_layer(x, tuple(t[i // 2] for t in od))
    return x


def setup_inputs(seed: int = 0) -> dict:
    key = jax.random.key(seed)
    keys = jax.random.split(key, 64)
    counter = [0]

    def nk():
        counter[0] += 1
        return keys[counter[0] - 1]

    def nrm(shape, scale):
        return jax.random.normal(nk(), shape, jnp.float32) * scale

    def gain(shape):
        return 1.0 + nrm(shape, 0.1)

    E, O, D = N_EVEN, N_ODD, D_MODEL
    x_prompt = nrm((BATCH, SEQ, D), 1.0)
    x_sample = nrm((DEC_BATCH, DEC_SEQ, D), 1.0)
    ev_w_in = nrm((E, D, EV_IN), D ** -0.5)
    ev_q_norm = gain((E, Q_RANK))
    ev_w_uq = nrm((E, Q_RANK, MLA_HEADS * (QK_NOPE + QK_ROPE)), Q_RANK ** -0.5)
    ev_kv_norm = gain((E, KV_RANK))
    ev_w_ukv = nrm((E, KV_RANK, MLA_HEADS * (QK_NOPE + V_DIM)), KV_RANK ** -0.5)
    ev_conv_w = nrm((E, CONV_K, CONV_DIM), CONV_K ** -0.5)
    ev_conv_b = nrm((E, CONV_DIM), 0.02)
    u = jax.random.uniform(nk(), (E, 2, SSD_HEADS), jnp.float32)
    dt0 = jnp.exp(u * (math.log(0.1) - math.log(1e-3)) + math.log(1e-3))
    ev_dt_bias = dt0 + jnp.log(-jnp.expm1(-dt0))
    ev_a_log = jnp.log(jax.random.uniform(nk(), (E, 2, SSD_HEADS), jnp.float32, 1.0, 16.0))
    ev_d_skip = gain((E, SSD_HEADS))
    ev_ssm_norm = gain((E, SSD_INNER))
    ev_w_out = nrm((E, MIX_EVEN, D), MIX_EVEN ** -0.5 * DN_BETA)
    ev_ln1_g = gain((E, D))
    ev_ln1_b = nrm((E, D), 0.02)
    ev_w_gate = nrm((E, D, D_FF), D ** -0.5)
    ev_w_up = nrm((E, D, D_FF), D ** -0.5)
    ev_w_down = nrm((E, D_FF, D), D_FF ** -0.5 * DN_BETA)
    ev_ln2_g = gain((E, D))
    ev_ln2_b = nrm((E, D), 0.02)
    od_w_in = nrm((O, D, OD_IN), D ** -0.5)
    od_gate_w2 = nrm((O, 2, GLA_GATE_RANK, GLA_HEADS * GLA_KDIM), GLA_GATE_RANK ** -0.5)
    od_gate_b = nrm((O, 2, GLA_HEADS * GLA_KDIM), 0.1)
    od_gla_norm = gain((O, GLA_HEADS * GLA_VDIM))
    od_sgu_ln_g = gain((O, SGU_WIDTH))
    od_sgu_ln_b = nrm((O, SGU_WIDTH), 0.02)
    od_w_s = nrm((O, SGU_GROUPS, SGU_CHUNK, SGU_CHUNK), SGU_CHUNK ** -0.5)
    od_b_s = 1.0 + nrm((O, SGU_GROUPS, SGU_CHUNK), 0.1)
    od_w_out = nrm((O, MIX_ODD, D), MIX_ODD ** -0.5 * DN_BETA)
    od_ln1_g = gain((O, D))
    od_ln1_b = nrm((O, D), 0.02)
    od_w_router = nrm((O, D, N_EXPERTS), D ** -0.5)
    od_we_gate = nrm((O, N_EXPERTS, D, D_FF_EXPERT), D ** -0.5)
    od_we_up = nrm((O, N_EXPERTS, D, D_FF_EXPERT), D ** -0.5)
    od_we_down = nrm((O, N_EXPERTS, D_FF_EXPERT, D), D_FF_EXPERT ** -0.5 * DN_BETA)
    od_ln2_g = gain((O, D))
    od_ln2_b = nrm((O, D), 0.02)
    return {'x_prompt': x_prompt, 'x_sample': x_sample,
            'ev_w_in': ev_w_in, 'ev_q_norm': ev_q_norm, 'ev_w_uq': ev_w_uq, 'ev_kv_norm': ev_kv_norm,
            'ev_w_ukv': ev_w_ukv, 'ev_conv_w': ev_conv_w, 'ev_conv_b': ev_conv_b, 'ev_dt_bias': ev_dt_bias,
            'ev_a_log': ev_a_log, 'ev_d_skip': ev_d_skip, 'ev_ssm_norm': ev_ssm_norm, 'ev_w_out': ev_w_out,
            'ev_ln1_g': ev_ln1_g, 'ev_ln1_b': ev_ln1_b, 'ev_w_gate': ev_w_gate, 'ev_w_up': ev_w_up,
            'ev_w_down': ev_w_down, 'ev_ln2_g': ev_ln2_g, 'ev_ln2_b': ev_ln2_b,
            'od_w_in': od_w_in, 'od_gate_w2': od_gate_w2, 'od_gate_b': od_gate_b, 'od_gla_norm': od_gla_norm,
            'od_sgu_ln_g': od_sgu_ln_g, 'od_sgu_ln_b': od_sgu_ln_b, 'od_w_s': od_w_s, 'od_b_s': od_b_s,
            'od_w_out': od_w_out, 'od_ln1_g': od_ln1_g, 'od_ln1_b': od_ln1_b, 'od_w_router': od_w_router,
            'od_we_gate': od_we_gate, 'od_we_up': od_we_up, 'od_we_down': od_we_down,
            'od_ln2_g': od_ln2_g, 'od_ln2_b': od_ln2_b}


def reference(x_prompt, x_sample,
              ev_w_in, ev_q_norm, ev_w_uq, ev_kv_norm, ev_w_ukv, ev_conv_w, ev_conv_b, ev_dt_bias,
              ev_a_log, ev_d_skip, ev_ssm_norm, ev_w_out, ev_ln1_g, ev_ln1_b, ev_w_gate, ev_w_up,
              ev_w_down, ev_ln2_g, ev_ln2_b,
              od_w_in, od_gate_w2, od_gate_b, od_gla_norm, od_sgu_ln_g, od_sgu_ln_b, od_w_s, od_b_s,
              od_w_out, od_ln1_g, od_ln1_b, od_w_router, od_we_gate, od_we_up, od_we_down,
              od_ln2_g, od_ln2_b):
    ev = (ev_w_in, ev_q_norm, ev_w_uq, ev_kv_norm, ev_w_ukv, ev_conv_w, ev_conv_b, ev_dt_bias,
          ev_a_log, ev_d_skip, ev_ssm_norm, ev_w_out, ev_ln1_g, ev_ln1_b, ev_w_gate, ev_w_up,
          ev_w_down, ev_ln2_g, ev_ln2_b)
    od = (od_w_in, od_gate_w2, od_gate_b, od_gla_norm, od_sgu_ln_g, od_sgu_ln_b, od_w_s, od_b_s,
          od_w_out, od_ln1_g, od_ln1_b, od_w_router, od_we_gate, od_we_up, od_we_down,
          od_ln2_g, od_ln2_b)
    y_prompt = trunk(x_prompt, ev, od)
    y_sample = trunk(x_sample, ev, od)
    return (y_prompt, y_sample)
```

```python
import functools
import math

import jax
import jax.numpy as jnp
from jax import lax
from jax.experimental import pallas as pl
from jax.experimental.pallas import tpu as pltpu
from jax.experimental.pallas import tpu_sc as plsc

BF = jnp.bfloat16
F32 = jnp.float32
HI = lax.Precision.HIGHEST

D_MODEL = 1024
DEPTH = 4
MLA_HEADS = 8
QK_NOPE = 64
QK_ROPE = 32
V_DIM = 64
Q_RANK = 256
KV_RANK = 128
ROPE_THETA = 10000.0
SSD_HEADS = 8
SSD_HEAD_DIM = 64
SSD_GROUPS = 2
SSD_STATE = 64
SSD_CHUNK = 128
CONV_K = 5
SSD_INNER = SSD_HEADS * SSD_HEAD_DIM
SSD_HPG = SSD_HEADS // SSD_GROUPS
CONV_DIM = SSD_INNER + 2 * SSD_GROUPS * SSD_STATE
GLA_HEADS = 4
GLA_KDIM = 64
GLA_VDIM = 128
GLA_GATE_RANK = 16
GLA_TAU = 16.0
GLA_CHUNK = 64
SGU_GROUPS = 4
SGU_CHUNK = 128
SGU_GROUP_DIM = 128
SGU_WIDTH = SGU_GROUPS * SGU_GROUP_DIM
D_FF = 2816
N_EXPERTS = 8
D_FF_EXPERT = 3584
DN_ALPHA = (2 * DEPTH) ** 0.25
EPS = 1e-5

LANES = 128
SUBLANES = 8
VMEM_LIMIT = 56 * 1024 * 1024
SC_WINDOW = 128
SC_ROW_SPLIT = 4

ROW_TILE = 512
FF_CHUNK = 256
ATT_TQ = 512
ATT_TK = 512
MOE_TILE = 512
ROUTE_TILE = 512
GLA_ROWS = 2 * GLA_CHUNK


def _cparams(*sem):
    return pltpu.CompilerParams(dimension_semantics=sem, vmem_limit_bytes=VMEM_LIMIT)


def _resident(shape, index_map):
    return pl.BlockSpec(shape, index_map, pipeline_mode=pl.Buffered(1))


def _rms(x):
    return x * lax.rsqrt(jnp.mean(x * x, axis=-1, keepdims=True) + EPS)


def _layernorm(x, g, b):
    mu = jnp.mean(x, axis=-1, keepdims=True)
    xc = x - mu
    var = jnp.mean(xc * xc, axis=-1, keepdims=True)
    return xc * lax.rsqrt(var + EPS) * g + b


def _silu(x):
    return x * jax.nn.sigmoid(x)


def _softplus(x):
    return jnp.maximum(x, 0.0) + jnp.log1p(jnp.exp(-jnp.abs(x)))


def _dot(a, b):
    return jnp.dot(a, b, preferred_element_type=F32)


def _dot_nt(a, b):
    return lax.dot_general(a, b, (((1,), (1,)), ((), ())), preferred_element_type=F32)


def _proj_kernel(x_ref, w_ref, *o_refs, splits):
    xb = x_ref[...].astype(BF)
    off = 0
    for o_ref, n in zip(o_refs, splits):
        o_ref[...] = _dot(xb, w_ref[:, off:off + n]).astype(o_ref.dtype)
        off += n


def _proj(x2d, w_bf, splits, dtypes):
    t, k = x2d.shape
    n = w_bf.shape[1]
    tm = min(ROW_TILE, t)
    return pl.pallas_call(
        functools.partial(_proj_kernel, splits=splits),
        grid=(t // tm,),
        in_specs=[pl.BlockSpec((tm, k), lambda i: (i, 0)), _resident((k, n), lambda i: (0, 0))],
        out_specs=[pl.BlockSpec((tm, s), lambda i: (i, 0)) for s in splits],
        out_shape=[jax.ShapeDtypeStruct((t, s), d) for s, d in zip(splits, dtypes)],
        compiler_params=_cparams("parallel"),
        name="proj_in",
    )(x2d, w_bf)


def _mla_prep_kernel(m_ref, cos_ref, sin_ref, qn_ref, kvn_ref, wqa_ref, wqb_ref, wk_ref, wv_ref, vadd_ref,
                     q_ref, k_ref, v_ref):
    cq = m_ref[:, 0:Q_RANK]
    ckv = m_ref[:, Q_RANK:Q_RANK + KV_RANK]
    kra = m_ref[:, Q_RANK + KV_RANK:Q_RANK + KV_RANK + LANES]
    krb = m_ref[:, Q_RANK + KV_RANK + LANES:Q_RANK + KV_RANK + 2 * LANES]
    cos = cos_ref[...]
    sin = sin_ref[...]
    cos8 = jnp.concatenate([cos] * MLA_HEADS, axis=1)
    sin8 = jnp.concatenate([sin] * MLA_HEADS, axis=1)
    cqn = (_rms(cq) * qn_ref[...]).astype(BF)
    q = _dot(cqn, wqa_ref[...]) * cos8 + _dot(cqn, wqb_ref[...]) * sin8
    q_ref[...] = (q * ((QK_NOPE + QK_ROPE) ** -0.5)).astype(BF)
    ckvn = (_rms(ckv) * kvn_ref[...]).astype(BF)
    kr = kra * cos + krb * sin
    k = _dot(ckvn, wk_ref[...]) + jnp.concatenate([kr] * MLA_HEADS, axis=1)
    k_ref[...] = k.astype(BF)
    v_ref[...] = (_dot(ckvn, wv_ref[...]) + vadd_ref[...]).astype(BF)


def _mla_prep(mla_in, cos_t, sin_t, q_norm, kv_norm, wqa, wqb, wk, wv, vadd, seq):
    t = mla_in.shape[0]
    tm = min(ROW_TILE, seq)
    nseq = seq // tm
    hw = MLA_HEADS * LANES
    full = lambda i: (0, 0)
    return pl.pallas_call(
        _mla_prep_kernel,
        grid=(t // tm,),
        in_specs=[pl.BlockSpec((tm, mla_in.shape[1]), lambda i: (i, 0)),
                  pl.BlockSpec((tm, LANES), lambda i: (i % nseq, 0)),
                  pl.BlockSpec((tm, LANES), lambda i: (i % nseq, 0)),
                  pl.BlockSpec((1, Q_RANK), full), pl.BlockSpec((1, KV_RANK), full),
                  pl.BlockSpec((Q_RANK, hw), full), pl.BlockSpec((Q_RANK, hw), full),
                  pl.BlockSpec((KV_RANK, hw), full), pl.BlockSpec((KV_RANK, hw), full),
                  pl.BlockSpec((1, hw), full)],
        out_specs=[pl.BlockSpec((tm, hw), lambda i: (i, 0))] * 3,
        out_shape=[jax.ShapeDtypeStruct((t, hw), BF)] * 3,
        compiler_params=_cparams("parallel"),
        name="mla_prep",
    )(mla_in, cos_t, sin_t, q_norm, kv_norm, wqa, wqb, wk, wv, vadd)


def _flash_kernel(q_ref, k_ref, v_ref, o_ref, m_sc, acc_sc, *, tk, nk):
    for h in range(2):
        m_sc[h] = jnp.full(m_sc.shape[1:], -jnp.inf, F32)
        acc_sc[h] = jnp.zeros(acc_sc.shape[1:], F32)

    def body(j, carry):
        off = pl.multiple_of(j * tk, tk)
        for h in range(2):
            lanes = slice(h * LANES, (h + 1) * LANES)
            s = _dot_nt(q_ref[0, :, lanes], k_ref[0, pl.ds(off, tk), lanes])
            m_prev = m_sc[h]
            m_new = jnp.maximum(m_prev, jnp.max(s, axis=1, keepdims=True))
            p = jnp.exp(s - m_new[:, 0:1])
            alpha = jnp.exp(m_prev - m_new)
            acc_sc[h] = alpha * acc_sc[h] + _dot(p.astype(BF), v_ref[0, pl.ds(off, tk), lanes])
            m_sc[h] = m_new
        return carry

    lax.fori_loop(0, nk, body, 0)
    acc_e = acc_sc[0]
    acc_o = acc_sc[1]
    lane = lax.broadcasted_iota(jnp.int32, acc_e.shape, 1)
    o_ref[0] = jnp.where(lane < V_DIM, acc_e / acc_e[:, V_DIM:V_DIM + 1], acc_o / acc_o[:, 0:1])


def _flash(q, k, v):
    b, s, hw = q.shape
    tq = min(ATT_TQ, s)
    tk = min(ATT_TK, s)
    pairs = MLA_HEADS // 2
    return pl.pallas_call(
        functools.partial(_flash_kernel, tk=tk, nk=s // tk),
        grid=(b, pairs, s // tq),
        in_specs=[pl.BlockSpec((1, tq, 2 * LANES), lambda bi, hp, i: (bi, i, hp)),
                  pl.BlockSpec((1, s, 2 * LANES), lambda bi, hp, i: (bi, 0, hp)),
                  pl.BlockSpec((1, s, 2 * LANES), lambda bi, hp, i: (bi, 0, hp))],
        out_specs=pl.BlockSpec((1, tq, LANES), lambda bi, hp, i: (bi, i, hp)),
        out_shape=jax.ShapeDtypeStruct((b, s, MLA_HEADS * V_DIM), F32),
        scratch_shapes=[pltpu.VMEM((2, tq, LANES), F32), pltpu.VMEM((2, tq, LANES), F32)],
        compiler_params=_cparams("parallel", "parallel", "arbitrary"),
        name="mla_flash",
    )(q, k, v)


def _ssd_kernel(*refs, reverse, nc):
    if reverse:
        (xc_ref, xp_ref, xn_ref, dt_ref, cw_ref, cb_ref, dtb_ref, alog_ref,
         yf_ref, z_ref, dskip_ref, nrm_ref, o_ref, xe_sc, st_sc) = refs
    else:
        (xc_ref, xp_ref, xn_ref, dt_ref, cw_ref, cb_ref, dtb_ref, alog_ref, o_ref, xe_sc, st_sc) = refs
    L = SSD_CHUNK
    N = SSD_STATE
    P = SSD_HEAD_DIM
    c = pl.program_id(1)
    cc = (nc - 1 - c) if reverse else c

    @pl.when(c == 0)
    def _():
        st_sc[...] = jnp.zeros(st_sc.shape, F32)

    xe_sc[0:SUBLANES, :] = jnp.where(cc > 0, xp_ref[0], 0.0)
    xe_sc[SUBLANES:SUBLANES + L, :] = xc_ref[0]
    xe_sc[SUBLANES + L:2 * SUBLANES + L, :] = jnp.where(cc < nc - 1, xn_ref[0], 0.0)
    conv = cb_ref[...] + cw_ref[0:1, :] * xe_sc[pl.ds(SUBLANES - CONV_K // 2, L), :]
    for j in range(1, CONV_K):
        conv = conv + cw_ref[j:j + 1, :] * xe_sc[pl.ds(SUBLANES - CONV_K // 2 + j, L), :]
    xbc = _silu(conv)
    xs = xbc[:, :SSD_INNER]
    bc = xbc[:, SSD_INNER:]
    bc_t = bc.T

    lane = lax.broadcasted_iota(jnp.int32, (L, LANES), 1)
    dtv = _softplus(dt_ref[0] + dtb_ref[...])
    a = jnp.where(lane[0:1] < 2 * SSD_HEADS, -jnp.exp(alog_ref[...]), 0.0)
    dta = dtv * a
    row_i = lax.broadcasted_iota(jnp.int32, (L, L), 0)
    col_i = lax.broadcasted_iota(jnp.int32, (L, L), 1)
    causal = (col_i >= row_i) if reverse else (col_i <= row_i)
    acs = jnp.dot(causal.astype(F32), dta, precision=HI, preferred_element_type=F32)
    acs_t = acs.T
    d0 = SSD_HEADS if reverse else 0
    end = 0 if reverse else L - 1

    ys = []
    for g in range(SSD_GROUPS):
        bm_g = bc[:, g * N:(g + 1) * N].astype(BF)
        cm_g = bc[:, (SSD_GROUPS + g) * N:(SSD_GROUPS + g + 1) * N].astype(BF)
        bm_t_g = bc_t[g * N:(g + 1) * N, :].astype(BF)
        cb = _dot_nt(cm_g, bm_g)
        for jh in range(SSD_HPG):
            h = g * SSD_HPG + jh
            col = acs[:, d0 + h:d0 + h + 1]
            row = acs_t[d0 + h:d0 + h + 1, :]
            decay = jnp.exp(jnp.where(causal, col - row, -jnp.inf))
            xdt = xs[:, h * P:(h + 1) * P] * dtv[:, d0 + h:d0 + h + 1]
            y_diag = _dot((cb * decay).astype(BF), xdt.astype(BF))
            tot = acs[end:end + 1, d0 + h:d0 + h + 1]
            xw = xdt * jnp.exp(tot - col)
            states_t = _dot(bm_t_g, xw.astype(BF))
            prev_t = st_sc[h]
            y_off = _dot(cm_g, prev_t.astype(BF)) * jnp.exp(col)
            st_sc[h] = prev_t * jnp.exp(tot) + states_t
            ys.append(y_diag + y_off)
    y = jnp.concatenate(ys, axis=1)
    if reverse:
        y = y + yf_ref[0] + dskip_ref[...] * xs
        y = _rms(y * _silu(z_ref[0])) * nrm_ref[...]
    o_ref[0] = y.astype(o_ref.dtype)


def _ssd(xbc, dt, z, conv_w, conv_b, dt_bias, a_log, d_skip, ssm_norm):
    b, s, _ = xbc.shape
    L = SSD_CHUNK
    nc = s // L
    hb = L // SUBLANES
    nhb = s // SUBLANES

    def call(reverse, extra_in, extra_specs, out_dtype):
        cidx = (lambda c: nc - 1 - c) if reverse else (lambda c: c)
        row = lambda bi, c: (bi, cidx(c), 0)
        full = lambda bi, c: (0, 0)
        in_specs = [
            pl.BlockSpec((1, L, CONV_DIM), row),
            pl.BlockSpec((1, SUBLANES, CONV_DIM), lambda bi, c: (bi, jnp.maximum(cidx(c) * hb - 1, 0), 0)),
            pl.BlockSpec((1, SUBLANES, CONV_DIM), lambda bi, c: (bi, jnp.minimum((cidx(c) + 1) * hb, nhb - 1), 0)),
            pl.BlockSpec((1, L, LANES), row),
            pl.BlockSpec((CONV_K, CONV_DIM), full), pl.BlockSpec((1, CONV_DIM), full),
            pl.BlockSpec((1, LANES), full), pl.BlockSpec((1, LANES), full),
        ] + [pl.BlockSpec(sh, row if len(sh) == 3 else full) for sh in extra_specs]
        return pl.pallas_call(
            functools.partial(_ssd_kernel, reverse=reverse, nc=nc),
            grid=(b, nc),
            in_specs=in_specs,
            out_specs=pl.BlockSpec((1, L, SSD_INNER), row),
            out_shape=jax.ShapeDtypeStruct((b, s, SSD_INNER), out_dtype),
            scratch_shapes=[pltpu.VMEM((L + 2 * SUBLANES, CONV_DIM), F32),
                            pltpu.VMEM((SSD_HEADS, SSD_STATE, SSD_HEAD_DIM), F32)],
            compiler_params=_cparams("parallel", "arbitrary"),
            name="ssd_bwd" if reverse else "ssd_fwd",
        )(xbc, xbc, xbc, dt, conv_w, conv_b, dt_bias, a_log, *extra_in)

    y_f = call(False, (), (), F32)
    return call(True, (y_f, z, d_skip, ssm_norm),
                ((1, L, SSD_INNER), (1, L, SSD_INNER), (1, SSD_INNER), (1, SSD_INNER)), F32)


def _outproj_ln_kernel(a_ref, b_ref, x_ref, wa_ref, wb_ref, g_ref, beta_ref, o_ref):
    y = _dot(a_ref[...].astype(BF), wa_ref[...]) + _dot(b_ref[...].astype(BF), wb_ref[...])
    o_ref[...] = _layernorm(DN_ALPHA * x_ref[...] + y, g_ref[...], beta_ref[...])


def _outproj_ln(a, b, x, wa, wb, g, beta):
    t, d = x.shape
    tm = min(ROW_TILE, t)
    row = lambda i: (i, 0)
    full = lambda i: (0, 0)
    return pl.pallas_call(
        _outproj_ln_kernel,
        grid=(t // tm,),
        in_specs=[pl.BlockSpec((tm, a.shape[1]), row), pl.BlockSpec((tm, b.shape[1]), row),
                  pl.BlockSpec((tm, d), row),
                  _resident(wa.shape, full), _resident(wb.shape, full),
                  pl.BlockSpec((1, d), full), pl.BlockSpec((1, d), full)],
        out_specs=pl.BlockSpec((tm, d), row),
        out_shape=jax.ShapeDtypeStruct((t, d), F32),
        compiler_params=_cparams("parallel"),
        name="outproj_ln",
    )(a, b, x, wa, wb, g, beta)


def _swiglu_acc(xb, wg_ref, wu_ref, wd_ref, acc_sc, nf):
    for f in range(nf):
        cols = slice(f * FF_CHUNK, (f + 1) * FF_CHUNK)
        h = _silu(_dot(xb, wg_ref[:, cols])) * _dot(xb, wu_ref[:, cols])
        part = _dot(h.astype(BF), wd_ref[cols, :])
        if f == 0:
            acc_sc[...] = part
        else:
            acc_sc[...] += part


def _ffn_ln_kernel(x_ref, wg_ref, wu_ref, wd_ref, g_ref, beta_ref, o_ref, acc_sc, *, nf):
    x = x_ref[...]
    _swiglu_acc(x.astype(BF), wg_ref, wu_ref, wd_ref, acc_sc, nf)
    o_ref[...] = _layernorm(DN_ALPHA * x + acc_sc[...], g_ref[...], beta_ref[...])


def _ffn_ln(x, wg, wu, wd, g, beta):
    t, d = x.shape
    f = wg.shape[1]
    tm = min(ROW_TILE, t)
    row = lambda i: (i, 0)
    full = lambda i: (0, 0)
    return pl.pallas_call(
        functools.partial(_ffn_ln_kernel, nf=f // FF_CHUNK),
        grid=(t // tm,),
        in_specs=[pl.BlockSpec((tm, d), row),
                  _resident((d, f), full), _resident((d, f), full), _resident((f, d), full),
                  pl.BlockSpec((1, d), full), pl.BlockSpec((1, d), full)],
        out_specs=pl.BlockSpec((tm, d), row),
        out_shape=jax.ShapeDtypeStruct((t, d), F32),
        scratch_shapes=[pltpu.VMEM((tm, d), F32)],
        compiler_params=_cparams("parallel"),
        name="ffn_ln",
    )(x, wg, wu, wd, g, beta)


def _moe_ffn_kernel(te_ref, tv_ref, xs_ref, wg_ref, wu_ref, wd_ref, o_ref, acc_sc, *, nf):
    @pl.when(tv_ref[pl.program_id(0)] > 0)
    def _():
        _swiglu_acc(xs_ref[...].astype(BF), wg_ref, wu_ref, wd_ref, acc_sc, nf)
        o_ref[...] = acc_sc[...]


def _moe_ffn(xs, tile_expert, tile_valid, wg, wu, wd):
    p, d = xs.shape
    f = wg.shape[2]
    tm = MOE_TILE
    row = lambda i, te, tv: (i, 0)
    grid_spec = pltpu.PrefetchScalarGridSpec(
        num_scalar_prefetch=2,
        grid=(p // tm,),
        in_specs=[pl.BlockSpec((tm, d), row),
                  _resident((None, d, f), lambda i, te, tv: (te[i], 0, 0)),
                  _resident((None, d, f), lambda i, te, tv: (te[i], 0, 0)),
                  _resident((None, f, d), lambda i, te, tv: (te[i], 0, 0))],
        out_specs=pl.BlockSpec((tm, d), row),
        scratch_shapes=[pltpu.VMEM((tm, d), F32)],
    )
    return pl.pallas_call(
        functools.partial(_moe_ffn_kernel, nf=f // FF_CHUNK),
        grid_spec=grid_spec,
        out_shape=jax.ShapeDtypeStruct((p, d), F32),
        compiler_params=_cparams("arbitrary"),
        name="moe_ffn",
    )(tile_expert, tile_valid, xs, wg, wu, wd)


def _gla_kernel(*refs, reverse):
    if reverse:
        q_ref, k_ref, v_ref, gl_ref, w2_ref, gb_ref, of_ref, r_ref, gn_ref, o_ref, st_sc = refs
    else:
        q_ref, k_ref, v_ref, gl_ref, w2_ref, gb_ref, o_ref, st_sc = refs
    L = GLA_CHUNK
    R = GLA_ROWS
    dk = GLA_KDIM
    dv = GLA_VDIM

    @pl.when(pl.program_id(1) == 0)
    def _():
        st_sc[...] = jnp.zeros(st_sc.shape, F32)

    pre = jnp.dot(gl_ref[0], w2_ref[...], precision=HI, preferred_element_type=F32) + gb_ref[...]
    lg = -_softplus(-pre) * (1.0 / GLA_TAU)
    row_i = lax.broadcasted_iota(jnp.int32, (R, R), 0)
    col_i = lax.broadcasted_iota(jnp.int32, (R, R), 1)
    same = (row_i // L) == (col_i // L)
    causal = (col_i >= row_i) if reverse else (col_i <= row_i)
    bc = jnp.dot((same & causal).astype(F32), lg, precision=HI, preferred_element_type=F32)
    mid = (L // 2 - 1) if reverse else (L // 2)
    end = 0 if reverse else (L - 1)
    width = bc.shape[1]
    ref_b = jnp.concatenate([jnp.broadcast_to(bc[ci * L + mid:ci * L + mid + 1], (L, width))
                             for ci in range(R // L)], axis=0)
    end_b = jnp.concatenate([jnp.broadcast_to(bc[ci * L + end:ci * L + end + 1], (L, width))
                             for ci in range(R // L)], axis=0)
    q = q_ref[0] * (dk ** -0.5)
    k = k_ref[0]
    qi = (q * jnp.exp(bc - ref_b)).astype(BF)
    ki = (k * jnp.exp(ref_b - bc)).astype(BF)
    qe = (q * jnp.exp(bc)).astype(BF)
    kd_t = (k * jnp.exp(end_b - bc)).T
    bc_t = bc.T
    lane = lax.broadcasted_iota(jnp.int32, (dk, R), 1)
    mask_c = causal[0:L, 0:L]
    order = range(R // L - 1, -1, -1) if reverse else range(R // L)
    for ci in order:
        rows = slice(ci * L, (ci + 1) * L)
        for h in range(GLA_HEADS):
            kl = slice(h * dk, (h + 1) * dk)
            vl = slice(h * dv, (h + 1) * dv)
            vb = v_ref[0, :, vl].astype(BF)
            att = jnp.where(mask_c, _dot_nt(qi[rows, kl], ki[rows, kl]), 0.0)
            prev = st_sc[h]
            piece = _dot(att.astype(BF), vb[rows]) + _dot(qe[rows, kl], prev.astype(BF))
            kdm = jnp.where(lane // L == ci, kd_t[kl, :], 0.0).astype(BF)
            e_idx = ci * L + end
            st_sc[h] = prev * jnp.exp(bc_t[kl, e_idx:e_idx + 1]) + _dot(kdm, vb)
            if reverse:
                piece = _rms(piece + of_ref[0, rows, vl]) * gn_ref[:, vl] * _silu(r_ref[0, rows, vl])
            o_ref[0, rows, vl] = piece.astype(o_ref.dtype)


def _gla(q, k, v, gl, r, w2f, w2b, gbf, gbb, gla_norm):
    b, s, _ = q.shape
    R = GLA_ROWS
    nb = s // R
    hk = GLA_HEADS * GLA_KDIM
    hv = GLA_HEADS * GLA_VDIM

    def call(reverse, w2, gb, extra_in, extra_shapes):
        cidx = (lambda c: nb - 1 - c) if reverse else (lambda c: c)
        row = lambda bi, c: (bi, cidx(c), 0)
        full = lambda bi, c: (0, 0)
        in_specs = [pl.BlockSpec((1, R, hk), row), pl.BlockSpec((1, R, hk), row), pl.BlockSpec((1, R, hv), row),
                    pl.BlockSpec((1, R, LANES), row), pl.BlockSpec((LANES, hk), full), pl.BlockSpec((1, hk), full)]
        in_specs += [pl.BlockSpec(sh, row if len(sh) == 3 else full) for sh in extra_shapes]
        return pl.pallas_call(
            functools.partial(_gla_kernel, reverse=reverse),
            grid=(b, nb),
            in_specs=in_specs,
            out_specs=pl.BlockSpec((1, R, hv), row),
            out_shape=jax.ShapeDtypeStruct((b, s, hv), F32),
            scratch_shapes=[pltpu.VMEM((GLA_HEADS, GLA_KDIM, GLA_VDIM), F32)],
            compiler_params=_cparams("parallel", "arbitrary"),
            name="gla_bwd" if reverse else "gla_fwd",
        )(q, k, v, gl, w2, gb, *extra_in)

    o_f = call(False, w2f, gbf, (), ())
    return call(True, w2b, gbb, (o_f, r, gla_norm), ((1, R, hv), (1, R, hv), (1, hv)))


def _sgu_kernel(sg_ref, g_ref, b_ref, ws_ref, bias_ref, o_ref):
    x = sg_ref[...]
    gel = x * (0.5 * (1.0 + jnp.tanh(math.sqrt(2.0 / math.pi) * (x + 0.044715 * (x * x * x)))))
    u = gel[:, :SGU_WIDTH]
    svn = _layernorm(gel[:, SGU_WIDTH:], g_ref[...], b_ref[...]).astype(BF)
    for gi in range(SGU_GROUPS):
        cols = slice(gi * SGU_GROUP_DIM, (gi + 1) * SGU_GROUP_DIM)
        sp = _dot(ws_ref[gi], svn[:, cols]) + bias_ref[:, cols]
        o_ref[:, cols] = u[:, cols] * sp


def _sgu(sg, ln_g, ln_b, ws_bf, bias_full):
    t = sg.shape[0]
    c = SGU_CHUNK
    row = lambda i: (i, 0)
    full = lambda i: (0, 0)
    return pl.pallas_call(
        _sgu_kernel,
        grid=(t // c,),
        in_specs=[pl.BlockSpec((c, 2 * SGU_WIDTH), row),
                  pl.BlockSpec((1, SGU_WIDTH), full), pl.BlockSpec((1, SGU_WIDTH), full),
                  pl.BlockSpec((SGU_GROUPS, c, c), lambda i: (0, 0, 0)),
                  pl.BlockSpec((c, SGU_WIDTH), full)],
        out_specs=pl.BlockSpec((c, SGU_WIDTH), row),
        out_shape=jax.ShapeDtypeStruct((t, SGU_WIDTH), F32),
        compiler_params=_cparams("parallel"),
        name="sgu",
    )(sg, ln_g, ln_b, ws_bf, bias_full)


def _router_kernel(x_ref, wr_ref, route_ref, cnt_ref, base_sc):
    @pl.when(pl.program_id(0) == 0)
    def _():
        base_sc[...] = jnp.zeros(base_sc.shape, F32)

    tr = x_ref.shape[0]
    logits = jnp.dot(x_ref[...], wr_ref[...], precision=HI, preferred_element_type=F32)
    lane = lax.broadcasted_iota(jnp.int32, (tr, LANES), 1).astype(F32)
    lg = jnp.where(lane < N_EXPERTS, logits, -jnp.inf)
    m1 = jnp.max(lg, axis=1, keepdims=True)
    i1 = jnp.min(jnp.where(lg == m1, lane, float(LANES)), axis=1, keepdims=True)
    lg2 = jnp.where(lane == i1, -jnp.inf, lg)
    m2 = jnp.max(lg2, axis=1, keepdims=True)
    i2 = jnp.min(jnp.where(lg2 == m2, lane, float(LANES)), axis=1, keepdims=True)
    e = jnp.exp(m2 - m1)
    g1 = 1.0 / (1.0 + e)
    g2 = e / (1.0 + e)
    oh1 = (lane == i1).astype(F32)
    oh2 = (lane == i2).astype(F32)
    oh = oh1 + oh2
    row_i = lax.broadcasted_iota(jnp.int32, (tr, tr), 0)
    col_i = lax.broadcasted_iota(jnp.int32, (tr, tr), 1)
    before = _dot((col_i < row_i).astype(BF), oh.astype(BF)) + base_sc[...]
    r1 = jnp.sum(oh1 * before, axis=1, keepdims=True)
    r2 = jnp.sum(oh2 * before, axis=1, keepdims=True)
    base_sc[...] += jnp.sum(oh, axis=0, keepdims=True)
    route = jnp.zeros((tr, LANES), F32)
    for idx, val in enumerate((i1, i2, r1, r2, g1, g2)):
        route = jnp.where(lane == float(idx), val, route)
    route_ref[...] = route
    cnt_ref[...] = base_sc[...]


def _router(x, wr_pad):
    t, d = x.shape
    tr = min(ROUTE_TILE, t)
    return pl.pallas_call(
        _router_kernel,
        grid=(t // tr,),
        in_specs=[pl.BlockSpec((tr, d), lambda i: (i, 0)), pl.BlockSpec((d, LANES), lambda i: (0, 0))],
        out_specs=[pl.BlockSpec((tr, LANES), lambda i: (i, 0)), pl.BlockSpec((1, LANES), lambda i: (0, 0))],
        out_shape=[jax.ShapeDtypeStruct((t, LANES), F32), jax.ShapeDtypeStruct((1, LANES), F32)],
        scratch_shapes=[pltpu.VMEM((1, LANES), F32)],
        compiler_params=_cparams("arbitrary"),
        name="router",
    )(x, wr_pad)


def _split_rows(idx):
    r = SC_ROW_SPLIT
    return (idx[:, None] * r + jnp.arange(r, dtype=jnp.int32)[None, :]).reshape(1, -1)


def _sc_scatter_rows(x, idx, n_out):
    t, d = x.shape
    w = d // SC_ROW_SPLIT
    xr = x.reshape(t * SC_ROW_SPLIT, w)
    ir = _split_rows(idx)
    nblk = xr.shape[0] // SC_WINDOW
    mesh = plsc.VectorSubcoreMesh(core_axis_name="c", subcore_axis_name="s")

    @pl.kernel(out_type=jax.ShapeDtypeStruct((n_out * SC_ROW_SPLIT, w), x.dtype), mesh=mesh)
    def k(x_hbm, i_hbm, o_hbm):
        def body(x_vmem, i_vmem):
            pltpu.sync_copy(x_vmem, o_hbm.at[i_vmem.at[0]])

        pltpu.emit_pipeline(
            body,
            grid=(ir.shape[1] // SC_WINDOW,),
            in_specs=[pl.BlockSpec((SC_WINDOW, w), index_map=lambda i: (i % nblk, 0)),
                      pl.BlockSpec((1, SC_WINDOW), index_map=lambda i: (0, i))],
            out_specs=[],
            core_axis_name=("c", "s"),
            dimension_semantics=(pltpu.PARALLEL,),
        )(x_hbm, i_hbm)

    return k(xr, ir).reshape(n_out, d)


def _sc_gather_rows(x, idx):
    n, d = x.shape
    w = d // SC_ROW_SPLIT
    xr = x.reshape(n * SC_ROW_SPLIT, w)
    ir = _split_rows(idx)
    mesh = plsc.VectorSubcoreMesh(core_axis_name="c", subcore_axis_name="s")

    @pl.kernel(out_type=jax.ShapeDtypeStruct((ir.shape[1], w), x.dtype), mesh=mesh)
    def k(x_hbm, i_hbm, o_hbm):
        def body(i_vmem, o_vmem):
            pltpu.sync_copy(x_hbm.at[i_vmem.at[0]], o_vmem)

        pltpu.emit_pipeline(
            body,
            grid=(ir.shape[1] // SC_WINDOW,),
            in_specs=[pl.BlockSpec((1, SC_WINDOW), index_map=lambda i: (0, i))],
            out_specs=[pl.BlockSpec((SC_WINDOW, w), index_map=lambda i: (i, 0))],
            core_axis_name=("c", "s"),
            dimension_semantics=(pltpu.PARALLEL,),
        )(i_hbm, o_hbm)

    return k(xr, ir).reshape(idx.shape[0], d)


def _combine_ln_kernel(x_ref, y_ref, route_ref, g_ref, beta_ref, o_ref):
    g1 = route_ref[:, 4:5]
    g2 = route_ref[:, 5:6]
    y = g1 * y_ref[0] + g2 * y_ref[1]
    o_ref[...] = _layernorm(DN_ALPHA * x_ref[...] + y, g_ref[...], beta_ref[...])


def _combine_ln(x, y2, route, g, beta):
    t, d = x.shape
    tm = min(ROW_TILE, t)
    row = lambda i: (i, 0)
    full = lambda i: (0, 0)
    return pl.pallas_call(
        _combine_ln_kernel,
        grid=(t // tm,),
        in_specs=[pl.BlockSpec((tm, d), row), pl.BlockSpec((2, tm, d), lambda i: (0, i, 0)),
                  pl.BlockSpec((tm, LANES), row), pl.BlockSpec((1, d), full), pl.BlockSpec((1, d), full)],
        out_specs=pl.BlockSpec((tm, d), row),
        out_shape=jax.ShapeDtypeStruct((t, d), F32),
        compiler_params=_cparams("parallel"),
        name="moe_combine_ln",
    )(x, y2, route, g, beta)


def _pad_cols(w, n):
    return jnp.pad(w, ((0, 0), (0, n - w.shape[1])))


def _prep_even(p):
    (w_in, q_norm, w_uq, kv_norm, w_ukv, conv_w, conv_b, dt_bias, a_log, d_skip, ssm_norm, w_out,
     ln1_g, ln1_b, w_gate, w_up, w_down, ln2_g, ln2_b) = p
    o = 0
    cq = w_in[:, o:o + Q_RANK]; o += Q_RANK
    ckv = w_in[:, o:o + KV_RANK]; o += KV_RANK
    kr = w_in[:, o:o + QK_ROPE]; o += QK_ROPE
    z = w_in[:, o:o + SSD_INNER]; o += SSD_INNER
    xbc = w_in[:, o:o + CONV_DIM]; o += CONV_DIM
    dt = w_in[:, o:]
    half = QK_ROPE // 2
    zeros = lambda n: jnp.zeros((w_in.shape[0], n), F32)
    kra = jnp.concatenate([zeros(QK_NOPE), kr, zeros(LANES - QK_NOPE - QK_ROPE)], axis=1)
    krb = jnp.concatenate([zeros(QK_NOPE), -kr[:, half:], kr[:, :half], zeros(LANES - QK_NOPE - QK_ROPE)], axis=1)
    w_in_p = jnp.concatenate([cq, ckv, kra, krb, z, xbc, _pad_cols(dt, LANES)], axis=1).astype(BF)

    wq = w_uq.reshape(Q_RANK, MLA_HEADS, QK_NOPE + QK_ROPE)
    nope, rope = wq[..., :QK_NOPE], wq[..., QK_NOPE:]
    zq = lambda n: jnp.zeros((Q_RANK, MLA_HEADS, n), F32)
    wqa = jnp.concatenate([nope, rope, zq(LANES - QK_NOPE - QK_ROPE)], axis=-1)
    wqb = jnp.concatenate([zq(QK_NOPE), -rope[..., half:], rope[..., :half], zq(LANES - QK_NOPE - QK_ROPE)], axis=-1)
    wkv = w_ukv.reshape(KV_RANK, MLA_HEADS, QK_NOPE + V_DIM)
    zk = jnp.zeros((KV_RANK, MLA_HEADS, LANES - QK_NOPE), F32)
    wk = jnp.concatenate([wkv[..., :QK_NOPE], zk], axis=-1)
    vv = wkv[..., QK_NOPE:]
    zv = jnp.zeros_like(vv)
    odd = (jnp.arange(MLA_HEADS) % 2 == 1)[None, :, None]
    wv = jnp.where(odd, jnp.concatenate([zv, vv], axis=-1), jnp.concatenate([vv, zv], axis=-1))
    lane = jnp.arange(LANES)[None, :]
    ones_lane = jnp.where(jnp.arange(MLA_HEADS)[:, None] % 2 == 1, lane == 0, lane == V_DIM)
    vadd = ones_lane.astype(F32).reshape(1, MLA_HEADS * LANES)
    hw = MLA_HEADS * LANES
    return dict(
        w_in=w_in_p, q_norm=q_norm[None], kv_norm=kv_norm[None],
        wqa=wqa.reshape(Q_RANK, hw).astype(BF), wqb=wqb.reshape(Q_RANK, hw).astype(BF),
        wk=wk.reshape(KV_RANK, hw).astype(BF), wv=wv.reshape(KV_RANK, hw).astype(BF), vadd=vadd,
        conv_w=conv_w, conv_b=conv_b[None],
        dt_bias=_pad_cols(dt_bias.reshape(1, -1), LANES), a_log=_pad_cols(a_log.reshape(1, -1), LANES),
        d_skip=jnp.repeat(d_skip, SSD_HEAD_DIM)[None], ssm_norm=ssm_norm[None],
        wo_a=w_out[:MLA_HEADS * V_DIM].astype(BF), wo_b=w_out[MLA_HEADS * V_DIM:].astype(BF),
        ln1_g=ln1_g[None], ln1_b=ln1_b[None],
        wg=w_gate.astype(BF), wu=w_up.astype(BF), wd=w_down.astype(BF),
        ln2_g=ln2_g[None], ln2_b=ln2_b[None])


def _prep_odd(p):
    (w_in, gate_w2, gate_b, gla_norm, sgu_ln_g, sgu_ln_b, w_s, b_s, w_out, ln1_g, ln1_b,
     w_router, we_gate, we_up, we_down, ln2_g, ln2_b) = p
    hk = GLA_HEADS * GLA_KDIM
    hv = GLA_HEADS * GLA_VDIM
    o = 2 * hk + 2 * hv
    gl = w_in[:, o:o + 2 * GLA_GATE_RANK]
    w_in_p = jnp.concatenate([w_in[:, :o], _pad_cols(gl, LANES), w_in[:, o + 2 * GLA_GATE_RANK:]], axis=1).astype(BF)
    zr = lambda n: jnp.zeros((n, hk), F32)
    w2f = jnp.concatenate([gate_w2[0], zr(LANES - GLA_GATE_RANK)], axis=0)
    w2b = jnp.concatenate([zr(GLA_GATE_RANK), gate_w2[1], zr(LANES - 2 * GLA_GATE_RANK)], axis=0)
    bias_full = jnp.repeat(b_s.T, SGU_GROUP_DIM, axis=1)
    return dict(
        w_in=w_in_p, w2f=w2f, w2b=w2b, gbf=gate_b[0][None], gbb=gate_b[1][None], gla_norm=gla_norm[None],
        sgu_g=sgu_ln_g[None], sgu_b=sgu_ln_b[None], ws=w_s.astype(BF), sgu_bias=bias_full,
        wo_a=w_out[:hv].astype(BF), wo_b=w_out[hv:].astype(BF), ln1_g=ln1_g[None], ln1_b=ln1_b[None],
        w_router=_pad_cols(w_router, LANES),
        wg=we_gate.astype(BF), wu=we_up.astype(BF), wd=we_down.astype(BF),
        ln2_g=ln2_g[None], ln2_b=ln2_b[None])


def _rope_tables(s):
    half = QK_ROPE // 2
    inv = jnp.exp(-math.log(ROPE_THETA) * jnp.arange(half, dtype=F32) / half)
    ang = jnp.arange(s, dtype=F32)[:, None] * inv[None, :]
    cos, sin = jnp.cos(ang), jnp.sin(ang)
    pad = LANES - QK_NOPE - QK_ROPE
    cos_t = jnp.concatenate([jnp.ones((s, QK_NOPE), F32), cos, cos, jnp.ones((s, pad), F32)], axis=1)
    sin_t = jnp.concatenate([jnp.zeros((s, QK_NOPE), F32), sin, sin, jnp.zeros((s, pad), F32)], axis=1)
    return cos_t, sin_t


def _even_layer(x, w, b, s):
    t = b * s
    mla_in, z, xbc, dt = _proj(x, w["w_in"], (Q_RANK + KV_RANK + 2 * LANES, SSD_INNER, CONV_DIM, LANES),
                               (F32, F32, F32, F32))
    cos_t, sin_t = _rope_tables(s)
    q, k, v = _mla_prep(mla_in, cos_t, sin_t, w["q_norm"], w["kv_norm"], w["wqa"], w["wqb"], w["wk"], w["wv"],
                        w["vadd"], s)
    hw = MLA_HEADS * LANES
    o_attn = _flash(q.reshape(b, s, hw), k.reshape(b, s, hw), v.reshape(b, s, hw))
    y = _ssd(xbc.reshape(b, s, CONV_DIM), dt.reshape(b, s, LANES), z.reshape(b, s, SSD_INNER),
             w["conv_w"], w["conv_b"], w["dt_bias"], w["a_log"], w["d_skip"], w["ssm_norm"])
    x1 = _outproj_ln(o_attn.reshape(t, -1), y.reshape(t, -1), x, w["wo_a"], w["wo_b"], w["ln1_g"], w["ln1_b"])
    return _ffn_ln(x1, w["wg"], w["wu"], w["wd"], w["ln2_g"], w["ln2_b"])


def _moe(x1, w):
    t, d = x1.shape
    route, cnt = _router(x1, w["w_router"])
    eid = route[:, 0:2].astype(jnp.int32)
    rank = route[:, 2:4].astype(jnp.int32)
    counts = cnt[0, :N_EXPERTS].astype(jnp.int32)
    tm = MOE_TILE
    padded = ((counts + tm - 1) // tm) * tm
    ends = jnp.cumsum(padded)
    offs = ends - padded
    pos = offs[eid] + rank
    sc_rows = SC_WINDOW * 32 // SC_ROW_SPLIT
    p_rows = -(-(2 * t + N_EXPERTS * tm) // sc_rows) * sc_rows
    p_rows = -(-p_rows // tm) * tm
    tiles = jnp.arange(p_rows // tm, dtype=jnp.int32)
    tile_ends = ends // tm
    tile_expert = jnp.minimum(jnp.sum(tiles[:, None] >= tile_ends[None, :], axis=1), N_EXPERTS - 1).astype(jnp.int32)
    tile_valid = (tiles < tile_ends[-1]).astype(jnp.int32)
    flat_pos = pos.T.reshape(-1)
    xs = _sc_scatter_rows(x1, flat_pos, p_rows)
    ys = _moe_ffn(xs, tile_expert, tile_valid, w["wg"], w["wu"], w["wd"])
    y2 = _sc_gather_rows(ys, flat_pos).reshape(2, t, d)
    return _combine_ln(x1, y2, route, w["ln2_g"], w["ln2_b"])


def _odd_layer(x, w, b, s):
    t = b * s
    hk = GLA_HEADS * GLA_KDIM
    hv = GLA_HEADS * GLA_VDIM
    q, k, v, r, gl, sg = _proj(x, w["w_in"], (hk, hk, hv, hv, LANES, 2 * SGU_WIDTH), (F32,) * 6)
    o = _gla(q.reshape(b, s, hk), k.reshape(b, s, hk), v.reshape(b, s, hv), gl.reshape(b, s, LANES),
             r.reshape(b, s, hv), w["w2f"], w["w2b"], w["gbf"], w["gbb"], w["gla_norm"])
    o_sgu = _sgu(sg, w["sgu_g"], w["sgu_b"], w["ws"], w["sgu_bias"])
    x1 = _outproj_ln(o.reshape(t, hv), o_sgu, x, w["wo_a"], w["wo_b"], w["ln1_g"], w["ln1_b"])
    return _moe(x1, w)


def _trunk(x, ev_w, od_w):
    b, s, d = x.shape
    x = x.reshape(b * s, d)
    for i in range(DEPTH):
        if i % 2 == 0:
            x = _even_layer(x, ev_w[i // 2], b, s)
        else:
            x = _odd_layer(x, od_w[i // 2], b, s)
    return x.reshape(b, s, d)


def kernel(x_prompt, x_sample, ev_w_in, ev_q_norm, ev_w_uq, ev_kv_norm, ev_w_ukv, ev_conv_w, ev_conv_b, ev_dt_bias, ev_a_log, ev_d_skip, ev_ssm_norm, ev_w_out, ev_ln1_g, ev_ln1_b, ev_w_gate, ev_w_up, ev_w_down, ev_ln2_g, ev_ln2_b, od_w_in, od_gate_w2, od_gate_b, od_gla_norm, od_sgu_ln_g, od_sgu_ln_b, od_w_s, od_b_s, od_w_out, od_ln1_g, od_ln1_b, od_w_router, od_we_gate, od_we_up, od_we_down, od_ln2_g, od_ln2_b):
    ev = (ev_w_in, ev_q_norm, ev_w_uq, ev_kv_norm, ev_w_ukv, ev_conv_w, ev_conv_b, ev_dt_bias,
          ev_a_log, ev_d_skip, ev_ssm_norm, ev_w_out, ev_ln1_g, ev_ln1_b, ev_w_gate, ev_w_up,
          ev_w_down, ev_ln2_g, ev_ln2_b)
    od = (od_w_in, od_gate_w2, od_gate_b, od_gla_norm, od_sgu_ln_g, od_sgu_ln_b, od_w_s, od_b_s,
          od_w_out, od_ln1_g, od_ln1_b, od_w_router, od_we_gate, od_we_up, od_we_down,
          od_ln2_g, od_ln2_b)
    ev_w = [_prep_even(tuple(t[i] for t in ev)) for i in range(ev_w_in.shape[0])]
    od_w = [_prep_odd(tuple(t[i] for t in od)) for i in range(od_w_in.shape[0])]
    return (_trunk(x_prompt, ev_w, od_w), _trunk(x_sample, ev_w, od_w))
```

```python
import functools
import math

import jax
import jax.numpy as jnp
from jax import lax
from jax.experimental import pallas as pl
from jax.experimental.pallas import tpu as pltpu
from jax.experimental.pallas import tpu_sc as plsc

BF = jnp.bfloat16
F32 = jnp.float32
HI = lax.Precision.HIGHEST

D_MODEL = 1024
DEPTH = 4
MLA_HEADS = 8
QK_NOPE = 64
QK_ROPE = 32
V_DIM = 64
Q_RANK = 256
KV_RANK = 128
ROPE_THETA = 10000.0
SSD_HEADS = 8
SSD_HEAD_DIM = 64
SSD_GROUPS = 2
SSD_STATE = 64
SSD_CHUNK = 128
CONV_K = 5
SSD_INNER = SSD_HEADS * SSD_HEAD_DIM
SSD_HPG = SSD_HEADS // SSD_GROUPS
CONV_DIM = SSD_INNER + 2 * SSD_GROUPS * SSD_STATE
GLA_HEADS = 4
GLA_KDIM = 64
GLA_VDIM = 128
GLA_GATE_RANK = 16
GLA_TAU = 16.0
GLA_CHUNK = 64
SGU_GROUPS = 4
SGU_CHUNK = 128
SGU_GROUP_DIM = 128
SGU_WIDTH = SGU_GROUPS * SGU_GROUP_DIM
D_FF = 2816
N_EXPERTS = 8
D_FF_EXPERT = 3584
DN_ALPHA = (2 * DEPTH) ** 0.25
EPS = 1e-5

LANES = 128
SUBLANES = 8
VMEM_LIMIT = 56 * 1024 * 1024
SC_WINDOW = 128
SC_ROW_SPLIT = 4

ROW_TILE = 512
FF_CHUNK = 256
ATT_TQ = 512
ATT_TK = 512
MOE_TILE = 512
ROUTE_TILE = 512
GLA_ROWS = 2 * GLA_CHUNK
V_ROWS = 80


def _cparams(*sem):
    return pltpu.CompilerParams(dimension_semantics=sem, vmem_limit_bytes=VMEM_LIMIT)


def _resident(shape, index_map):
    return pl.BlockSpec(shape, index_map, pipeline_mode=pl.Buffered(1))


def _rms(x):
    return x * lax.rsqrt(jnp.mean(x * x, axis=-1, keepdims=True) + EPS)


def _layernorm(x, g, b):
    mu = jnp.mean(x, axis=-1, keepdims=True)
    xc = x - mu
    var = jnp.mean(xc * xc, axis=-1, keepdims=True)
    return xc * lax.rsqrt(var + EPS) * g + b


def _silu(x):
    return x * jax.nn.sigmoid(x)


def _softplus(x):
    return jnp.maximum(x, 0.0) + jnp.log1p(jnp.exp(-jnp.abs(x)))


def _dot(a, b):
    return jnp.dot(a, b, preferred_element_type=F32)


def _dot_nt(a, b):
    return lax.dot_general(a, b, (((1,), (1,)), ((), ())), preferred_element_type=F32)


def _proj_kernel(x_ref, w_ref, *o_refs, splits):
    xb = x_ref[...].astype(BF)
    off = 0
    for o_ref, n in zip(o_refs, splits):
        o_ref[...] = _dot(xb, w_ref[:, off:off + n]).astype(o_ref.dtype)
        off += n


def _proj(x2d, w_bf, splits, dtypes):
    t, k = x2d.shape
    n = w_bf.shape[1]
    tm = min(ROW_TILE, t)
    return pl.pallas_call(
        functools.partial(_proj_kernel, splits=splits),
        grid=(t // tm,),
        in_specs=[pl.BlockSpec((tm, k), lambda i: (i, 0)), _resident((k, n), lambda i: (0, 0))],
        out_specs=[pl.BlockSpec((tm, s), lambda i: (i, 0)) for s in splits],
        out_shape=[jax.ShapeDtypeStruct((t, s), d) for s, d in zip(splits, dtypes)],
        compiler_params=_cparams("parallel"),
        name="proj_in",
    )(x2d, w_bf)


def _mla_prep_kernel(m_ref, cos_ref, sin_ref, qn_ref, kvn_ref, wqa_ref, wqb_ref, wk_ref, wvt_ref, vadd_ref,
                     q_ref, k_ref, vt_ref):
    cq = m_ref[:, 0:Q_RANK]
    ckv = m_ref[:, Q_RANK:Q_RANK + KV_RANK]
    kra = m_ref[:, Q_RANK + KV_RANK:Q_RANK + KV_RANK + LANES]
    krb = m_ref[:, Q_RANK + KV_RANK + LANES:Q_RANK + KV_RANK + 2 * LANES]
    cos = cos_ref[...]
    sin = sin_ref[...]
    cos8 = jnp.concatenate([cos] * MLA_HEADS, axis=1)
    sin8 = jnp.concatenate([sin] * MLA_HEADS, axis=1)
    cqn = (_rms(cq) * qn_ref[...]).astype(BF)
    q = _dot(cqn, wqa_ref[...]) * cos8 + _dot(cqn, wqb_ref[...]) * sin8
    q_ref[...] = (q * ((QK_NOPE + QK_ROPE) ** -0.5 * math.log2(math.e))).astype(BF)
    ckvn = (_rms(ckv) * kvn_ref[...]).astype(BF)
    kr = kra * cos + krb * sin
    k = _dot(ckvn, wk_ref[...]) + jnp.concatenate([kr] * MLA_HEADS, axis=1)
    k_ref[...] = k.astype(BF)
    vt_ref[0] = (_dot_nt(wvt_ref[...], ckvn) + vadd_ref[...]).astype(BF)


def _mla_prep(mla_in, cos_t, sin_t, q_norm, kv_norm, wqa, wqb, wk, wvt, vadd, batch, seq):
    t = mla_in.shape[0]
    tm = min(ROW_TILE, seq)
    nseq = seq // tm
    hw = MLA_HEADS * LANES
    vr = MLA_HEADS * V_ROWS
    full = lambda i: (0, 0)
    return pl.pallas_call(
        _mla_prep_kernel,
        grid=(t // tm,),
        in_specs=[pl.BlockSpec((tm, mla_in.shape[1]), lambda i: (i, 0)),
                  pl.BlockSpec((tm, LANES), lambda i: (i % nseq, 0)),
                  pl.BlockSpec((tm, LANES), lambda i: (i % nseq, 0)),
                  pl.BlockSpec((1, Q_RANK), full), pl.BlockSpec((1, KV_RANK), full),
                  pl.BlockSpec((Q_RANK, hw), full), pl.BlockSpec((Q_RANK, hw), full),
                  pl.BlockSpec((KV_RANK, hw), full), pl.BlockSpec((vr, KV_RANK), full),
                  pl.BlockSpec((vr, 1), full)],
        out_specs=[pl.BlockSpec((tm, hw), lambda i: (i, 0)), pl.BlockSpec((tm, hw), lambda i: (i, 0)),
                   pl.BlockSpec((1, vr, tm), lambda i: (i // nseq, 0, i % nseq))],
        out_shape=[jax.ShapeDtypeStruct((t, hw), BF), jax.ShapeDtypeStruct((t, hw), BF),
                   jax.ShapeDtypeStruct((batch, vr, seq), BF)],
        compiler_params=_cparams("parallel"),
        name="mla_prep",
    )(mla_in, cos_t, sin_t, q_norm, kv_norm, wqa, wqb, wk, wvt, vadd)


def _flash_kernel(q_ref, k_ref, vt_ref, o_ref, m_sc, acc_sc, s_sc, *, tk, nk, unroll):
    for h in range(2):
        m_sc[h] = jnp.full(m_sc.shape[1:], -jnp.inf, F32)
        acc_sc[h] = jnp.zeros(acc_sc.shape[1:], F32)

    def scores(j, slot):
        off = pl.multiple_of(j * tk, tk)
        for h in range(2):
            lanes = slice(h * LANES, (h + 1) * LANES)
            s_sc[slot, h] = _dot_nt(k_ref[0, pl.ds(off, tk), lanes], q_ref[0, :, lanes])

    def consume(j, slot):
        off = pl.multiple_of(j * tk, tk)
        for h in range(2):
            st = s_sc[slot, h]
            m_prev = m_sc[h]
            m_new = jnp.maximum(m_prev, jnp.max(st, axis=0, keepdims=True))
            p = jnp.exp2(st - m_new[0:1, :]).astype(BF)
            alpha = jnp.exp2(m_prev - m_new)
            pv = _dot(vt_ref[0, h * V_ROWS:(h + 1) * V_ROWS, pl.ds(off, tk)], p)
            acc_sc[h] = alpha[0:1, :] * acc_sc[h] + pv
            m_sc[h] = m_new

    scores(0, 0)

    def body(jj, carry):
        j = unroll * jj
        for u in range(unroll):
            scores(jnp.minimum(j + u + 1, nk - 1), (u + 1) % 2)
            consume(j + u, u % 2)
        return carry

    lax.fori_loop(0, nk // unroll, body, 0)
    outs = []
    for h in range(2):
        acc = acc_sc[h]
        outs.append(acc[0:V_DIM, :] / acc[V_DIM:V_DIM + 1, :])
    o_ref[0] = jnp.concatenate(outs, axis=0).T


def _flash(q, k, vt):
    b, s, hw = q.shape
    tq = min(ATT_TQ, s)
    tk = min(ATT_TK, s)
    nk = s // tk
    unroll = 4 if nk % 4 == 0 else (2 if nk % 2 == 0 else 1)
    pairs = MLA_HEADS // 2
    return pl.pallas_call(
        functools.partial(_flash_kernel, tk=tk, nk=nk, unroll=unroll),
        grid=(b, pairs, s // tq),
        in_specs=[pl.BlockSpec((1, tq, 2 * LANES), lambda bi, hp, i: (bi, i, hp)),
                  pl.BlockSpec((1, s, 2 * LANES), lambda bi, hp, i: (bi, 0, hp)),
                  pl.BlockSpec((1, 2 * V_ROWS, s), lambda bi, hp, i: (bi, hp, 0))],
        out_specs=pl.BlockSpec((1, tq, LANES), lambda bi, hp, i: (bi, i, hp)),
        out_shape=jax.ShapeDtypeStruct((b, s, MLA_HEADS * V_DIM), F32),
        scratch_shapes=[pltpu.VMEM((2, SUBLANES, tq), F32), pltpu.VMEM((2, V_ROWS, tq), F32),
                        pltpu.VMEM((2, 2, tk, tq), F32)],
        compiler_params=_cparams("parallel", "parallel", "arbitrary"),
        name="mla_flash",
    )(q, k, vt)


def _ssd_kernel(*refs, reverse, nc):
    if reverse:
        (xc_ref, xp_ref, xn_ref, dt_ref, cw_ref, cb_ref, dtb_ref, alog_ref,
         yf_ref, z_ref, dskip_ref, nrm_ref, o_ref, xe_sc, st_sc) = refs
    else:
        (xc_ref, xp_ref, xn_ref, dt_ref, cw_ref, cb_ref, dtb_ref, alog_ref, o_ref, xe_sc, st_sc) = refs
    L = SSD_CHUNK
    N = SSD_STATE
    P = SSD_HEAD_DIM
    c = pl.program_id(1)
    cc = (nc - 1 - c) if reverse else c

    @pl.when(c == 0)
    def _():
        st_sc[...] = jnp.zeros(st_sc.shape, F32)

    xe_sc[0:SUBLANES, :] = jnp.where(cc > 0, xp_ref[0], 0.0)
    xe_sc[SUBLANES:SUBLANES + L, :] = xc_ref[0]
    xe_sc[SUBLANES + L:2 * SUBLANES + L, :] = jnp.where(cc < nc - 1, xn_ref[0], 0.0)
    conv = cb_ref[...] + cw_ref[0:1, :] * xe_sc[pl.ds(SUBLANES - CONV_K // 2, L), :]
    for j in range(1, CONV_K):
        conv = conv + cw_ref[j:j + 1, :] * xe_sc[pl.ds(SUBLANES - CONV_K // 2 + j, L), :]
    xbc = _silu(conv)
    xs = xbc[:, :SSD_INNER]
    bc = xbc[:, SSD_INNER:]
    bc_t = bc.T

    lane = lax.broadcasted_iota(jnp.int32, (L, LANES), 1)
    dtv = _softplus(dt_ref[0] + dtb_ref[...])
    a = jnp.where(lane[0:1] < 2 * SSD_HEADS, -jnp.exp(alog_ref[...]), 0.0)
    dta = dtv * a
    row_i = lax.broadcasted_iota(jnp.int32, (L, L), 0)
    col_i = lax.broadcasted_iota(jnp.int32, (L, L), 1)
    causal = (col_i >= row_i) if reverse else (col_i <= row_i)
    acs = jnp.dot(causal.astype(F32), dta, precision=HI, preferred_element_type=F32)
    acs_t = acs.T
    d0 = SSD_HEADS if reverse else 0
    end = 0 if reverse else L - 1

    ys = []
    for g in range(SSD_GROUPS):
        bm_g = bc[:, g * N:(g + 1) * N].astype(BF)
        cm_g = bc[:, (SSD_GROUPS + g) * N:(SSD_GROUPS + g + 1) * N].astype(BF)
        bm_t_g = bc_t[g * N:(g + 1) * N, :].astype(BF)
        cb = _dot_nt(cm_g, bm_g)
        for jh in range(SSD_HPG):
            h = g * SSD_HPG + jh
            col = acs[:, d0 + h:d0 + h + 1]
            row = acs_t[d0 + h:d0 + h + 1, :]
            decay = jnp.exp(jnp.where(causal, col - row, -jnp.inf))
            xdt = xs[:, h * P:(h + 1) * P] * dtv[:, d0 + h:d0 + h + 1]
            y_diag = _dot((cb * decay).astype(BF), xdt.astype(BF))
            tot = acs[end:end + 1, d0 + h:d0 + h + 1]
            xw = xdt * jnp.exp(tot - col)
            states_t = _dot(bm_t_g, xw.astype(BF))
            prev_t = st_sc[h]
            y_off = _dot(cm_g, prev_t.astype(BF)) * jnp.exp(col)
            st_sc[h] = prev_t * jnp.exp(tot) + states_t
            ys.append(y_diag + y_off)
    y = jnp.concatenate(ys, axis=1)
    if reverse:
        y = y + yf_ref[0] + dskip_ref[...] * xs
        y = _rms(y * _silu(z_ref[0])) * nrm_ref[...]
    o_ref[0] = y.astype(o_ref.dtype)


def _ssd(xbc, dt, z, conv_w, conv_b, dt_bias, a_log, d_skip, ssm_norm):
    b, s, _ = xbc.shape
    L = SSD_CHUNK
    nc = s // L
    hb = L // SUBLANES
    nhb = s // SUBLANES

    def call(reverse, extra_in, extra_specs, out_dtype):
        cidx = (lambda c: nc - 1 - c) if reverse else (lambda c: c)
        row = lambda bi, c: (bi, cidx(c), 0)
        full = lambda bi, c: (0, 0)
        in_specs = [
            pl.BlockSpec((1, L, CONV_DIM), row),
            pl.BlockSpec((1, SUBLANES, CONV_DIM), lambda bi, c: (bi, jnp.maximum(cidx(c) * hb - 1, 0), 0)),
            pl.BlockSpec((1, SUBLANES, CONV_DIM), lambda bi, c: (bi, jnp.minimum((cidx(c) + 1) * hb, nhb - 1), 0)),
            pl.BlockSpec((1, L, LANES), row),
            pl.BlockSpec((CONV_K, CONV_DIM), full), pl.BlockSpec((1, CONV_DIM), full),
            pl.BlockSpec((1, LANES), full), pl.BlockSpec((1, LANES), full),
        ] + [pl.BlockSpec(sh, row if len(sh) == 3 else full) for sh in extra_specs]
        return pl.pallas_call(
            functools.partial(_ssd_kernel, reverse=reverse, nc=nc),
            grid=(b, nc),
            in_specs=in_specs,
            out_specs=pl.BlockSpec((1, L, SSD_INNER), row),
            out_shape=jax.ShapeDtypeStruct((b, s, SSD_INNER), out_dtype),
            scratch_shapes=[pltpu.VMEM((L + 2 * SUBLANES, CONV_DIM), F32),
                            pltpu.VMEM((SSD_HEADS, SSD_STATE, SSD_HEAD_DIM), F32)],
            compiler_params=_cparams("parallel", "arbitrary"),
            name="ssd_bwd" if reverse else "ssd_fwd",
        )(xbc, xbc, xbc, dt, conv_w, conv_b, dt_bias, a_log, *extra_in)

    y_f = call(False, (), (), F32)
    return call(True, (y_f, z, d_skip, ssm_norm),
                ((1, L, SSD_INNER), (1, L, SSD_INNER), (1, SSD_INNER), (1, SSD_INNER)), F32)


def _to_pieces(ref, y):
    w = ref.shape[-1]
    for j in range(SC_ROW_SPLIT):
        ref[j] = y[:, j * w:(j + 1) * w].astype(ref.dtype)


def _from_pieces(ref):
    return jnp.concatenate([ref[j] for j in range(SC_ROW_SPLIT)], axis=1)


def _outproj_ln_kernel(a_ref, b_ref, x_ref, wa_ref, wb_ref, g_ref, beta_ref, o_ref, *p_refs):
    y = _dot(a_ref[...].astype(BF), wa_ref[...]) + _dot(b_ref[...].astype(BF), wb_ref[...])
    out = _layernorm(DN_ALPHA * x_ref[...] + y, g_ref[...], beta_ref[...])
    o_ref[...] = out
    for p_ref in p_refs:
        _to_pieces(p_ref, out)


def _outproj_ln(a, b, x, wa, wb, g, beta, pieces=False):
    t, d = x.shape
    tm = min(ROW_TILE, t)
    row = lambda i: (i, 0)
    full = lambda i: (0, 0)
    out_specs = [pl.BlockSpec((tm, d), row)]
    out_shape = [jax.ShapeDtypeStruct((t, d), F32)]
    if pieces:
        out_specs.append(pl.BlockSpec((SC_ROW_SPLIT, tm, d // SC_ROW_SPLIT), lambda i: (0, i, 0)))
        out_shape.append(jax.ShapeDtypeStruct((SC_ROW_SPLIT, t, d // SC_ROW_SPLIT), F32))
    res = pl.pallas_call(
        _outproj_ln_kernel,
        grid=(t // tm,),
        in_specs=[pl.BlockSpec((tm, a.shape[1]), row), pl.BlockSpec((tm, b.shape[1]), row),
                  pl.BlockSpec((tm, d), row),
                  _resident(wa.shape, full), _resident(wb.shape, full),
                  pl.BlockSpec((1, d), full), pl.BlockSpec((1, d), full)],
        out_specs=out_specs,
        out_shape=out_shape,
        compiler_params=_cparams("parallel"),
        name="outproj_ln",
    )(a, b, x, wa, wb, g, beta)
    return res if pieces else res[0]


def _swiglu_acc(xb, wg_ref, wu_ref, wd_ref, acc_sc, nf):
    for f in range(nf):
        cols = slice(f * FF_CHUNK, (f + 1) * FF_CHUNK)
        h = _silu(_dot(xb, wg_ref[:, cols])) * _dot(xb, wu_ref[:, cols])
        part = _dot(h.astype(BF), wd_ref[cols, :])
        if f == 0:
            acc_sc[...] = part
        else:
            acc_sc[...] += part


def _ffn_ln_kernel(x_ref, wg_ref, wu_ref, wd_ref, g_ref, beta_ref, o_ref, acc_sc, *, nf):
    x = x_ref[...]
    _swiglu_acc(x.astype(BF), wg_ref, wu_ref, wd_ref, acc_sc, nf)
    o_ref[...] = _layernorm(DN_ALPHA * x + acc_sc[...], g_ref[...], beta_ref[...])


def _ffn_ln(x, wg, wu, wd, g, beta):
    t, d = x.shape
    f = wg.shape[1]
    tm = min(ROW_TILE, t)
    row = lambda i: (i, 0)
    full = lambda i: (0, 0)
    return pl.pallas_call(
        functools.partial(_ffn_ln_kernel, nf=f // FF_CHUNK),
        grid=(t // tm,),
        in_specs=[pl.BlockSpec((tm, d), row),
                  _resident((d, f), full), _resident((d, f), full), _resident((f, d), full),
                  pl.BlockSpec((1, d), full), pl.BlockSpec((1, d), full)],
        out_specs=pl.BlockSpec((tm, d), row),
        out_shape=jax.ShapeDtypeStruct((t, d), F32),
        scratch_shapes=[pltpu.VMEM((tm, d), F32)],
        compiler_params=_cparams("parallel"),
        name="ffn_ln",
    )(x, wg, wu, wd, g, beta)


def _moe_ffn_kernel(te_ref, tv_ref, xs_ref, wg_ref, wu_ref, wd_ref, o_ref, acc_sc, *, nf):
    @pl.when(tv_ref[pl.program_id(0)] > 0)
    def _():
        _swiglu_acc(_from_pieces(xs_ref).astype(BF), wg_ref, wu_ref, wd_ref, acc_sc, nf)
        _to_pieces(o_ref, acc_sc[...])


def _moe_ffn(xs, tile_expert, tile_valid, wg, wu, wd):
    ns, p, w = xs.shape
    d = ns * w
    f = wg.shape[2]
    tm = MOE_TILE
    row = lambda i, te, tv: (0, i, 0)
    grid_spec = pltpu.PrefetchScalarGridSpec(
        num_scalar_prefetch=2,
        grid=(p // tm,),
        in_specs=[pl.BlockSpec((ns, tm, w), row),
                  _resident((None, d, f), lambda i, te, tv: (te[i], 0, 0)),
                  _resident((None, d, f), lambda i, te, tv: (te[i], 0, 0)),
                  _resident((None, f, d), lambda i, te, tv: (te[i], 0, 0))],
        out_specs=pl.BlockSpec((ns, tm, w), row),
        scratch_shapes=[pltpu.VMEM((tm, d), F32)],
    )
    return pl.pallas_call(
        functools.partial(_moe_ffn_kernel, nf=f // FF_CHUNK),
        grid_spec=grid_spec,
        out_shape=jax.ShapeDtypeStruct((ns, p, w), F32),
        compiler_params=_cparams("arbitrary"),
        name="moe_ffn",
    )(tile_expert, tile_valid, xs, wg, wu, wd)


def _gla_kernel(*refs, reverse):
    if reverse:
        q_ref, k_ref, v_ref, gl_ref, w2_ref, gb_ref, of_ref, r_ref, gn_ref, o_ref, st_sc = refs
    else:
        q_ref, k_ref, v_ref, gl_ref, w2_ref, gb_ref, o_ref, st_sc = refs
    L = GLA_CHUNK
    R = GLA_ROWS
    dk = GLA_KDIM
    dv = GLA_VDIM

    @pl.when(pl.program_id(1) == 0)
    def _():
        st_sc[...] = jnp.zeros(st_sc.shape, F32)

    pre = jnp.dot(gl_ref[0], w2_ref[...], precision=HI, preferred_element_type=F32) + gb_ref[...]
    lg = -_softplus(-pre) * (1.0 / GLA_TAU)
    row_i = lax.broadcasted_iota(jnp.int32, (R, R), 0)
    col_i = lax.broadcasted_iota(jnp.int32, (R, R), 1)
    same = (row_i // L) == (col_i // L)
    causal = (col_i >= row_i) if reverse else (col_i <= row_i)
    bc = jnp.dot((same & causal).astype(F32), lg, precision=HI, preferred_element_type=F32)
    mid = (L // 2 - 1) if reverse else (L // 2)
    end = 0 if reverse else (L - 1)
    width = bc.shape[1]
    ref_b = jnp.concatenate([jnp.broadcast_to(bc[ci * L + mid:ci * L + mid + 1], (L, width))
                             for ci in range(R // L)], axis=0)
    end_b = jnp.concatenate([jnp.broadcast_to(bc[ci * L + end:ci * L + end + 1], (L, width))
                             for ci in range(R // L)], axis=0)
    q = q_ref[0] * (dk ** -0.5)
    k = k_ref[0]
    qi = (q * jnp.exp(bc - ref_b)).astype(BF)
    ki = (k * jnp.exp(ref_b - bc)).astype(BF)
    qe = (q * jnp.exp(bc)).astype(BF)
    kd_t = (k * jnp.exp(end_b - bc)).T
    bc_t = bc.T
    lane = lax.broadcasted_iota(jnp.int32, (dk, R), 1)
    mask_c = causal[0:L, 0:L]
    order = range(R // L - 1, -1, -1) if reverse else range(R // L)
    for ci in order:
        rows = slice(ci * L, (ci + 1) * L)
        for h in range(GLA_HEADS):
            kl = slice(h * dk, (h + 1) * dk)
            vl = slice(h * dv, (h + 1) * dv)
            vb = v_ref[0, :, vl].astype(BF)
            att = jnp.where(mask_c, _dot_nt(qi[rows, kl], ki[rows, kl]), 0.0)
            prev = st_sc[h]
            piece = _dot(att.astype(BF), vb[rows]) + _dot(qe[rows, kl], prev.astype(BF))
            kdm = jnp.where(lane // L == ci, kd_t[kl, :], 0.0).astype(BF)
            e_idx = ci * L + end
            st_sc[h] = prev * jnp.exp(bc_t[kl, e_idx:e_idx + 1]) + _dot(kdm, vb)
            if reverse:
                piece = _rms(piece + of_ref[0, rows, vl]) * gn_ref[:, vl] * _silu(r_ref[0, rows, vl])
            o_ref[0, rows, vl] = piece.astype(o_ref.dtype)


def _gla(q, k, v, gl, r, w2f, w2b, gbf, gbb, gla_norm):
    b, s, _ = q.shape
    R = GLA_ROWS
    nb = s // R
    hk = GLA_HEADS * GLA_KDIM
    hv = GLA_HEADS * GLA_VDIM

    def call(reverse, w2, gb, extra_in, extra_shapes):
        cidx = (lambda c: nb - 1 - c) if reverse else (lambda c: c)
        row = lambda bi, c: (bi, cidx(c), 0)
        full = lambda bi, c: (0, 0)
        in_specs = [pl.BlockSpec((1, R, hk), row), pl.BlockSpec((1, R, hk), row), pl.BlockSpec((1, R, hv), row),
                    pl.BlockSpec((1, R, LANES), row), pl.BlockSpec((LANES, hk), full), pl.BlockSpec((1, hk), full)]
        in_specs += [pl.BlockSpec(sh, row if len(sh) == 3 else full) for sh in extra_shapes]
        return pl.pallas_call(
            functools.partial(_gla_kernel, reverse=reverse),
            grid=(b, nb),
            in_specs=in_specs,
            out_specs=pl.BlockSpec((1, R, hv), row),
            out_shape=jax.ShapeDtypeStruct((b, s, hv), F32),
            scratch_shapes=[pltpu.VMEM((GLA_HEADS, GLA_KDIM, GLA_VDIM), F32)],
            compiler_params=_cparams("parallel", "arbitrary"),
            name="gla_bwd" if reverse else "gla_fwd",
        )(q, k, v, gl, w2, gb, *extra_in)

    o_f = call(False, w2f, gbf, (), ())
    return call(True, w2b, gbb, (o_f, r, gla_norm), ((1, R, hv), (1, R, hv), (1, hv)))


def _sgu_kernel(sg_ref, g_ref, b_ref, ws_ref, bias_ref, o_ref):
    x = sg_ref[...]
    gel = x * (0.5 * (1.0 + jnp.tanh(math.sqrt(2.0 / math.pi) * (x + 0.044715 * (x * x * x)))))
    u = gel[:, :SGU_WIDTH]
    svn = _layernorm(gel[:, SGU_WIDTH:], g_ref[...], b_ref[...]).astype(BF)
    for gi in range(SGU_GROUPS):
        cols = slice(gi * SGU_GROUP_DIM, (gi + 1) * SGU_GROUP_DIM)
        sp = _dot(ws_ref[gi], svn[:, cols]) + bias_ref[:, cols]
        o_ref[:, cols] = u[:, cols] * sp


def _sgu(sg, ln_g, ln_b, ws_bf, bias_full):
    t = sg.shape[0]
    c = SGU_CHUNK
    row = lambda i: (i, 0)
    full = lambda i: (0, 0)
    return pl.pallas_call(
        _sgu_kernel,
        grid=(t // c,),
        in_specs=[pl.BlockSpec((c, 2 * SGU_WIDTH), row),
                  pl.BlockSpec((1, SGU_WIDTH), full), pl.BlockSpec((1, SGU_WIDTH), full),
                  pl.BlockSpec((SGU_GROUPS, c, c), lambda i: (0, 0, 0)),
                  pl.BlockSpec((c, SGU_WIDTH), full)],
        out_specs=pl.BlockSpec((c, SGU_WIDTH), row),
        out_shape=jax.ShapeDtypeStruct((t, SGU_WIDTH), F32),
        compiler_params=_cparams("parallel"),
        name="sgu",
    )(sg, ln_g, ln_b, ws_bf, bias_full)


def _router_kernel(x_ref, wr_ref, route_ref, cnt_ref, base_sc):
    @pl.when(pl.program_id(0) == 0)
    def _():
        base_sc[...] = jnp.zeros(base_sc.shape, F32)

    tr = x_ref.shape[0]
    logits = jnp.dot(x_ref[...], wr_ref[...], precision=HI, preferred_element_type=F32)
    lane = lax.broadcasted_iota(jnp.int32, (tr, LANES), 1).astype(F32)
    lg = jnp.where(lane < N_EXPERTS, logits, -jnp.inf)
    m1 = jnp.max(lg, axis=1, keepdims=True)
    i1 = jnp.min(jnp.where(lg == m1, lane, float(LANES)), axis=1, keepdims=True)
    lg2 = jnp.where(lane == i1, -jnp.inf, lg)
    m2 = jnp.max(lg2, axis=1, keepdims=True)
    i2 = jnp.min(jnp.where(lg2 == m2, lane, float(LANES)), axis=1, keepdims=True)
    e = jnp.exp(m2 - m1)
    g1 = 1.0 / (1.0 + e)
    g2 = e / (1.0 + e)
    oh1 = (lane == i1).astype(F32)
    oh2 = (lane == i2).astype(F32)
    oh = oh1 + oh2
    row_i = lax.broadcasted_iota(jnp.int32, (tr, tr), 0)
    col_i = lax.broadcasted_iota(jnp.int32, (tr, tr), 1)
    before = _dot((col_i < row_i).astype(BF), oh.astype(BF)) + base_sc[...]
    r1 = jnp.sum(oh1 * before, axis=1, keepdims=True)
    r2 = jnp.sum(oh2 * before, axis=1, keepdims=True)
    base_sc[...] += jnp.sum(oh, axis=0, keepdims=True)
    route = jnp.zeros((tr, LANES), F32)
    for idx, val in enumerate((i1, i2, r1, r2, g1, g2)):
        route = jnp.where(lane == float(idx), val, route)
    route_ref[...] = route
    cnt_ref[...] = base_sc[...]


def _router(x, wr_pad):
    t, d = x.shape
    tr = min(ROUTE_TILE, t)
    return pl.pallas_call(
        _router_kernel,
        grid=(t // tr,),
        in_specs=[pl.BlockSpec((tr, d), lambda i: (i, 0)), pl.BlockSpec((d, LANES), lambda i: (0, 0))],
        out_specs=[pl.BlockSpec((tr, LANES), lambda i: (i, 0)), pl.BlockSpec((1, LANES), lambda i: (0, 0))],
        out_shape=[jax.ShapeDtypeStruct((t, LANES), F32), jax.ShapeDtypeStruct((1, LANES), F32)],
        scratch_shapes=[pltpu.VMEM((1, LANES), F32)],
        compiler_params=_cparams("arbitrary"),
        name="router",
    )(x, wr_pad)


def _piece_indices(pos, n_rows):
    base = jnp.arange(SC_ROW_SPLIT, dtype=jnp.int32)[:, None, None] * n_rows
    return (base + pos.T[None]).reshape(1, -1)


def _sc_scatter_rows(xp, idx, n_out):
    ns, t, w = xp.shape
    nblk = t // SC_WINDOW
    per_piece = idx.shape[1] // ns // SC_WINDOW
    mesh = plsc.VectorSubcoreMesh(core_axis_name="c", subcore_axis_name="s")

    @pl.kernel(out_type=jax.ShapeDtypeStruct((ns * n_out, w), xp.dtype), mesh=mesh)
    def k(x_hbm, i_hbm, o_hbm):
        def body(x_vmem, i_vmem):
            pltpu.sync_copy(x_vmem, o_hbm.at[i_vmem.at[0]])

        pltpu.emit_pipeline(
            body,
            grid=(idx.shape[1] // SC_WINDOW,),
            in_specs=[pl.BlockSpec((SC_WINDOW, w), index_map=lambda i: ((i // per_piece) * nblk + i % nblk, 0)),
                      pl.BlockSpec((1, SC_WINDOW), index_map=lambda i: (0, i))],
            out_specs=[],
            core_axis_name=("c", "s"),
            dimension_semantics=(pltpu.PARALLEL,),
        )(x_hbm, i_hbm)

    return k(xp.reshape(ns * t, w), idx).reshape(ns, n_out, w)


def _sc_gather_rows(yp, idx):
    ns, n, w = yp.shape
    mesh = plsc.VectorSubcoreMesh(core_axis_name="c", subcore_axis_name="s")

    @pl.kernel(out_type=jax.ShapeDtypeStruct((idx.shape[1], w), yp.dtype), mesh=mesh)
    def k(x_hbm, i_hbm, o_hbm):
        def body(i_vmem, o_vmem):
            pltpu.sync_copy(x_hbm.at[i_vmem.at[0]], o_vmem)

        pltpu.emit_pipeline(
            body,
            grid=(idx.shape[1] // SC_WINDOW,),
            in_specs=[pl.BlockSpec((1, SC_WINDOW), index_map=lambda i: (0, i))],
            out_specs=[pl.BlockSpec((SC_WINDOW, w), index_map=lambda i: (i, 0))],
            core_axis_name=("c", "s"),
            dimension_semantics=(pltpu.PARALLEL,),
        )(i_hbm, o_hbm)

    return k(yp.reshape(ns * n, w), idx)


def _combine_ln_kernel(x_ref, y_ref, route_ref, g_ref, beta_ref, o_ref):
    g1 = route_ref[:, 4:5]
    g2 = route_ref[:, 5:6]
    y = (g1 * jnp.concatenate([y_ref[j, 0] for j in range(SC_ROW_SPLIT)], axis=1)
         + g2 * jnp.concatenate([y_ref[j, 1] for j in range(SC_ROW_SPLIT)], axis=1))
    o_ref[...] = _layernorm(DN_ALPHA * x_ref[...] + y, g_ref[...], beta_ref[...])


def _combine_ln(x, y2, route, g, beta):
    t, d = x.shape
    tm = min(ROW_TILE, t)
    row = lambda i: (i, 0)
    full = lambda i: (0, 0)
    return pl.pallas_call(
        _combine_ln_kernel,
        grid=(t // tm,),
        in_specs=[pl.BlockSpec((tm, d), row),
                  pl.BlockSpec((SC_ROW_SPLIT, 2, tm, d // SC_ROW_SPLIT), lambda i: (0, 0, i, 0)),
                  pl.BlockSpec((tm, LANES), row), pl.BlockSpec((1, d), full), pl.BlockSpec((1, d), full)],
        out_specs=pl.BlockSpec((tm, d), row),
        out_shape=jax.ShapeDtypeStruct((t, d), F32),
        compiler_params=_cparams("parallel"),
        name="moe_combine_ln",
    )(x, y2, route, g, beta)


def _pad_cols(w, n):
    return jnp.pad(w, ((0, 0), (0, n - w.shape[1])))


def _prep_even(p):
    (w_in, q_norm, w_uq, kv_norm, w_ukv, conv_w, conv_b, dt_bias, a_log, d_skip, ssm_norm, w_out,
     ln1_g, ln1_b, w_gate, w_up, w_down, ln2_g, ln2_b) = p
    o = 0
    cq = w_in[:, o:o + Q_RANK]; o += Q_RANK
    ckv = w_in[:, o:o + KV_RANK]; o += KV_RANK
    kr = w_in[:, o:o + QK_ROPE]; o += QK_ROPE
    z = w_in[:, o:o + SSD_INNER]; o += SSD_INNER
    xbc = w_in[:, o:o + CONV_DIM]; o += CONV_DIM
    dt = w_in[:, o:]
    half = QK_ROPE // 2
    zeros = lambda n: jnp.zeros((w_in.shape[0], n), F32)
    kra = jnp.concatenate([zeros(QK_NOPE), kr, zeros(LANES - QK_NOPE - QK_ROPE)], axis=1)
    krb = jnp.concatenate([zeros(QK_NOPE), -kr[:, half:], kr[:, :half], zeros(LANES - QK_NOPE - QK_ROPE)], axis=1)
    w_in_p = jnp.concatenate([cq, ckv, kra, krb, z, xbc, _pad_cols(dt, LANES)], axis=1).astype(BF)

    wq = w_uq.reshape(Q_RANK, MLA_HEADS, QK_NOPE + QK_ROPE)
    nope, rope = wq[..., :QK_NOPE], wq[..., QK_NOPE:]
    zq = lambda n: jnp.zeros((Q_RANK, MLA_HEADS, n), F32)
    wqa = jnp.concatenate([nope, rope, zq(LANES - QK_NOPE - QK_ROPE)], axis=-1)
    wqb = jnp.concatenate([zq(QK_NOPE), -rope[..., half:], rope[..., :half], zq(LANES - QK_NOPE - QK_ROPE)], axis=-1)
    wkv = w_ukv.reshape(KV_RANK, MLA_HEADS, QK_NOPE + V_DIM)
    zk = jnp.zeros((KV_RANK, MLA_HEADS, LANES - QK_NOPE), F32)
    wk = jnp.concatenate([wkv[..., :QK_NOPE], zk], axis=-1)
    vv_t = jnp.transpose(wkv[..., QK_NOPE:], (1, 2, 0))
    wvt = jnp.concatenate([vv_t, jnp.zeros((MLA_HEADS, V_ROWS - V_DIM, KV_RANK), F32)], axis=1)
    vadd = jnp.tile((jnp.arange(V_ROWS) == V_DIM).astype(F32), MLA_HEADS)[:, None]
    hw = MLA_HEADS * LANES
    return dict(
        w_in=w_in_p, q_norm=q_norm[None], kv_norm=kv_norm[None],
        wqa=wqa.reshape(Q_RANK, hw).astype(BF), wqb=wqb.reshape(Q_RANK, hw).astype(BF),
        wk=wk.reshape(KV_RANK, hw).astype(BF), wvt=wvt.reshape(MLA_HEADS * V_ROWS, KV_RANK).astype(BF), vadd=vadd,
        conv_w=conv_w, conv_b=conv_b[None],
        dt_bias=_pad_cols(dt_bias.reshape(1, -1), LANES), a_log=_pad_cols(a_log.reshape(1, -1), LANES),
        d_skip=jnp.repeat(d_skip, SSD_HEAD_DIM)[None], ssm_norm=ssm_norm[None],
        wo_a=w_out[:MLA_HEADS * V_DIM].astype(BF), wo_b=w_out[MLA_HEADS * V_DIM:].astype(BF),
        ln1_g=ln1_g[None], ln1_b=ln1_b[None],
        wg=w_gate.astype(BF), wu=w_up.astype(BF), wd=w_down.astype(BF),
        ln2_g=ln2_g[None], ln2_b=ln2_b[None])


def _prep_odd(p):
    (w_in, gate_w2, gate_b, gla_norm, sgu_ln_g, sgu_ln_b, w_s, b_s, w_out, ln1_g, ln1_b,
     w_router, we_gate, we_up, we_down, ln2_g, ln2_b) = p
    hk = GLA_HEADS * GLA_KDIM
    hv = GLA_HEADS * GLA_VDIM
    o = 2 * hk + 2 * hv
    gl = w_in[:, o:o + 2 * GLA_GATE_RANK]
    w_in_p = jnp.concatenate([w_in[:, :o], _pad_cols(gl, LANES), w_in[:, o + 2 * GLA_GATE_RANK:]], axis=1).astype(BF)
    zr = lambda n: jnp.zeros((n, hk), F32)
    w2f = jnp.concatenate([gate_w2[0], zr(LANES - GLA_GATE_RANK)], axis=0)
    w2b = jnp.concatenate([zr(GLA_GATE_RANK), gate_w2[1], zr(LANES - 2 * GLA_GATE_RANK)], axis=0)
    bias_full = jnp.repeat(b_s.T, SGU_GROUP_DIM, axis=1)
    return dict(
        w_in=w_in_p, w2f=w2f, w2b=w2b, gbf=gate_b[0][None], gbb=gate_b[1][None], gla_norm=gla_norm[None],
        sgu_g=sgu_ln_g[None], sgu_b=sgu_ln_b[None], ws=w_s.astype(BF), sgu_bias=bias_full,
        wo_a=w_out[:hv].astype(BF), wo_b=w_out[hv:].astype(BF), ln1_g=ln1_g[None], ln1_b=ln1_b[None],
        w_router=_pad_cols(w_router, LANES),
        wg=we_gate.astype(BF), wu=we_up.astype(BF), wd=we_down.astype(BF),
        ln2_g=ln2_g[None], ln2_b=ln2_b[None])


def _rope_tables(s):
    half = QK_ROPE // 2
    inv = jnp.exp(-math.log(ROPE_THETA) * jnp.arange(half, dtype=F32) / half)
    ang = jnp.arange(s, dtype=F32)[:, None] * inv[None, :]
    cos, sin = jnp.cos(ang), jnp.sin(ang)
    pad = LANES - QK_NOPE - QK_ROPE
    cos_t = jnp.concatenate([jnp.ones((s, QK_NOPE), F32), cos, cos, jnp.ones((s, pad), F32)], axis=1)
    sin_t = jnp.concatenate([jnp.zeros((s, QK_NOPE), F32), sin, sin, jnp.zeros((s, pad), F32)], axis=1)
    return cos_t, sin_t


def _even_layer(x, w, b, s):
    t = b * s
    mla_in, z, xbc, dt = _proj(x, w["w_in"], (Q_RANK + KV_RANK + 2 * LANES, SSD_INNER, CONV_DIM, LANES),
                               (F32, F32, F32, F32))
    cos_t, sin_t = _rope_tables(s)
    q, k, vt = _mla_prep(mla_in, cos_t, sin_t, w["q_norm"], w["kv_norm"], w["wqa"], w["wqb"], w["wk"], w["wvt"],
                         w["vadd"], b, s)
    hw = MLA_HEADS * LANES
    o_attn = _flash(q.reshape(b, s, hw), k.reshape(b, s, hw), vt)
    y = _ssd(xbc.reshape(b, s, CONV_DIM), dt.reshape(b, s, LANES), z.reshape(b, s, SSD_INNER),
             w["conv_w"], w["conv_b"], w["dt_bias"], w["a_log"], w["d_skip"], w["ssm_norm"])
    x1 = _outproj_ln(o_attn.reshape(t, -1), y.reshape(t, -1), x, w["wo_a"], w["wo_b"], w["ln1_g"], w["ln1_b"])
    return _ffn_ln(x1, w["wg"], w["wu"], w["wd"], w["ln2_g"], w["ln2_b"])


def _moe(x1, x1p, w):
    t, d = x1.shape
    route, cnt = _router(x1, w["w_router"])
    eid = route[:, 0:2].astype(jnp.int32)
    rank = route[:, 2:4].astype(jnp.int32)
    counts = cnt[0, :N_EXPERTS].astype(jnp.int32)
    tm = MOE_TILE
    padded = ((counts + tm - 1) // tm) * tm
    ends = jnp.cumsum(padded)
    offs = ends - padded
    pos = offs[eid] + rank
    p_rows = 2 * t + N_EXPERTS * tm
    tiles = jnp.arange(p_rows // tm, dtype=jnp.int32)
    tile_ends = ends // tm
    tile_expert = jnp.minimum(jnp.sum(tiles[:, None] >= tile_ends[None, :], axis=1), N_EXPERTS - 1).astype(jnp.int32)
    tile_valid = (tiles < tile_ends[-1]).astype(jnp.int32)
    idx = _piece_indices(pos, p_rows)
    xs = _sc_scatter_rows(x1p, idx, p_rows)
    ys = _moe_ffn(xs, tile_expert, tile_valid, w["wg"], w["wu"], w["wd"])
    y2 = _sc_gather_rows(ys, idx).reshape(SC_ROW_SPLIT, 2, t, d // SC_ROW_SPLIT)
    return _combine_ln(x1, y2, route, w["ln2_g"], w["ln2_b"])


def _odd_layer(x, w, b, s):
    t = b * s
    hk = GLA_HEADS * GLA_KDIM
    hv = GLA_HEADS * GLA_VDIM
    q, k, v, r, gl, sg = _proj(x, w["w_in"], (hk, hk, hv, hv, LANES, 2 * SGU_WIDTH), (F32,) * 6)
    o = _gla(q.reshape(b, s, hk), k.reshape(b, s, hk), v.reshape(b, s, hv), gl.reshape(b, s, LANES),
             r.reshape(b, s, hv), w["w2f"], w["w2b"], w["gbf"], w["gbb"], w["gla_norm"])
    o_sgu = _sgu(sg, w["sgu_g"], w["sgu_b"], w["ws"], w["sgu_bias"])
    x1, x1p = _outproj_ln(o.reshape(t, hv), o_sgu, x, w["wo_a"], w["wo_b"], w["ln1_g"], w["ln1_b"], pieces=True)
    return _moe(x1, x1p, w)


def _trunk(x, ev_w, od_w):
    b, s, d = x.shape
    x = x.reshape(b * s, d)
    for i in range(DEPTH):
        if i % 2 == 0:
            x = _even_layer(x, ev_w[i // 2], b, s)
        else:
            x = _odd_layer(x, od_w[i // 2], b, s)
    return x.reshape(b, s, d)


def kernel(x_prompt, x_sample, ev_w_in, ev_q_norm, ev_w_uq, ev_kv_norm, ev_w_ukv, ev_conv_w, ev_conv_b, ev_dt_bias, ev_a_log, ev_d_skip, ev_ssm_norm, ev_w_out, ev_ln1_g, ev_ln1_b, ev_w_gate, ev_w_up, ev_w_down, ev_ln2_g, ev_ln2_b, od_w_in, od_gate_w2, od_gate_b, od_gla_norm, od_sgu_ln_g, od_sgu_ln_b, od_w_s, od_b_s, od_w_out, od_ln1_g, od_ln1_b, od_w_router, od_we_gate, od_we_up, od_we_down, od_ln2_g, od_ln2_b):
    ev = (ev_w_in, ev_q_norm, ev_w_uq, ev_kv_norm, ev_w_ukv, ev_conv_w, ev_conv_b, ev_dt_bias,
          ev_a_log, ev_d_skip, ev_ssm_norm, ev_w_out, ev_ln1_g, ev_ln1_b, ev_w_gate, ev_w_up,
          ev_w_down, ev_ln2_g, ev_ln2_b)
    od = (od_w_in, od_gate_w2, od_gate_b, od_gla_norm, od_sgu_ln_g, od_sgu_ln_b, od_w_s, od_b_s,
          od_w_out, od_ln1_g, od_ln1_b, od_w_router, od_we_gate, od_we_up, od_we_down,
          od_ln2_g, od_ln2_b)
    ev_w = [_prep_even(tuple(t[i] for t in ev)) for i in range(ev_w_in.shape[0])]
    od_w = [_prep_odd(tuple(t[i] for t in od)) for i in range(od_w_in.shape[0])]
    return (_trunk(x_prompt, ev_w, od_w), _trunk(x_sample, ev_w, od_w))
```

```python
import functools
import math

import jax
import jax.numpy as jnp
from jax import lax
from jax.experimental import pallas as pl
from jax.experimental.pallas import tpu as pltpu
from jax.experimental.pallas import tpu_sc as plsc

BF = jnp.bfloat16
F32 = jnp.float32
HI = lax.Precision.HIGHEST

D_MODEL = 1024
DEPTH = 4
MLA_HEADS = 8
QK_NOPE = 64
QK_ROPE = 32
V_DIM = 64
Q_RANK = 256
KV_RANK = 128
ROPE_THETA = 10000.0
SSD_HEADS = 8
SSD_HEAD_DIM = 64
SSD_GROUPS = 2
SSD_STATE = 64
SSD_CHUNK = 128
CONV_K = 5
SSD_INNER = SSD_HEADS * SSD_HEAD_DIM
SSD_HPG = SSD_HEADS // SSD_GROUPS
CONV_DIM = SSD_INNER + 2 * SSD_GROUPS * SSD_STATE
GLA_HEADS = 4
GLA_KDIM = 64
GLA_VDIM = 128
GLA_GATE_RANK = 16
GLA_TAU = 16.0
GLA_CHUNK = 64
SGU_GROUPS = 4
SGU_CHUNK = 128
SGU_GROUP_DIM = 128
SGU_WIDTH = SGU_GROUPS * SGU_GROUP_DIM
D_FF = 2816
N_EXPERTS = 8
D_FF_EXPERT = 3584
DN_ALPHA = (2 * DEPTH) ** 0.25
EPS = 1e-5

LANES = 128
SUBLANES = 8
VMEM_LIMIT = 56 * 1024 * 1024
SC_WINDOW = 128
SC_ROW_SPLIT = 4

ROW_TILE = 512
FF_CHUNK = 256
ATT_TQ = 512
ATT_TK = 512
MOE_TILE = 512
ROUTE_TILE = 512
GLA_ROWS = 2 * GLA_CHUNK
V_ROWS = 80


def _cparams(*sem):
    return pltpu.CompilerParams(dimension_semantics=sem, vmem_limit_bytes=VMEM_LIMIT)


def _resident(shape, index_map):
    return pl.BlockSpec(shape, index_map, pipeline_mode=pl.Buffered(1))


def _rms(x):
    return x * lax.rsqrt(jnp.mean(x * x, axis=-1, keepdims=True) + EPS)


def _layernorm(x, g, b):
    mu = jnp.mean(x, axis=-1, keepdims=True)
    xc = x - mu
    var = jnp.mean(xc * xc, axis=-1, keepdims=True)
    return xc * lax.rsqrt(var + EPS) * g + b


def _silu(x):
    return x * jax.nn.sigmoid(x)


def _softplus(x):
    return jnp.maximum(x, 0.0) + jnp.log1p(jnp.exp(-jnp.abs(x)))


def _dot(a, b):
    return jnp.dot(a, b, preferred_element_type=F32)


def _dot_nt(a, b):
    return lax.dot_general(a, b, (((1,), (1,)), ((), ())), preferred_element_type=F32)


def _proj_kernel(x_ref, w_ref, *o_refs, splits):
    xb = x_ref[...].astype(BF)
    off = 0
    for o_ref, n in zip(o_refs, splits):
        o_ref[...] = _dot(xb, w_ref[:, off:off + n]).astype(o_ref.dtype)
        off += n


def _proj(x2d, w_bf, splits, dtypes):
    t, k = x2d.shape
    n = w_bf.shape[1]
    tm = min(ROW_TILE, t)
    return pl.pallas_call(
        functools.partial(_proj_kernel, splits=splits),
        grid=(t // tm,),
        in_specs=[pl.BlockSpec((tm, k), lambda i: (i, 0)), _resident((k, n), lambda i: (0, 0))],
        out_specs=[pl.BlockSpec((tm, s), lambda i: (i, 0)) for s in splits],
        out_shape=[jax.ShapeDtypeStruct((t, s), d) for s, d in zip(splits, dtypes)],
        compiler_params=_cparams("parallel"),
        name="proj_in",
    )(x2d, w_bf)


def _mla_prep_kernel(m_ref, cos_ref, sin_ref, qn_ref, kvn_ref, wqa_ref, wqb_ref, wk_ref, wvt_ref, vadd_ref,
                     q_ref, k_ref, vt_ref):
    cq = m_ref[:, 0:Q_RANK]
    ckv = m_ref[:, Q_RANK:Q_RANK + KV_RANK]
    kra = m_ref[:, Q_RANK + KV_RANK:Q_RANK + KV_RANK + LANES]
    krb = m_ref[:, Q_RANK + KV_RANK + LANES:Q_RANK + KV_RANK + 2 * LANES]
    cos = cos_ref[...]
    sin = sin_ref[...]
    cos8 = jnp.concatenate([cos] * MLA_HEADS, axis=1)
    sin8 = jnp.concatenate([sin] * MLA_HEADS, axis=1)
    cqn = (_rms(cq) * qn_ref[...]).astype(BF)
    q = _dot(cqn, wqa_ref[...]) * cos8 + _dot(cqn, wqb_ref[...]) * sin8
    q_ref[...] = (q * ((QK_NOPE + QK_ROPE) ** -0.5 * math.log2(math.e))).astype(BF)
    ckvn = (_rms(ckv) * kvn_ref[...]).astype(BF)
    kr = kra * cos + krb * sin
    k = _dot(ckvn, wk_ref[...]) + jnp.concatenate([kr] * MLA_HEADS, axis=1)
    k_ref[...] = k.astype(BF)
    vt_ref[0] = (_dot_nt(wvt_ref[...], ckvn) + vadd_ref[...]).astype(BF)


def _mla_prep(mla_in, cos_t, sin_t, q_norm, kv_norm, wqa, wqb, wk, wvt, vadd, batch, seq):
    t = mla_in.shape[0]
    tm = min(ROW_TILE, seq)
    nseq = seq // tm
    hw = MLA_HEADS * LANES
    vr = MLA_HEADS * V_ROWS
    full = lambda i: (0, 0)
    return pl.pallas_call(
        _mla_prep_kernel,
        grid=(t // tm,),
        in_specs=[pl.BlockSpec((tm, mla_in.shape[1]), lambda i: (i, 0)),
                  pl.BlockSpec((tm, LANES), lambda i: (i % nseq, 0)),
                  pl.BlockSpec((tm, LANES), lambda i: (i % nseq, 0)),
                  pl.BlockSpec((1, Q_RANK), full), pl.BlockSpec((1, KV_RANK), full),
                  pl.BlockSpec((Q_RANK, hw), full), pl.BlockSpec((Q_RANK, hw), full),
                  pl.BlockSpec((KV_RANK, hw), full), pl.BlockSpec((vr, KV_RANK), full),
                  pl.BlockSpec((vr, 1), full)],
        out_specs=[pl.BlockSpec((tm, hw), lambda i: (i, 0)), pl.BlockSpec((tm, hw), lambda i: (i, 0)),
                   pl.BlockSpec((1, vr, tm), lambda i: (i // nseq, 0, i % nseq))],
        out_shape=[jax.ShapeDtypeStruct((t, hw), BF), jax.ShapeDtypeStruct((t, hw), BF),
                   jax.ShapeDtypeStruct((batch, vr, seq), BF)],
        compiler_params=_cparams("parallel"),
        name="mla_prep",
    )(mla_in, cos_t, sin_t, q_norm, kv_norm, wqa, wqb, wk, wvt, vadd)


def _flash_kernel(q_ref, k_ref, vt_ref, o_ref, m_sc, acc_sc, s_sc, *, tk, nk, unroll):
    for h in range(2):
        m_sc[h] = jnp.full(m_sc.shape[1:], -jnp.inf, F32)
        acc_sc[h] = jnp.zeros(acc_sc.shape[1:], F32)

    def scores(j, slot):
        off = pl.multiple_of(j * tk, tk)
        for h in range(2):
            lanes = slice(h * LANES, (h + 1) * LANES)
            s_sc[slot, h] = _dot_nt(k_ref[0, pl.ds(off, tk), lanes], q_ref[0, :, lanes])

    def consume(j, slot):
        off = pl.multiple_of(j * tk, tk)
        for h in range(2):
            st = s_sc[slot, h]
            m_prev = m_sc[h]
            m_new = jnp.maximum(m_prev, jnp.max(st, axis=0, keepdims=True))
            p = jnp.exp2(st - m_new[0:1, :]).astype(BF)
            alpha = jnp.exp2(m_prev - m_new)
            pv = _dot(vt_ref[0, h * V_ROWS:(h + 1) * V_ROWS, pl.ds(off, tk)], p)
            acc_sc[h] = alpha[0:1, :] * acc_sc[h] + pv
            m_sc[h] = m_new

    scores(0, 0)

    def body(jj, carry):
        j = unroll * jj
        for u in range(unroll):
            scores(jnp.minimum(j + u + 1, nk - 1), (u + 1) % 2)
            consume(j + u, u % 2)
        return carry

    lax.fori_loop(0, nk // unroll, body, 0)
    outs = []
    for h in range(2):
        acc = acc_sc[h]
        outs.append(acc[0:V_DIM, :] / acc[V_DIM:V_DIM + 1, :])
    o_ref[0] = jnp.concatenate(outs, axis=0).T


def _flash(q, k, vt):
    b, s, hw = q.shape
    tq = min(ATT_TQ, s)
    tk = min(ATT_TK, s)
    nk = s // tk
    unroll = 4 if nk % 4 == 0 else (2 if nk % 2 == 0 else 1)
    pairs = MLA_HEADS // 2
    return pl.pallas_call(
        functools.partial(_flash_kernel, tk=tk, nk=nk, unroll=unroll),
        grid=(b, pairs, s // tq),
        in_specs=[pl.BlockSpec((1, tq, 2 * LANES), lambda bi, hp, i: (bi, i, hp)),
                  pl.BlockSpec((1, s, 2 * LANES), lambda bi, hp, i: (bi, 0, hp)),
                  pl.BlockSpec((1, 2 * V_ROWS, s), lambda bi, hp, i: (bi, hp, 0))],
        out_specs=pl.BlockSpec((1, tq, LANES), lambda bi, hp, i: (bi, i, hp)),
        out_shape=jax.ShapeDtypeStruct((b, s, MLA_HEADS * V_DIM), F32),
        scratch_shapes=[pltpu.VMEM((2, SUBLANES, tq), F32), pltpu.VMEM((2, V_ROWS, tq), F32),
                        pltpu.VMEM((2, 2, tk, tq), F32)],
        compiler_params=_cparams("parallel", "parallel", "arbitrary"),
        name="mla_flash",
    )(q, k, vt)


def _split_bf16(x, pieces):
    out = []
    for _ in range(pieces):
        p = x.astype(BF)
        out.append(p)
        x = x - p.astype(F32)
    return jnp.concatenate(out, axis=1)


def _spread_matrix(first_lane, heads, width, pieces):
    src = jnp.arange(LANES)[:, None] - first_lane
    dst = jnp.arange(heads * width)[None, :] // width
    return jnp.tile((src == dst).astype(BF), (pieces, 1))


def _ssd_direction(xc_ref, xp_ref, xn_ref, dt_ref, ep_ref, el_ref, cw_ref, cb_ref, dtb_ref, alog_ref, dskip_ref,
                   o_ref, xe_sc, st_sc, *, reverse, cc, nc):
    L = SSD_CHUNK
    N = SSD_STATE
    P = SSD_HEAD_DIM
    xe_sc[0:SUBLANES, :] = jnp.where(cc > 0, xp_ref[0], 0.0)
    xe_sc[SUBLANES:SUBLANES + L, :] = xc_ref[0]
    xe_sc[SUBLANES + L:2 * SUBLANES + L, :] = jnp.where(cc < nc - 1, xn_ref[0], 0.0)
    conv = cb_ref[...] + cw_ref[0:1, :] * xe_sc[pl.ds(SUBLANES - CONV_K // 2, L), :]
    for j in range(1, CONV_K):
        conv = conv + cw_ref[j:j + 1, :] * xe_sc[pl.ds(SUBLANES - CONV_K // 2 + j, L), :]
    xbc = _silu(conv)
    xs = xbc[:, :SSD_INNER]
    bc = xbc[:, SSD_INNER:]
    bc_t = bc.T

    lane = lax.broadcasted_iota(jnp.int32, (L, LANES), 1)
    dtv = _softplus(dt_ref[0] + dtb_ref[...])
    a = jnp.where(lane[0:1] < 2 * SSD_HEADS, -jnp.exp(alog_ref[...]), 0.0)
    dta = dtv * a
    row_i = lax.broadcasted_iota(jnp.int32, (L, L), 0)
    col_i = lax.broadcasted_iota(jnp.int32, (L, L), 1)
    causal = (col_i >= row_i) if reverse else (col_i <= row_i)
    acs3 = _dot(causal.astype(BF), _split_bf16(dta, 3))
    acs = acs3[:, 0:LANES] + acs3[:, LANES:2 * LANES] + acs3[:, 2 * LANES:]
    acs_t = acs.T
    d0 = SSD_HEADS if reverse else 0
    end = 0 if reverse else L - 1
    tot = acs[end:end + 1, :]
    yield

    stacked = jnp.concatenate([dtv, jnp.exp(acs), jnp.exp(tot - acs),
                               jnp.broadcast_to(jnp.exp(tot), (2 * SUBLANES, LANES))], axis=0)
    spread = _dot(_split_bf16(stacked, 2), ep_ref[...])
    dt_x = spread[0:L]
    ea_x = spread[L:2 * L]
    eb_x = spread[2 * L:3 * L]
    et_x = spread[3 * L:3 * L + 1]
    col_x = _dot(_split_bf16(acs, 3), el_ref[...])
    row_x = jnp.concatenate([jnp.broadcast_to(acs_t[d0 + h:d0 + h + 1, :], (L, L)) for h in range(SSD_HEADS)], axis=1)
    causal_x = jnp.concatenate([causal] * SSD_HEADS, axis=1)
    decay_x = jnp.exp(jnp.where(causal_x, col_x - row_x, -jnp.inf))
    xdt = xs * dt_x
    xdt_b = xdt.astype(BF)
    xw_b = (xdt * eb_x).astype(BF)
    gw = SSD_HPG * P
    lane_g = lax.broadcasted_iota(jnp.int32, (L, gw), 1)
    yield

    ys = []
    for g in range(SSD_GROUPS):
        bm_g = bc[:, g * N:(g + 1) * N].astype(BF)
        cm_g = bc[:, (SSD_GROUPS + g) * N:(SSD_GROUPS + g + 1) * N].astype(BF)
        bm_t_g = bc_t[g * N:(g + 1) * N, :].astype(BF)
        cb = _dot_nt(cm_g, bm_g)
        m_g = (jnp.concatenate([cb] * SSD_HPG, axis=1) * decay_x[:, g * SSD_HPG * L:(g + 1) * SSD_HPG * L]).astype(BF)
        xg = xdt_b[:, g * gw:(g + 1) * gw]
        xbd = jnp.concatenate([jnp.where(lane_g // P == j, xg, jnp.zeros_like(xg)) for j in range(SSD_HPG)], axis=0)
        y_diag = _dot(m_g, xbd)
        states_t = _dot(bm_t_g, xw_b[:, g * gw:(g + 1) * gw])
        prev_t = st_sc[g]
        y_off = _dot(cm_g, prev_t.astype(BF)) * ea_x[:, g * gw:(g + 1) * gw]
        st_sc[g] = prev_t * et_x[:, g * gw:(g + 1) * gw] + states_t
        ys.append(y_diag + y_off)
        yield
    y = jnp.concatenate(ys, axis=1)
    if not reverse:
        y = y + dskip_ref[...] * xs
    o_ref[0] = y


def _interleave(*tracers):
    live = list(tracers)
    while live:
        for g in list(live):
            try:
                next(g)
            except StopIteration:
                live.remove(g)


def _ssd_kernel(fxc, fxp, fxn, fdt, bxc, bxp, bxn, bdt, epf_ref, elf_ref, epb_ref, elb_ref,
                cw_ref, cb_ref, dtb_ref, alog_ref, dskip_ref, yf_ref, yb_ref, xe_sc, st_sc, *, nc):
    c = pl.program_id(1)

    @pl.when(c == 0)
    def _():
        st_sc[...] = jnp.zeros(st_sc.shape, F32)

    shared = (cw_ref, cb_ref, dtb_ref, alog_ref, dskip_ref)
    _interleave(
        _ssd_direction(fxc, fxp, fxn, fdt, epf_ref, elf_ref, *shared, yf_ref, xe_sc.at[0], st_sc.at[0],
                       reverse=False, cc=c, nc=nc),
        _ssd_direction(bxc, bxp, bxn, bdt, epb_ref, elb_ref, *shared, yb_ref, xe_sc.at[1], st_sc.at[1],
                       reverse=True, cc=nc - 1 - c, nc=nc))


def _ssd(xbc, dt, conv_w, conv_b, dt_bias, a_log, d_skip):
    b, s, _ = xbc.shape
    L = SSD_CHUNK
    nc = s // L
    hb = L // SUBLANES
    nhb = s // SUBLANES
    full = lambda bi, c: (0, 0)

    def views(cidx):
        row = lambda bi, c: (bi, cidx(c), 0)
        return [pl.BlockSpec((1, L, CONV_DIM), row),
                pl.BlockSpec((1, SUBLANES, CONV_DIM), lambda bi, c: (bi, jnp.maximum(cidx(c) * hb - 1, 0), 0)),
                pl.BlockSpec((1, SUBLANES, CONV_DIM), lambda bi, c: (bi, jnp.minimum((cidx(c) + 1) * hb, nhb - 1), 0)),
                pl.BlockSpec((1, L, LANES), row)]

    spreads = [_spread_matrix(d0, SSD_HEADS, width, pieces)
               for d0 in (0, SSD_HEADS) for width, pieces in ((SSD_HEAD_DIM, 2), (L, 3))]
    return pl.pallas_call(
        functools.partial(_ssd_kernel, nc=nc),
        grid=(b, nc),
        in_specs=views(lambda c: c) + views(lambda c: nc - 1 - c) + [pl.BlockSpec(m.shape, full) for m in spreads] + [
            pl.BlockSpec((CONV_K, CONV_DIM), full), pl.BlockSpec((1, CONV_DIM), full),
            pl.BlockSpec((1, LANES), full), pl.BlockSpec((1, LANES), full), pl.BlockSpec((1, SSD_INNER), full)],
        out_specs=[pl.BlockSpec((1, L, SSD_INNER), lambda bi, c: (bi, c, 0)),
                   pl.BlockSpec((1, L, SSD_INNER), lambda bi, c: (bi, nc - 1 - c, 0))],
        out_shape=[jax.ShapeDtypeStruct((b, s, SSD_INNER), F32)] * 2,
        scratch_shapes=[pltpu.VMEM((2, L + 2 * SUBLANES, CONV_DIM), F32),
                        pltpu.VMEM((2, SSD_GROUPS, SSD_STATE, SSD_HPG * SSD_HEAD_DIM), F32)],
        compiler_params=_cparams("parallel", "arbitrary"),
        name="ssd",
    )(xbc, xbc, xbc, dt, xbc, xbc, xbc, dt, *spreads, conv_w, conv_b, dt_bias, a_log, d_skip)


def _to_pieces(ref, y):
    w = ref.shape[-1]
    for j in range(SC_ROW_SPLIT):
        ref[j] = y[:, j * w:(j + 1) * w].astype(ref.dtype)


def _from_pieces(ref):
    return jnp.concatenate([ref[j] for j in range(SC_ROW_SPLIT)], axis=1)


def _outproj_ln_kernel(*refs, even):
    if even:
        oa_ref, yf_ref, yb_ref, z_ref, nrm_ref = refs[:5]
        a = oa_ref[...]
        b = _rms((yf_ref[...] + yb_ref[...]) * _silu(z_ref[...])) * nrm_ref[...]
    else:
        of_ref, ob_ref, r_ref, gn_ref, sgu_ref = refs[:5]
        o = of_ref[...] + ob_ref[...]
        o = jnp.concatenate([_rms(o[:, h * GLA_VDIM:(h + 1) * GLA_VDIM]) for h in range(GLA_HEADS)], axis=1)
        a = o * gn_ref[...] * _silu(r_ref[...])
        b = sgu_ref[...]
    x_ref, wa_ref, wb_ref, g_ref, beta_ref, o_ref = refs[5:11]
    y = _dot(a.astype(BF), wa_ref[...]) + _dot(b.astype(BF), wb_ref[...])
    out = _layernorm(DN_ALPHA * x_ref[...] + y, g_ref[...], beta_ref[...])
    o_ref[...] = out
    for p_ref in refs[11:]:
        _to_pieces(p_ref, out)


def _outproj_ln(mix, x, wa, wb, g, beta, even):
    t, d = x.shape
    tm = min(ROW_TILE, t)
    row = lambda i: (i, 0)
    full = lambda i: (0, 0)
    out_specs = [pl.BlockSpec((tm, d), row)]
    out_shape = [jax.ShapeDtypeStruct((t, d), F32)]
    if not even:
        out_specs.append(pl.BlockSpec((SC_ROW_SPLIT, tm, d // SC_ROW_SPLIT), lambda i: (0, i, 0)))
        out_shape.append(jax.ShapeDtypeStruct((SC_ROW_SPLIT, t, d // SC_ROW_SPLIT), F32))
    mix_specs = [pl.BlockSpec((1, m.shape[1]), full) if m.shape[0] == 1 else pl.BlockSpec((tm, m.shape[1]), row)
                 for m in mix]
    res = pl.pallas_call(
        functools.partial(_outproj_ln_kernel, even=even),
        grid=(t // tm,),
        in_specs=mix_specs + [pl.BlockSpec((tm, d), row),
                              _resident(wa.shape, full), _resident(wb.shape, full),
                              pl.BlockSpec((1, d), full), pl.BlockSpec((1, d), full)],
        out_specs=out_specs,
        out_shape=out_shape,
        compiler_params=_cparams("parallel"),
        name="outproj_ln",
    )(*mix, x, wa, wb, g, beta)
    return res[0] if even else res


def _swiglu_acc(xb, wg_ref, wu_ref, wd_ref, acc_sc, nf):
    for f in range(nf):
        cols = slice(f * FF_CHUNK, (f + 1) * FF_CHUNK)
        h = _silu(_dot(xb, wg_ref[:, cols])) * _dot(xb, wu_ref[:, cols])
        part = _dot(h.astype(BF), wd_ref[cols, :])
        if f == 0:
            acc_sc[...] = part
        else:
            acc_sc[...] += part


def _ffn_ln_kernel(x_ref, wg_ref, wu_ref, wd_ref, g_ref, beta_ref, o_ref, acc_sc, *, nf):
    x = x_ref[...]
    _swiglu_acc(x.astype(BF), wg_ref, wu_ref, wd_ref, acc_sc, nf)
    o_ref[...] = _layernorm(DN_ALPHA * x + acc_sc[...], g_ref[...], beta_ref[...])


def _ffn_ln(x, wg, wu, wd, g, beta):
    t, d = x.shape
    f = wg.shape[1]
    tm = min(ROW_TILE, t)
    row = lambda i: (i, 0)
    full = lambda i: (0, 0)
    return pl.pallas_call(
        functools.partial(_ffn_ln_kernel, nf=f // FF_CHUNK),
        grid=(t // tm,),
        in_specs=[pl.BlockSpec((tm, d), row),
                  _resident((d, f), full), _resident((d, f), full), _resident((f, d), full),
                  pl.BlockSpec((1, d), full), pl.BlockSpec((1, d), full)],
        out_specs=pl.BlockSpec((tm, d), row),
        out_shape=jax.ShapeDtypeStruct((t, d), F32),
        scratch_shapes=[pltpu.VMEM((tm, d), F32)],
        compiler_params=_cparams("parallel"),
        name="ffn_ln",
    )(x, wg, wu, wd, g, beta)


def _moe_ffn_kernel(te_ref, tv_ref, xs_ref, wg_ref, wu_ref, wd_ref, o_ref, acc_sc, *, nf):
    @pl.when(tv_ref[pl.program_id(0)] > 0)
    def _():
        _swiglu_acc(_from_pieces(xs_ref).astype(BF), wg_ref, wu_ref, wd_ref, acc_sc, nf)
        _to_pieces(o_ref, acc_sc[...])


def _moe_ffn(xs, tile_expert, tile_valid, wg, wu, wd):
    ns, p, w = xs.shape
    d = ns * w
    f = wg.shape[2]
    tm = MOE_TILE
    row = lambda i, te, tv: (0, i, 0)
    grid_spec = pltpu.PrefetchScalarGridSpec(
        num_scalar_prefetch=2,
        grid=(p // tm,),
        in_specs=[pl.BlockSpec((ns, tm, w), row),
                  _resident((None, d, f), lambda i, te, tv: (te[i], 0, 0)),
                  _resident((None, d, f), lambda i, te, tv: (te[i], 0, 0)),
                  _resident((None, f, d), lambda i, te, tv: (te[i], 0, 0))],
        out_specs=pl.BlockSpec((ns, tm, w), row),
        scratch_shapes=[pltpu.VMEM((tm, d), F32)],
    )
    return pl.pallas_call(
        functools.partial(_moe_ffn_kernel, nf=f // FF_CHUNK),
        grid_spec=grid_spec,
        out_shape=jax.ShapeDtypeStruct((ns, p, w), F32),
        compiler_params=_cparams("arbitrary"),
        name="moe_ffn",
    )(tile_expert, tile_valid, xs, wg, wu, wd)


def _gla_direction(q_ref, k_ref, v_ref, gl_ref, w2_ref, gb_ref, o_ref, st_sc, *, reverse):
    L = GLA_CHUNK
    R = GLA_ROWS
    dk = GLA_KDIM
    dv = GLA_VDIM
    hk = GLA_HEADS * dk
    hv = GLA_HEADS * dv
    g2 = _split_bf16(gl_ref[0], 2)
    pre = _dot(jnp.concatenate([g2, g2[:, :LANES]], axis=1), w2_ref[...]) + gb_ref[...]
    lg = -_softplus(-pre) * (1.0 / GLA_TAU)
    row_i = lax.broadcasted_iota(jnp.int32, (R, R), 0)
    col_i = lax.broadcasted_iota(jnp.int32, (R, R), 1)
    intra = ((row_i // L) == (col_i // L)) & ((col_i >= row_i) if reverse else (col_i <= row_i))
    bc3 = _dot(intra.astype(BF), _split_bf16(lg, 3))
    bc = bc3[:, 0:hk] + bc3[:, hk:2 * hk] + bc3[:, 2 * hk:]
    mid = (L // 2 - 1) if reverse else (L // 2)
    end = 0 if reverse else (L - 1)
    ref_b = jnp.concatenate([jnp.broadcast_to(bc[ci * L + mid:ci * L + mid + 1], (L, hk))
                             for ci in range(R // L)], axis=0)
    end_b = jnp.concatenate([jnp.broadcast_to(bc[ci * L + end:ci * L + end + 1], (L, hk))
                             for ci in range(R // L)], axis=0)
    q = q_ref[0] * (dk ** -0.5)
    k = k_ref[0]
    qi = q * jnp.exp(bc - ref_b)
    ki = k * jnp.exp(ref_b - bc)
    qe = q * jnp.exp(bc)
    kd = k * jnp.exp(end_b - bc)
    first = 1 if reverse else 0
    in_first = (lax.broadcasted_iota(jnp.int32, (R, hk), 0) // L) == first
    d_first = jnp.exp(bc[first * L + end:first * L + end + 1])
    d_second = jnp.exp(bc[(1 - first) * L + end:(1 - first) * L + end + 1])
    qx = jnp.where(in_first, qe, qe * d_first).astype(BF)
    kx = jnp.where(in_first, kd * d_second, kd).astype(BF)
    qe_m = jnp.where(in_first, 0.0, qe)
    kd_m = jnp.where(in_first, kd, 0.0)
    yield

    a_heads = []
    for h in range(GLA_HEADS):
        kl = slice(h * dk, (h + 1) * dk)
        lhs = jnp.concatenate([qi[:, kl], qe_m[:, kl]], axis=0).astype(BF)
        rhs = jnp.concatenate([ki[:, kl], kd_m[:, kl]], axis=0).astype(BF)
        full = _dot_nt(lhs, rhs)
        a_heads.append((jnp.where(intra, full[0:R, 0:R], 0.0) + full[R:, R:]).astype(BF))
        yield
    vb = v_ref[0].astype(BF)
    lane_v = lax.broadcasted_iota(jnp.int32, (R, hv), 1)
    vbd = jnp.concatenate([jnp.where(lane_v // dv == h, vb, jnp.zeros_like(vb)) for h in range(GLA_HEADS)], axis=0)
    o = _dot(jnp.concatenate(a_heads, axis=1), vbd)
    yield
    st = st_sc[...]
    o_ref[0] = o + _dot_nt(qx, st.astype(BF))
    upd = _dot(v_ref[0].T.astype(BF), kx)
    on_diag = (lax.broadcasted_iota(jnp.int32, (hv, hk), 0) // dv) == (lax.broadcasted_iota(jnp.int32, (hv, hk), 1) // dk)
    st_sc[...] = st * (d_first * d_second) + jnp.where(on_diag, upd, 0.0)
    yield


def _gla_kernel(fq, fk, fv, fgl, bq, bk, bv, bgl, w2f_ref, gbf_ref, w2b_ref, gbb_ref, of_ref, ob_ref, st_sc):
    @pl.when(pl.program_id(1) == 0)
    def _():
        st_sc[...] = jnp.zeros(st_sc.shape, F32)

    _interleave(_gla_direction(fq, fk, fv, fgl, w2f_ref, gbf_ref, of_ref, st_sc.at[0], reverse=False),
                _gla_direction(bq, bk, bv, bgl, w2b_ref, gbb_ref, ob_ref, st_sc.at[1], reverse=True))


def _gla(q, k, v, gl, w2f, w2b, gbf, gbb):
    b, s, _ = q.shape
    R = GLA_ROWS
    nb = s // R
    hk = GLA_HEADS * GLA_KDIM
    hv = GLA_HEADS * GLA_VDIM
    full = lambda bi, c: (0, 0)

    def views(cidx):
        row = lambda bi, c: (bi, cidx(c), 0)
        return [pl.BlockSpec((1, R, hk), row), pl.BlockSpec((1, R, hk), row), pl.BlockSpec((1, R, hv), row),
                pl.BlockSpec((1, R, LANES), row)]

    return pl.pallas_call(
        _gla_kernel,
        grid=(b, nb),
        in_specs=views(lambda c: c) + views(lambda c: nb - 1 - c) + [
            pl.BlockSpec((3 * LANES, hk), full), pl.BlockSpec((1, hk), full),
            pl.BlockSpec((3 * LANES, hk), full), pl.BlockSpec((1, hk), full)],
        out_specs=[pl.BlockSpec((1, R, hv), lambda bi, c: (bi, c, 0)),
                   pl.BlockSpec((1, R, hv), lambda bi, c: (bi, nb - 1 - c, 0))],
        out_shape=[jax.ShapeDtypeStruct((b, s, hv), F32)] * 2,
        scratch_shapes=[pltpu.VMEM((2, hv, hk), F32)],
        compiler_params=_cparams("parallel", "arbitrary"),
        name="gla",
    )(q, k, v, gl, q, k, v, gl, w2f, gbf, w2b, gbb)


def _sgu_kernel(sg_ref, g_ref, b_ref, ws_ref, bias_ref, o_ref):
    x = sg_ref[...]
    gel = x * (0.5 * (1.0 + jnp.tanh(math.sqrt(2.0 / math.pi) * (x + 0.044715 * (x * x * x)))))
    u = gel[:, :SGU_WIDTH]
    svn = _layernorm(gel[:, SGU_WIDTH:], g_ref[...], b_ref[...]).astype(BF)
    c = SGU_CHUNK
    for ci in range(x.shape[0] // c):
        rows = slice(ci * c, (ci + 1) * c)
        for gi in range(SGU_GROUPS):
            cols = slice(gi * SGU_GROUP_DIM, (gi + 1) * SGU_GROUP_DIM)
            sp = _dot(ws_ref[gi], svn[rows, cols]) + bias_ref[:, cols]
            o_ref[rows, cols] = u[rows, cols] * sp


def _sgu(sg, ln_g, ln_b, ws_bf, bias_full):
    t = sg.shape[0]
    c = SGU_CHUNK
    tm = min(ROW_TILE, t)
    row = lambda i: (i, 0)
    full = lambda i: (0, 0)
    return pl.pallas_call(
        _sgu_kernel,
        grid=(t // tm,),
        in_specs=[pl.BlockSpec((tm, 2 * SGU_WIDTH), row),
                  pl.BlockSpec((1, SGU_WIDTH), full), pl.BlockSpec((1, SGU_WIDTH), full),
                  pl.BlockSpec((SGU_GROUPS, c, c), lambda i: (0, 0, 0)),
                  pl.BlockSpec((c, SGU_WIDTH), full)],
        out_specs=pl.BlockSpec((tm, SGU_WIDTH), row),
        out_shape=jax.ShapeDtypeStruct((t, SGU_WIDTH), F32),
        compiler_params=_cparams("parallel"),
        name="sgu",
    )(sg, ln_g, ln_b, ws_bf, bias_full)


def _router_kernel(x_ref, wr_ref, route_ref, cnt_ref, base_sc):
    @pl.when(pl.program_id(0) == 0)
    def _():
        base_sc[...] = jnp.zeros(base_sc.shape, F32)

    tr = x_ref.shape[0]
    logits = jnp.dot(x_ref[...], wr_ref[...], precision=HI, preferred_element_type=F32)
    lane = lax.broadcasted_iota(jnp.int32, (tr, LANES), 1).astype(F32)
    lg = jnp.where(lane < N_EXPERTS, logits, -jnp.inf)
    m1 = jnp.max(lg, axis=1, keepdims=True)
    i1 = jnp.min(jnp.where(lg == m1, lane, float(LANES)), axis=1, keepdims=True)
    lg2 = jnp.where(lane == i1, -jnp.inf, lg)
    m2 = jnp.max(lg2, axis=1, keepdims=True)
    i2 = jnp.min(jnp.where(lg2 == m2, lane, float(LANES)), axis=1, keepdims=True)
    e = jnp.exp(m2 - m1)
    g1 = 1.0 / (1.0 + e)
    g2 = e / (1.0 + e)
    oh1 = (lane == i1).astype(F32)
    oh2 = (lane == i2).astype(F32)
    oh = oh1 + oh2
    row_i = lax.broadcasted_iota(jnp.int32, (tr, tr), 0)
    col_i = lax.broadcasted_iota(jnp.int32, (tr, tr), 1)
    before = _dot((col_i < row_i).astype(BF), oh.astype(BF)) + base_sc[...]
    r1 = jnp.sum(oh1 * before, axis=1, keepdims=True)
    r2 = jnp.sum(oh2 * before, axis=1, keepdims=True)
    base_sc[...] += jnp.sum(oh, axis=0, keepdims=True)
    route = jnp.zeros((tr, LANES), F32)
    for idx, val in enumerate((i1, i2, r1, r2, g1, g2)):
        route = jnp.where(lane == float(idx), val, route)
    route_ref[...] = route
    cnt_ref[...] = base_sc[...]


def _router(x, wr_pad):
    t, d = x.shape
    tr = min(ROUTE_TILE, t)
    return pl.pallas_call(
        _router_kernel,
        grid=(t // tr,),
        in_specs=[pl.BlockSpec((tr, d), lambda i: (i, 0)), pl.BlockSpec((d, LANES), lambda i: (0, 0))],
        out_specs=[pl.BlockSpec((tr, LANES), lambda i: (i, 0)), pl.BlockSpec((1, LANES), lambda i: (0, 0))],
        out_shape=[jax.ShapeDtypeStruct((t, LANES), F32), jax.ShapeDtypeStruct((1, LANES), F32)],
        scratch_shapes=[pltpu.VMEM((1, LANES), F32)],
        compiler_params=_cparams("arbitrary"),
        name="router",
    )(x, wr_pad)


def _piece_indices(pos, n_rows):
    base = jnp.arange(SC_ROW_SPLIT, dtype=jnp.int32)[:, None, None] * n_rows
    return (base + pos.T[None]).reshape(1, -1)


def _sc_scatter_rows(xp, idx, n_out):
    ns, t, w = xp.shape
    nblk = t // SC_WINDOW
    per_piece = idx.shape[1] // ns // SC_WINDOW
    mesh = plsc.VectorSubcoreMesh(core_axis_name="c", subcore_axis_name="s")

    @pl.kernel(out_type=jax.ShapeDtypeStruct((ns * n_out, w), xp.dtype), mesh=mesh)
    def k(x_hbm, i_hbm, o_hbm):
        def body(x_vmem, i_vmem):
            pltpu.sync_copy(x_vmem, o_hbm.at[i_vmem.at[0]])

        pltpu.emit_pipeline(
            body,
            grid=(idx.shape[1] // SC_WINDOW,),
            in_specs=[pl.BlockSpec((SC_WINDOW, w), index_map=lambda i: ((i // per_piece) * nblk + i % nblk, 0)),
                      pl.BlockSpec((1, SC_WINDOW), index_map=lambda i: (0, i))],
            out_specs=[],
            core_axis_name=("c", "s"),
            dimension_semantics=(pltpu.PARALLEL,),
        )(x_hbm, i_hbm)

    return k(xp.reshape(ns * t, w), idx).reshape(ns, n_out, w)


def _sc_gather_rows(yp, idx):
    ns, n, w = yp.shape
    mesh = plsc.VectorSubcoreMesh(core_axis_name="c", subcore_axis_name="s")

    @pl.kernel(out_type=jax.ShapeDtypeStruct((idx.shape[1], w), yp.dtype), mesh=mesh)
    def k(x_hbm, i_hbm, o_hbm):
        def body(i_vmem, o_vmem):
            pltpu.sync_copy(x_hbm.at[i_vmem.at[0]], o_vmem)

        pltpu.emit_pipeline(
            body,
            grid=(idx.shape[1] // SC_WINDOW,),
            in_specs=[pl.BlockSpec((1, SC_WINDOW), index_map=lambda i: (0, i))],
            out_specs=[pl.BlockSpec((SC_WINDOW, w), index_map=lambda i: (i, 0))],
            core_axis_name=("c", "s"),
            dimension_semantics=(pltpu.PARALLEL,),
        )(i_hbm, o_hbm)

    return k(yp.reshape(ns * n, w), idx)


def _combine_ln_kernel(x_ref, y_ref, route_ref, g_ref, beta_ref, o_ref):
    g1 = route_ref[:, 4:5]
    g2 = route_ref[:, 5:6]
    y = (g1 * jnp.concatenate([y_ref[j, 0] for j in range(SC_ROW_SPLIT)], axis=1)
         + g2 * jnp.concatenate([y_ref[j, 1] for j in range(SC_ROW_SPLIT)], axis=1))
    o_ref[...] = _layernorm(DN_ALPHA * x_ref[...] + y, g_ref[...], beta_ref[...])


def _combine_ln(x, y2, route, g, beta):
    t, d = x.shape
    tm = min(ROW_TILE, t)
    row = lambda i: (i, 0)
    full = lambda i: (0, 0)
    return pl.pallas_call(
        _combine_ln_kernel,
        grid=(t // tm,),
        in_specs=[pl.BlockSpec((tm, d), row),
                  pl.BlockSpec((SC_ROW_SPLIT, 2, tm, d // SC_ROW_SPLIT), lambda i: (0, 0, i, 0)),
                  pl.BlockSpec((tm, LANES), row), pl.BlockSpec((1, d), full), pl.BlockSpec((1, d), full)],
        out_specs=pl.BlockSpec((tm, d), row),
        out_shape=jax.ShapeDtypeStruct((t, d), F32),
        compiler_params=_cparams("parallel"),
        name="moe_combine_ln",
    )(x, y2, route, g, beta)


def _pad_cols(w, n):
    return jnp.pad(w, ((0, 0), (0, n - w.shape[1])))


def _prep_even(p):
    (w_in, q_norm, w_uq, kv_norm, w_ukv, conv_w, conv_b, dt_bias, a_log, d_skip, ssm_norm, w_out,
     ln1_g, ln1_b, w_gate, w_up, w_down, ln2_g, ln2_b) = p
    o = 0
    cq = w_in[:, o:o + Q_RANK]; o += Q_RANK
    ckv = w_in[:, o:o + KV_RANK]; o += KV_RANK
    kr = w_in[:, o:o + QK_ROPE]; o += QK_ROPE
    z = w_in[:, o:o + SSD_INNER]; o += SSD_INNER
    xbc = w_in[:, o:o + CONV_DIM]; o += CONV_DIM
    dt = w_in[:, o:]
    half = QK_ROPE // 2
    zeros = lambda n: jnp.zeros((w_in.shape[0], n), F32)
    kra = jnp.concatenate([zeros(QK_NOPE), kr, zeros(LANES - QK_NOPE - QK_ROPE)], axis=1)
    krb = jnp.concatenate([zeros(QK_NOPE), -kr[:, half:], kr[:, :half], zeros(LANES - QK_NOPE - QK_ROPE)], axis=1)
    w_in_p = jnp.concatenate([cq, ckv, kra, krb, z, xbc, _pad_cols(dt, LANES)], axis=1).astype(BF)

    wq = w_uq.reshape(Q_RANK, MLA_HEADS, QK_NOPE + QK_ROPE)
    nope, rope = wq[..., :QK_NOPE], wq[..., QK_NOPE:]
    zq = lambda n: jnp.zeros((Q_RANK, MLA_HEADS, n), F32)
    wqa = jnp.concatenate([nope, rope, zq(LANES - QK_NOPE - QK_ROPE)], axis=-1)
    wqb = jnp.concatenate([zq(QK_NOPE), -rope[..., half:], rope[..., :half], zq(LANES - QK_NOPE - QK_ROPE)], axis=-1)
    wkv = w_ukv.reshape(KV_RANK, MLA_HEADS, QK_NOPE + V_DIM)
    zk = jnp.zeros((KV_RANK, MLA_HEADS, LANES - QK_NOPE), F32)
    wk = jnp.concatenate([wkv[..., :QK_NOPE], zk], axis=-1)
    vv_t = jnp.transpose(wkv[..., QK_NOPE:], (1, 2, 0))
    wvt = jnp.concatenate([vv_t, jnp.zeros((MLA_HEADS, V_ROWS - V_DIM, KV_RANK), F32)], axis=1)
    vadd = jnp.tile((jnp.arange(V_ROWS) == V_DIM).astype(F32), MLA_HEADS)[:, None]
    hw = MLA_HEADS * LANES
    return dict(
        w_in=w_in_p, q_norm=q_norm[None], kv_norm=kv_norm[None],
        wqa=wqa.reshape(Q_RANK, hw).astype(BF), wqb=wqb.reshape(Q_RANK, hw).astype(BF),
        wk=wk.reshape(KV_RANK, hw).astype(BF), wvt=wvt.reshape(MLA_HEADS * V_ROWS, KV_RANK).astype(BF), vadd=vadd,
        conv_w=conv_w, conv_b=conv_b[None],
        dt_bias=_pad_cols(dt_bias.reshape(1, -1), LANES), a_log=_pad_cols(a_log.reshape(1, -1), LANES),
        d_skip=jnp.repeat(d_skip, SSD_HEAD_DIM)[None], ssm_norm=ssm_norm[None],
        wo_a=w_out[:MLA_HEADS * V_DIM].astype(BF), wo_b=w_out[MLA_HEADS * V_DIM:].astype(BF),
        ln1_g=ln1_g[None], ln1_b=ln1_b[None],
        wg=w_gate.astype(BF), wu=w_up.astype(BF), wd=w_down.astype(BF),
        ln2_g=ln2_g[None], ln2_b=ln2_b[None])


def _prep_odd(p):
    (w_in, gate_w2, gate_b, gla_norm, sgu_ln_g, sgu_ln_b, w_s, b_s, w_out, ln1_g, ln1_b,
     w_router, we_gate, we_up, we_down, ln2_g, ln2_b) = p
    hk = GLA_HEADS * GLA_KDIM
    hv = GLA_HEADS * GLA_VDIM
    o = 2 * hk + 2 * hv
    gl = w_in[:, o:o + 2 * GLA_GATE_RANK]
    w_in_p = jnp.concatenate([w_in[:, :o], _pad_cols(gl, LANES), w_in[:, o + 2 * GLA_GATE_RANK:]], axis=1).astype(BF)
    zr = lambda n: jnp.zeros((n, hk), F32)
    def pieces(w2):
        hi = w2.astype(BF)
        lo = (w2 - hi.astype(F32)).astype(BF)
        return jnp.concatenate([hi, hi, lo], axis=0)

    w2f = pieces(jnp.concatenate([gate_w2[0], zr(LANES - GLA_GATE_RANK)], axis=0))
    w2b = pieces(jnp.concatenate([zr(GLA_GATE_RANK), gate_w2[1], zr(LANES - 2 * GLA_GATE_RANK)], axis=0))
    bias_full = jnp.repeat(b_s.T, SGU_GROUP_DIM, axis=1)
    return dict(
        w_in=w_in_p, w2f=w2f, w2b=w2b, gbf=gate_b[0][None], gbb=gate_b[1][None], gla_norm=gla_norm[None],
        sgu_g=sgu_ln_g[None], sgu_b=sgu_ln_b[None], ws=w_s.astype(BF), sgu_bias=bias_full,
        wo_a=w_out[:hv].astype(BF), wo_b=w_out[hv:].astype(BF), ln1_g=ln1_g[None], ln1_b=ln1_b[None],
        w_router=_pad_cols(w_router, LANES),
        wg=we_gate.astype(BF), wu=we_up.astype(BF), wd=we_down.astype(BF),
        ln2_g=ln2_g[None], ln2_b=ln2_b[None])


def _rope_tables(s):
    half = QK_ROPE // 2
    inv = jnp.exp(-math.log(ROPE_THETA) * jnp.arange(half, dtype=F32) / half)
    ang = jnp.arange(s, dtype=F32)[:, None] * inv[None, :]
    cos, sin = jnp.cos(ang), jnp.sin(ang)
    pad = LANES - QK_NOPE - QK_ROPE
    cos_t = jnp.concatenate([jnp.ones((s, QK_NOPE), F32), cos, cos, jnp.ones((s, pad), F32)], axis=1)
    sin_t = jnp.concatenate([jnp.zeros((s, QK_NOPE), F32), sin, sin, jnp.zeros((s, pad), F32)], axis=1)
    return cos_t, sin_t


def _even_layer(x, w, b, s):
    t = b * s
    mla_in, z, xbc, dt = _proj(x, w["w_in"], (Q_RANK + KV_RANK + 2 * LANES, SSD_INNER, CONV_DIM, LANES),
                               (F32, F32, F32, F32))
    cos_t, sin_t = _rope_tables(s)
    q, k, vt = _mla_prep(mla_in, cos_t, sin_t, w["q_norm"], w["kv_norm"], w["wqa"], w["wqb"], w["wk"], w["wvt"],
                         w["vadd"], b, s)
    hw = MLA_HEADS * LANES
    o_attn = _flash(q.reshape(b, s, hw), k.reshape(b, s, hw), vt)
    y_f, y_b = _ssd(xbc.reshape(b, s, CONV_DIM), dt.reshape(b, s, LANES),
                    w["conv_w"], w["conv_b"], w["dt_bias"], w["a_log"], w["d_skip"])
    mix = (o_attn.reshape(t, -1), y_f.reshape(t, -1), y_b.reshape(t, -1), z, w["ssm_norm"])
    x1 = _outproj_ln(mix, x, w["wo_a"], w["wo_b"], w["ln1_g"], w["ln1_b"], even=True)
    return _ffn_ln(x1, w["wg"], w["wu"], w["wd"], w["ln2_g"], w["ln2_b"])


def _moe(x1, x1p, w):
    t, d = x1.shape
    route, cnt = _router(x1, w["w_router"])
    eid = route[:, 0:2].astype(jnp.int32)
    rank = route[:, 2:4].astype(jnp.int32)
    counts = cnt[0, :N_EXPERTS].astype(jnp.int32)
    tm = MOE_TILE
    padded = ((counts + tm - 1) // tm) * tm
    ends = jnp.cumsum(padded)
    offs = ends - padded
    pos = offs[eid] + rank
    p_rows = 2 * t + N_EXPERTS * tm
    tiles = jnp.arange(p_rows // tm, dtype=jnp.int32)
    tile_ends = ends // tm
    tile_expert = jnp.minimum(jnp.sum(tiles[:, None] >= tile_ends[None, :], axis=1), N_EXPERTS - 1).astype(jnp.int32)
    tile_valid = (tiles < tile_ends[-1]).astype(jnp.int32)
    idx = _piece_indices(pos, p_rows)
    xs = _sc_scatter_rows(x1p, idx, p_rows)
    ys = _moe_ffn(xs, tile_expert, tile_valid, w["wg"], w["wu"], w["wd"])
    y2 = _sc_gather_rows(ys, idx).reshape(SC_ROW_SPLIT, 2, t, d // SC_ROW_SPLIT)
    return _combine_ln(x1, y2, route, w["ln2_g"], w["ln2_b"])


def _odd_layer(x, w, b, s):
    t = b * s
    hk = GLA_HEADS * GLA_KDIM
    hv = GLA_HEADS * GLA_VDIM
    q, k, v, r, gl, sg = _proj(x, w["w_in"], (hk, hk, hv, hv, LANES, 2 * SGU_WIDTH), (F32,) * 6)
    o_f, o_b = _gla(q.reshape(b, s, hk), k.reshape(b, s, hk), v.reshape(b, s, hv), gl.reshape(b, s, LANES),
                    w["w2f"], w["w2b"], w["gbf"], w["gbb"])
    o_sgu = _sgu(sg, w["sgu_g"], w["sgu_b"], w["ws"], w["sgu_bias"])
    mix = (o_f.reshape(t, hv), o_b.reshape(t, hv), r, w["gla_norm"], o_sgu)
    x1, x1p = _outproj_ln(mix, x, w["wo_a"], w["wo_b"], w["ln1_g"], w["ln1_b"], even=False)
    return _moe(x1, x1p, w)


def _trunk(x, ev_w, od_w):
    b, s, d = x.shape
    x = x.reshape(b * s, d)
    for i in range(DEPTH):
        if i % 2 == 0:
            x = _even_layer(x, ev_w[i // 2], b, s)
        else:
            x = _odd_layer(x, od_w[i // 2], b, s)
    return x.reshape(b, s, d)


def kernel(x_prompt, x_sample, ev_w_in, ev_q_norm, ev_w_uq, ev_kv_norm, ev_w_ukv, ev_conv_w, ev_conv_b, ev_dt_bias, ev_a_log, ev_d_skip, ev_ssm_norm, ev_w_out, ev_ln1_g, ev_ln1_b, ev_w_gate, ev_w_up, ev_w_down, ev_ln2_g, ev_ln2_b, od_w_in, od_gate_w2, od_gate_b, od_gla_norm, od_sgu_ln_g, od_sgu_ln_b, od_w_s, od_b_s, od_w_out, od_ln1_g, od_ln1_b, od_w_router, od_we_gate, od_we_up, od_we_down, od_ln2_g, od_ln2_b):
    ev = (ev_w_in, ev_q_norm, ev_w_uq, ev_kv_norm, ev_w_ukv, ev_conv_w, ev_conv_b, ev_dt_bias,
          ev_a_log, ev_d_skip, ev_ssm_norm, ev_w_out, ev_ln1_g, ev_ln1_b, ev_w_gate, ev_w_up,
          ev_w_down, ev_ln2_g, ev_ln2_b)
    od = (od_w_in, od_gate_w2, od_gate_b, od_gla_norm, od_sgu_ln_g, od_sgu_ln_b, od_w_s, od_b_s,
          od_w_out, od_ln1_g, od_ln1_b, od_w_router, od_we_gate, od_we_up, od_we_down,
          od_ln2_g, od_ln2_b)
    ev_w = [_prep_even(tuple(t[i] for t in ev)) for i in range(ev_w_in.shape[0])]
    od_w = [_prep_odd(tuple(t[i] for t in od)) for i in range(od_w_in.shape[0])]
    return (_trunk(x_prompt, ev_w, od_w), _trunk(x_sample, ev_w, od_w))
```

```python
import functools
import math

import jax
import jax.numpy as jnp
from jax import lax
from jax.experimental import pallas as pl
from jax.experimental.pallas import tpu as pltpu
from jax.experimental.pallas import tpu_sc as plsc

BF = jnp.bfloat16
F32 = jnp.float32
HI = lax.Precision.HIGHEST

D_MODEL = 1024
DEPTH = 4
MLA_HEADS = 8
QK_NOPE = 64
QK_ROPE = 32
V_DIM = 64
Q_RANK = 256
KV_RANK = 128
ROPE_THETA = 10000.0
SSD_HEADS = 8
SSD_HEAD_DIM = 64
SSD_GROUPS = 2
SSD_STATE = 64
SSD_CHUNK = 128
CONV_K = 5
SSD_INNER = SSD_HEADS * SSD_HEAD_DIM
SSD_HPG = SSD_HEADS // SSD_GROUPS
CONV_DIM = SSD_INNER + 2 * SSD_GROUPS * SSD_STATE
GLA_HEADS = 4
GLA_KDIM = 64
GLA_VDIM = 128
GLA_GATE_RANK = 16
GLA_TAU = 16.0
GLA_CHUNK = 64
SGU_GROUPS = 4
SGU_CHUNK = 128
SGU_GROUP_DIM = 128
SGU_WIDTH = SGU_GROUPS * SGU_GROUP_DIM
D_FF = 2816
N_EXPERTS = 8
D_FF_EXPERT = 3584
DN_ALPHA = (2 * DEPTH) ** 0.25
EPS = 1e-5

LANES = 128
SUBLANES = 8
VMEM_LIMIT = 56 * 1024 * 1024
SC_WINDOW = 128
SC_ROW_SPLIT = 4

ROW_TILE = 512
FF_CHUNK = 256
ATT_TQ = 512
ATT_TK = 512
MOE_TILE = 512
ROUTE_TILE = 512
GLA_ROWS = 2 * GLA_CHUNK
V_ROWS = 80


def _cparams(*sem):
    return pltpu.CompilerParams(dimension_semantics=sem, vmem_limit_bytes=VMEM_LIMIT)


def _resident(shape, index_map):
    return pl.BlockSpec(shape, index_map, pipeline_mode=pl.Buffered(1))


def _rms(x):
    return x * lax.rsqrt(jnp.mean(x * x, axis=-1, keepdims=True) + EPS)


def _layernorm(x, g, b):
    mu = jnp.mean(x, axis=-1, keepdims=True)
    xc = x - mu
    var = jnp.mean(xc * xc, axis=-1, keepdims=True)
    return xc * lax.rsqrt(var + EPS) * g + b


def _silu(x):
    return x * jax.nn.sigmoid(x)


def _softplus(x):
    return jnp.maximum(x, 0.0) + jnp.log1p(jnp.exp(-jnp.abs(x)))


def _dot(a, b):
    return jnp.dot(a, b, preferred_element_type=F32)


def _dot_nt(a, b):
    return lax.dot_general(a, b, (((1,), (1,)), ((), ())), preferred_element_type=F32)


def _proj_kernel(x_ref, w_ref, *o_refs, splits):
    xb = x_ref[...].astype(BF)
    off = 0
    for o_ref, n in zip(o_refs, splits):
        o_ref[...] = _dot(xb, w_ref[:, off:off + n]).astype(o_ref.dtype)
        off += n


def _proj(x2d, w_bf, splits, dtypes):
    t, k = x2d.shape
    n = w_bf.shape[1]
    tm = min(ROW_TILE, t)
    return pl.pallas_call(
        functools.partial(_proj_kernel, splits=splits),
        grid=(t // tm,),
        in_specs=[pl.BlockSpec((tm, k), lambda i: (i, 0)), _resident((k, n), lambda i: (0, 0))],
        out_specs=[pl.BlockSpec((tm, s), lambda i: (i, 0)) for s in splits],
        out_shape=[jax.ShapeDtypeStruct((t, s), d) for s, d in zip(splits, dtypes)],
        compiler_params=_cparams("parallel"),
        name="proj_in",
    )(x2d, w_bf)


def _mla_prep_kernel(m_ref, cos_ref, sin_ref, qn_ref, kvn_ref, wqa_ref, wqb_ref, wk_ref, wvt_ref, vadd_ref,
                     q_ref, k_ref, vt_ref):
    cq = m_ref[:, 0:Q_RANK]
    ckv = m_ref[:, Q_RANK:Q_RANK + KV_RANK]
    kra = m_ref[:, Q_RANK + KV_RANK:Q_RANK + KV_RANK + LANES]
    krb = m_ref[:, Q_RANK + KV_RANK + LANES:Q_RANK + KV_RANK + 2 * LANES]
    cos = cos_ref[...]
    sin = sin_ref[...]
    cos8 = jnp.concatenate([cos] * MLA_HEADS, axis=1)
    sin8 = jnp.concatenate([sin] * MLA_HEADS, axis=1)
    cqn = (_rms(cq) * qn_ref[...]).astype(BF)
    q = _dot(cqn, wqa_ref[...]) * cos8 + _dot(cqn, wqb_ref[...]) * sin8
    q_ref[...] = (q * ((QK_NOPE + QK_ROPE) ** -0.5 * math.log2(math.e))).astype(BF)
    ckvn = (_rms(ckv) * kvn_ref[...]).astype(BF)
    kr = kra * cos + krb * sin
    k = _dot(ckvn, wk_ref[...]) + jnp.concatenate([kr] * MLA_HEADS, axis=1)
    k_ref[...] = k.astype(BF)
    vt_ref[0] = (_dot_nt(wvt_ref[...], ckvn) + vadd_ref[...]).astype(BF)


def _mla_prep(mla_in, cos_t, sin_t, q_norm, kv_norm, wqa, wqb, wk, wvt, vadd, batch, seq):
    t = mla_in.shape[0]
    tm = min(ROW_TILE, seq)
    nseq = seq // tm
    hw = MLA_HEADS * LANES
    vr = MLA_HEADS * V_ROWS
    full = lambda i: (0, 0)
    return pl.pallas_call(
        _mla_prep_kernel,
        grid=(t // tm,),
        in_specs=[pl.BlockSpec((tm, mla_in.shape[1]), lambda i: (i, 0)),
                  pl.BlockSpec((tm, LANES), lambda i: (i % nseq, 0)),
                  pl.BlockSpec((tm, LANES), lambda i: (i % nseq, 0)),
                  pl.BlockSpec((1, Q_RANK), full), pl.BlockSpec((1, KV_RANK), full),
                  pl.BlockSpec((Q_RANK, hw), full), pl.BlockSpec((Q_RANK, hw), full),
                  pl.BlockSpec((KV_RANK, hw), full), pl.BlockSpec((vr, KV_RANK), full),
                  pl.BlockSpec((vr, 1), full)],
        out_specs=[pl.BlockSpec((tm, hw), lambda i: (i, 0)), pl.BlockSpec((tm, hw), lambda i: (i, 0)),
                   pl.BlockSpec((1, vr, tm), lambda i: (i // nseq, 0, i % nseq))],
        out_shape=[jax.ShapeDtypeStruct((t, hw), BF), jax.ShapeDtypeStruct((t, hw), BF),
                   jax.ShapeDtypeStruct((batch, vr, seq), BF)],
        compiler_params=_cparams("parallel"),
        name="mla_prep",
    )(mla_in, cos_t, sin_t, q_norm, kv_norm, wqa, wqb, wk, wvt, vadd)


def _flash_kernel(q_ref, k_ref, vt_ref, o_ref, m_sc, acc_sc, s_sc, *, tk, nk, unroll):
    for h in range(2):
        m_sc[h] = jnp.full(m_sc.shape[1:], -jnp.inf, F32)
        acc_sc[h] = jnp.zeros(acc_sc.shape[1:], F32)

    def scores(j, slot):
        off = pl.multiple_of(j * tk, tk)
        for h in range(2):
            lanes = slice(h * LANES, (h + 1) * LANES)
            s_sc[slot, h] = _dot_nt(k_ref[0, pl.ds(off, tk), lanes], q_ref[0, :, lanes])

    def consume(j, slot):
        off = pl.multiple_of(j * tk, tk)
        for h in range(2):
            st = s_sc[slot, h]
            m_prev = m_sc[h]
            m_new = jnp.maximum(m_prev, jnp.max(st, axis=0, keepdims=True))
            p = jnp.exp2(st - m_new[0:1, :]).astype(BF)
            alpha = jnp.exp2(m_prev - m_new)
            pv = _dot(vt_ref[0, h * V_ROWS:(h + 1) * V_ROWS, pl.ds(off, tk)], p)
            acc_sc[h] = alpha[0:1, :] * acc_sc[h] + pv
            m_sc[h] = m_new

    scores(0, 0)

    def body(jj, carry):
        j = unroll * jj
        for u in range(unroll):
            scores(jnp.minimum(j + u + 1, nk - 1), (u + 1) % 2)
            consume(j + u, u % 2)
        return carry

    lax.fori_loop(0, nk // unroll, body, 0)
    outs = []
    for h in range(2):
        acc = acc_sc[h]
        outs.append(acc[0:V_DIM, :] / acc[V_DIM:V_DIM + 1, :])
    o_ref[0] = jnp.concatenate(outs, axis=0).T


def _flash(q, k, vt):
    b, s, hw = q.shape
    tq = min(ATT_TQ, s)
    tk = min(ATT_TK, s)
    nk = s // tk
    unroll = next(u for u in (8, 4, 2, 1) if nk % u == 0)
    pairs = MLA_HEADS // 2
    return pl.pallas_call(
        functools.partial(_flash_kernel, tk=tk, nk=nk, unroll=unroll),
        grid=(b, pairs, s // tq),
        in_specs=[pl.BlockSpec((1, tq, 2 * LANES), lambda bi, hp, i: (bi, i, hp)),
                  pl.BlockSpec((1, s, 2 * LANES), lambda bi, hp, i: (bi, 0, hp)),
                  pl.BlockSpec((1, 2 * V_ROWS, s), lambda bi, hp, i: (bi, hp, 0))],
        out_specs=pl.BlockSpec((1, tq, LANES), lambda bi, hp, i: (bi, i, hp)),
        out_shape=jax.ShapeDtypeStruct((b, s, MLA_HEADS * V_DIM), F32),
        scratch_shapes=[pltpu.VMEM((2, SUBLANES, tq), F32), pltpu.VMEM((2, V_ROWS, tq), F32),
                        pltpu.VMEM((2, 2, tk, tq), F32)],
        compiler_params=_cparams("parallel", "parallel", "arbitrary"),
        name="mla_flash",
    )(q, k, vt)


def _split_bf16(x, pieces):
    out = []
    for _ in range(pieces):
        p = x.astype(BF)
        out.append(p)
        x = x - p.astype(F32)
    return jnp.concatenate(out, axis=1)


def _spread_matrix(first_lane, heads, width, pieces):
    src = jnp.arange(LANES)[:, None] - first_lane
    dst = jnp.arange(heads * width)[None, :] // width
    return jnp.tile((src == dst).astype(BF), (pieces, 1))


def _ssd_direction(xc_ref, xp_ref, xn_ref, dt_ref, ep_ref, el_ref, cw_ref, cb_ref, dtb_ref, alog_ref, dskip_ref,
                   o_ref, xe_sc, st_sc, *, reverse, cc, nc):
    L = SSD_CHUNK
    N = SSD_STATE
    P = SSD_HEAD_DIM
    xe_sc[0:SUBLANES, :] = jnp.where(cc > 0, xp_ref[0], 0.0)
    xe_sc[SUBLANES:SUBLANES + L, :] = xc_ref[0]
    xe_sc[SUBLANES + L:2 * SUBLANES + L, :] = jnp.where(cc < nc - 1, xn_ref[0], 0.0)
    conv = cb_ref[...] + cw_ref[0:1, :] * xe_sc[pl.ds(SUBLANES - CONV_K // 2, L), :]
    for j in range(1, CONV_K):
        conv = conv + cw_ref[j:j + 1, :] * xe_sc[pl.ds(SUBLANES - CONV_K // 2 + j, L), :]
    yield
    xbc = _silu(conv)
    xs = xbc[:, :SSD_INNER]
    bc = xbc[:, SSD_INNER:]
    bc_t = bc.T

    lane = lax.broadcasted_iota(jnp.int32, (L, LANES), 1)
    dtv = _softplus(dt_ref[0] + dtb_ref[...])
    a = jnp.where(lane[0:1] < 2 * SSD_HEADS, -jnp.exp(alog_ref[...]), 0.0)
    dta = dtv * a
    row_i = lax.broadcasted_iota(jnp.int32, (L, L), 0)
    col_i = lax.broadcasted_iota(jnp.int32, (L, L), 1)
    causal = (col_i >= row_i) if reverse else (col_i <= row_i)
    acs3 = _dot(causal.astype(BF), _split_bf16(dta, 3))
    yield
    acs = acs3[:, 0:LANES] + acs3[:, LANES:2 * LANES] + acs3[:, 2 * LANES:]
    acs_t = acs.T
    d0 = SSD_HEADS if reverse else 0
    end = 0 if reverse else L - 1
    tot = acs[end:end + 1, :]
    yield

    stacked = jnp.concatenate([dtv, jnp.exp(acs), jnp.exp(tot - acs),
                               jnp.broadcast_to(jnp.exp(tot), (2 * SUBLANES, LANES))], axis=0)
    spread = _dot(_split_bf16(stacked, 2), ep_ref[...])
    dt_x = spread[0:L]
    ea_x = spread[L:2 * L]
    eb_x = spread[2 * L:3 * L]
    et_x = spread[3 * L:3 * L + 1]
    col_x = _dot(_split_bf16(acs, 3), el_ref[...])
    yield
    row_x = jnp.concatenate([jnp.broadcast_to(acs_t[d0 + h:d0 + h + 1, :], (L, L)) for h in range(SSD_HEADS)], axis=1)
    causal_x = jnp.concatenate([causal] * SSD_HEADS, axis=1)
    decay_x = jnp.exp(jnp.where(causal_x, col_x - row_x, -jnp.inf))
    xdt = xs * dt_x
    xdt_b = xdt.astype(BF)
    xw_b = (xdt * eb_x).astype(BF)
    gw = SSD_HPG * P
    lane_g = lax.broadcasted_iota(jnp.int32, (L, gw), 1)
    yield

    ys = []
    for g in range(SSD_GROUPS):
        bm_g = bc[:, g * N:(g + 1) * N].astype(BF)
        cm_g = bc[:, (SSD_GROUPS + g) * N:(SSD_GROUPS + g + 1) * N].astype(BF)
        bm_t_g = bc_t[g * N:(g + 1) * N, :].astype(BF)
        cb = _dot_nt(cm_g, bm_g)
        m_g = (jnp.concatenate([cb] * SSD_HPG, axis=1) * decay_x[:, g * SSD_HPG * L:(g + 1) * SSD_HPG * L]).astype(BF)
        xg = xdt_b[:, g * gw:(g + 1) * gw]
        xbd = jnp.concatenate([jnp.where(lane_g // P == j, xg, jnp.zeros_like(xg)) for j in range(SSD_HPG)], axis=0)
        y_diag = _dot(m_g, xbd)
        states_t = _dot(bm_t_g, xw_b[:, g * gw:(g + 1) * gw])
        prev_t = st_sc[g]
        y_off = _dot(cm_g, prev_t.astype(BF)) * ea_x[:, g * gw:(g + 1) * gw]
        st_sc[g] = prev_t * et_x[:, g * gw:(g + 1) * gw] + states_t
        ys.append(y_diag + y_off)
        yield
    y = jnp.concatenate(ys, axis=1)
    if not reverse:
        y = y + dskip_ref[...] * xs
    o_ref[0] = y


def _interleave(*tracers):
    live = list(tracers)
    while live:
        for g in list(live):
            try:
                next(g)
            except StopIteration:
                live.remove(g)


def _ssd_kernel(fxc, fxp, fxn, fdt, bxc, bxp, bxn, bdt, epf_ref, elf_ref, epb_ref, elb_ref,
                cw_ref, cb_ref, dtb_ref, alog_ref, dskip_ref, yf_ref, yb_ref, xe_sc, st_sc, *, nc):
    c = pl.program_id(1)

    @pl.when(c == 0)
    def _():
        st_sc[...] = jnp.zeros(st_sc.shape, F32)

    shared = (cw_ref, cb_ref, dtb_ref, alog_ref, dskip_ref)
    _interleave(
        _ssd_direction(fxc, fxp, fxn, fdt, epf_ref, elf_ref, *shared, yf_ref, xe_sc.at[0], st_sc.at[0],
                       reverse=False, cc=c, nc=nc),
        _ssd_direction(bxc, bxp, bxn, bdt, epb_ref, elb_ref, *shared, yb_ref, xe_sc.at[1], st_sc.at[1],
                       reverse=True, cc=nc - 1 - c, nc=nc))


def _ssd(xbc, dt, conv_w, conv_b, dt_bias, a_log, d_skip):
    b, s, _ = xbc.shape
    L = SSD_CHUNK
    nc = s // L
    hb = L // SUBLANES
    nhb = s // SUBLANES
    full = lambda bi, c: (0, 0)

    def views(cidx):
        row = lambda bi, c: (bi, cidx(c), 0)
        return [pl.BlockSpec((1, L, CONV_DIM), row),
                pl.BlockSpec((1, SUBLANES, CONV_DIM), lambda bi, c: (bi, jnp.maximum(cidx(c) * hb - 1, 0), 0)),
                pl.BlockSpec((1, SUBLANES, CONV_DIM), lambda bi, c: (bi, jnp.minimum((cidx(c) + 1) * hb, nhb - 1), 0)),
                pl.BlockSpec((1, L, LANES), row)]

    spreads = [_spread_matrix(d0, SSD_HEADS, width, pieces)
               for d0 in (0, SSD_HEADS) for width, pieces in ((SSD_HEAD_DIM, 2), (L, 3))]
    return pl.pallas_call(
        functools.partial(_ssd_kernel, nc=nc),
        grid=(b, nc),
        in_specs=views(lambda c: c) + views(lambda c: nc - 1 - c) + [pl.BlockSpec(m.shape, full) for m in spreads] + [
            pl.BlockSpec((CONV_K, CONV_DIM), full), pl.BlockSpec((1, CONV_DIM), full),
            pl.BlockSpec((1, LANES), full), pl.BlockSpec((1, LANES), full), pl.BlockSpec((1, SSD_INNER), full)],
        out_specs=[pl.BlockSpec((1, L, SSD_INNER), lambda bi, c: (bi, c, 0)),
                   pl.BlockSpec((1, L, SSD_INNER), lambda bi, c: (bi, nc - 1 - c, 0))],
        out_shape=[jax.ShapeDtypeStruct((b, s, SSD_INNER), F32)] * 2,
        scratch_shapes=[pltpu.VMEM((2, L + 2 * SUBLANES, CONV_DIM), F32),
                        pltpu.VMEM((2, SSD_GROUPS, SSD_STATE, SSD_HPG * SSD_HEAD_DIM), F32)],
        compiler_params=_cparams("parallel", "arbitrary"),
        name="ssd",
    )(xbc, xbc, xbc, dt, xbc, xbc, xbc, dt, *spreads, conv_w, conv_b, dt_bias, a_log, d_skip)


def _to_pieces(ref, y):
    w = ref.shape[-1]
    for j in range(SC_ROW_SPLIT):
        ref[j] = y[:, j * w:(j + 1) * w].astype(ref.dtype)


def _from_pieces(ref):
    return jnp.concatenate([ref[j] for j in range(SC_ROW_SPLIT)], axis=1)


def _outproj_ln_kernel(*refs, even):
    if even:
        oa_ref, yf_ref, yb_ref, z_ref, nrm_ref = refs[:5]
        a = oa_ref[...]
        b = _rms((yf_ref[...] + yb_ref[...]) * _silu(z_ref[...])) * nrm_ref[...]
    else:
        of_ref, ob_ref, r_ref, gn_ref, sgu_ref = refs[:5]
        o = of_ref[...] + ob_ref[...]
        o = jnp.concatenate([_rms(o[:, h * GLA_VDIM:(h + 1) * GLA_VDIM]) for h in range(GLA_HEADS)], axis=1)
        a = o * gn_ref[...] * _silu(r_ref[...])
        b = sgu_ref[...]
    x_ref, wa_ref, wb_ref, g_ref, beta_ref, o_ref = refs[5:11]
    y = _dot(a.astype(BF), wa_ref[...]) + _dot(b.astype(BF), wb_ref[...])
    out = _layernorm(DN_ALPHA * x_ref[...] + y, g_ref[...], beta_ref[...])
    o_ref[...] = out
    for p_ref in refs[11:]:
        _to_pieces(p_ref, out)


def _outproj_ln(mix, x, wa, wb, g, beta, even):
    t, d = x.shape
    tm = min(ROW_TILE, t)
    row = lambda i: (i, 0)
    full = lambda i: (0, 0)
    out_specs = [pl.BlockSpec((tm, d), row)]
    out_shape = [jax.ShapeDtypeStruct((t, d), F32)]
    if not even:
        out_specs.append(pl.BlockSpec((SC_ROW_SPLIT, tm, d // SC_ROW_SPLIT), lambda i: (0, i, 0)))
        out_shape.append(jax.ShapeDtypeStruct((SC_ROW_SPLIT, t, d // SC_ROW_SPLIT), F32))
    mix_specs = [pl.BlockSpec((1, m.shape[1]), full) if m.shape[0] == 1 else pl.BlockSpec((tm, m.shape[1]), row)
                 for m in mix]
    res = pl.pallas_call(
        functools.partial(_outproj_ln_kernel, even=even),
        grid=(t // tm,),
        in_specs=mix_specs + [pl.BlockSpec((tm, d), row),
                              _resident(wa.shape, full), _resident(wb.shape, full),
                              pl.BlockSpec((1, d), full), pl.BlockSpec((1, d), full)],
        out_specs=out_specs,
        out_shape=out_shape,
        compiler_params=_cparams("parallel"),
        name="outproj_ln",
    )(*mix, x, wa, wb, g, beta)
    return res[0] if even else res


def _swiglu_acc(xb, wg_ref, wu_ref, wd_ref, acc_sc, nf):
    for f in range(nf):
        cols = slice(f * FF_CHUNK, (f + 1) * FF_CHUNK)
        h = _silu(_dot(xb, wg_ref[:, cols])) * _dot(xb, wu_ref[:, cols])
        part = _dot(h.astype(BF), wd_ref[cols, :])
        if f == 0:
            acc_sc[...] = part
        else:
            acc_sc[...] += part


def _ffn_ln_kernel(x_ref, wg_ref, wu_ref, wd_ref, g_ref, beta_ref, o_ref, acc_sc, *, nf):
    x = x_ref[...]
    _swiglu_acc(x.astype(BF), wg_ref, wu_ref, wd_ref, acc_sc, nf)
    o_ref[...] = _layernorm(DN_ALPHA * x + acc_sc[...], g_ref[...], beta_ref[...])


def _ffn_ln(x, wg, wu, wd, g, beta):
    t, d = x.shape
    f = wg.shape[1]
    tm = min(ROW_TILE, t)
    row = lambda i: (i, 0)
    full = lambda i: (0, 0)
    return pl.pallas_call(
        functools.partial(_ffn_ln_kernel, nf=f // FF_CHUNK),
        grid=(t // tm,),
        in_specs=[pl.BlockSpec((tm, d), row),
                  _resident((d, f), full), _resident((d, f), full), _resident((f, d), full),
                  pl.BlockSpec((1, d), full), pl.BlockSpec((1, d), full)],
        out_specs=pl.BlockSpec((tm, d), row),
        out_shape=jax.ShapeDtypeStruct((t, d), F32),
        scratch_shapes=[pltpu.VMEM((tm, d), F32)],
        compiler_params=_cparams("parallel"),
        name="ffn_ln",
    )(x, wg, wu, wd, g, beta)


def _moe_ffn_kernel(te_ref, tv_ref, xs_ref, wg_ref, wu_ref, wd_ref, o_ref, acc_sc, *, nf):
    @pl.when(tv_ref[pl.program_id(0)] > 0)
    def _():
        _swiglu_acc(_from_pieces(xs_ref).astype(BF), wg_ref, wu_ref, wd_ref, acc_sc, nf)
        _to_pieces(o_ref, acc_sc[...])


def _moe_ffn(xs, tile_expert, tile_valid, wg, wu, wd):
    ns, p, w = xs.shape
    d = ns * w
    f = wg.shape[2]
    tm = MOE_TILE
    row = lambda i, te, tv: (0, i, 0)
    grid_spec = pltpu.PrefetchScalarGridSpec(
        num_scalar_prefetch=2,
        grid=(p // tm,),
        in_specs=[pl.BlockSpec((ns, tm, w), row),
                  _resident((None, d, f), lambda i, te, tv: (te[i], 0, 0)),
                  _resident((None, d, f), lambda i, te, tv: (te[i], 0, 0)),
                  _resident((None, f, d), lambda i, te, tv: (te[i], 0, 0))],
        out_specs=pl.BlockSpec((ns, tm, w), row),
        scratch_shapes=[pltpu.VMEM((tm, d), F32)],
    )
    return pl.pallas_call(
        functools.partial(_moe_ffn_kernel, nf=f // FF_CHUNK),
        grid_spec=grid_spec,
        out_shape=jax.ShapeDtypeStruct((ns, p, w), F32),
        compiler_params=_cparams("arbitrary"),
        name="moe_ffn",
    )(tile_expert, tile_valid, xs, wg, wu, wd)


def _gla_direction(q_ref, k_ref, v_ref, gl_ref, w2_ref, gb_ref, o_ref, st_sc, *, reverse):
    L = GLA_CHUNK
    R = GLA_ROWS
    dk = GLA_KDIM
    dv = GLA_VDIM
    hk = GLA_HEADS * dk
    hv = GLA_HEADS * dv
    g2 = _split_bf16(gl_ref[0], 2)
    pre = _dot(jnp.concatenate([g2, g2[:, :LANES]], axis=1), w2_ref[...]) + gb_ref[...]
    yield
    lg = -_softplus(-pre) * (1.0 / GLA_TAU)
    row_i = lax.broadcasted_iota(jnp.int32, (R, R), 0)
    col_i = lax.broadcasted_iota(jnp.int32, (R, R), 1)
    intra = ((row_i // L) == (col_i // L)) & ((col_i >= row_i) if reverse else (col_i <= row_i))
    bc3 = _dot(intra.astype(BF), _split_bf16(lg, 3))
    yield
    bc = bc3[:, 0:hk] + bc3[:, hk:2 * hk] + bc3[:, 2 * hk:]
    mid = (L // 2 - 1) if reverse else (L // 2)
    end = 0 if reverse else (L - 1)
    ref_b = jnp.concatenate([jnp.broadcast_to(bc[ci * L + mid:ci * L + mid + 1], (L, hk))
                             for ci in range(R // L)], axis=0)
    end_b = jnp.concatenate([jnp.broadcast_to(bc[ci * L + end:ci * L + end + 1], (L, hk))
                             for ci in range(R // L)], axis=0)
    q = q_ref[0] * (dk ** -0.5)
    k = k_ref[0]
    qi = q * jnp.exp(bc - ref_b)
    ki = k * jnp.exp(ref_b - bc)
    qe = q * jnp.exp(bc)
    kd = k * jnp.exp(end_b - bc)
    first = 1 if reverse else 0
    in_first = (lax.broadcasted_iota(jnp.int32, (R, hk), 0) // L) == first
    d_first = jnp.exp(bc[first * L + end:first * L + end + 1])
    d_second = jnp.exp(bc[(1 - first) * L + end:(1 - first) * L + end + 1])
    qx = jnp.where(in_first, qe, qe * d_first).astype(BF)
    kx = jnp.where(in_first, kd * d_second, kd).astype(BF)
    qe_m = jnp.where(in_first, 0.0, qe)
    kd_m = jnp.where(in_first, kd, 0.0)
    yield

    a_heads = []
    for h in range(GLA_HEADS):
        kl = slice(h * dk, (h + 1) * dk)
        lhs = jnp.concatenate([qi[:, kl], qe_m[:, kl]], axis=0).astype(BF)
        rhs = jnp.concatenate([ki[:, kl], kd_m[:, kl]], axis=0).astype(BF)
        full = _dot_nt(lhs, rhs)
        a_heads.append((jnp.where(intra, full[0:R, 0:R], 0.0) + full[R:, R:]).astype(BF))
        yield
    vb = v_ref[0].astype(BF)
    lane_v = lax.broadcasted_iota(jnp.int32, (R, hv), 1)
    vbd = jnp.concatenate([jnp.where(lane_v // dv == h, vb, jnp.zeros_like(vb)) for h in range(GLA_HEADS)], axis=0)
    o = _dot(jnp.concatenate(a_heads, axis=1), vbd)
    yield
    st = st_sc[...]
    o_ref[0] = o + _dot_nt(qx, st.astype(BF))
    upd = _dot(v_ref[0].T.astype(BF), kx)
    on_diag = (lax.broadcasted_iota(jnp.int32, (hv, hk), 0) // dv) == (lax.broadcasted_iota(jnp.int32, (hv, hk), 1) // dk)
    st_sc[...] = st * (d_first * d_second) + jnp.where(on_diag, upd, 0.0)
    yield


def _gla_kernel(fq, fk, fv, fgl, bq, bk, bv, bgl, w2f_ref, gbf_ref, w2b_ref, gbb_ref, of_ref, ob_ref, st_sc):
    @pl.when(pl.program_id(1) == 0)
    def _():
        st_sc[...] = jnp.zeros(st_sc.shape, F32)

    _interleave(_gla_direction(fq, fk, fv, fgl, w2f_ref, gbf_ref, of_ref, st_sc.at[0], reverse=False),
                _gla_direction(bq, bk, bv, bgl, w2b_ref, gbb_ref, ob_ref, st_sc.at[1], reverse=True))


def _gla(q, k, v, gl, w2f, w2b, gbf, gbb):
    b, s, _ = q.shape
    R = GLA_ROWS
    nb = s // R
    hk = GLA_HEADS * GLA_KDIM
    hv = GLA_HEADS * GLA_VDIM
    full = lambda bi, c: (0, 0)

    def views(cidx):
        row = lambda bi, c: (bi, cidx(c), 0)
        return [pl.BlockSpec((1, R, hk), row), pl.BlockSpec((1, R, hk), row), pl.BlockSpec((1, R, hv), row),
                pl.BlockSpec((1, R, LANES), row)]

    return pl.pallas_call(
        _gla_kernel,
        grid=(b, nb),
        in_specs=views(lambda c: c) + views(lambda c: nb - 1 - c) + [
            pl.BlockSpec((3 * LANES, hk), full), pl.BlockSpec((1, hk), full),
            pl.BlockSpec((3 * LANES, hk), full), pl.BlockSpec((1, hk), full)],
        out_specs=[pl.BlockSpec((1, R, hv), lambda bi, c: (bi, c, 0)),
                   pl.BlockSpec((1, R, hv), lambda bi, c: (bi, nb - 1 - c, 0))],
        out_shape=[jax.ShapeDtypeStruct((b, s, hv), F32)] * 2,
        scratch_shapes=[pltpu.VMEM((2, hv, hk), F32)],
        compiler_params=_cparams("parallel", "arbitrary"),
        name="gla",
    )(q, k, v, gl, q, k, v, gl, w2f, gbf, w2b, gbb)


def _sgu_kernel(sg_ref, g_ref, b_ref, ws_ref, bias_ref, o_ref):
    x = sg_ref[...]
    gel = x * (0.5 * (1.0 + jnp.tanh(math.sqrt(2.0 / math.pi) * (x + 0.044715 * (x * x * x)))))
    u = gel[:, :SGU_WIDTH]
    svn = _layernorm(gel[:, SGU_WIDTH:], g_ref[...], b_ref[...]).astype(BF)
    c = SGU_CHUNK
    for ci in range(x.shape[0] // c):
        rows = slice(ci * c, (ci + 1) * c)
        for gi in range(SGU_GROUPS):
            cols = slice(gi * SGU_GROUP_DIM, (gi + 1) * SGU_GROUP_DIM)
            sp = _dot(ws_ref[gi], svn[rows, cols]) + bias_ref[:, cols]
            o_ref[rows, cols] = u[rows, cols] * sp


def _sgu(sg, ln_g, ln_b, ws_bf, bias_full):
    t = sg.shape[0]
    c = SGU_CHUNK
    tm = min(ROW_TILE, t)
    row = lambda i: (i, 0)
    full = lambda i: (0, 0)
    return pl.pallas_call(
        _sgu_kernel,
        grid=(t // tm,),
        in_specs=[pl.BlockSpec((tm, 2 * SGU_WIDTH), row),
                  pl.BlockSpec((1, SGU_WIDTH), full), pl.BlockSpec((1, SGU_WIDTH), full),
                  pl.BlockSpec((SGU_GROUPS, c, c), lambda i: (0, 0, 0)),
                  pl.BlockSpec((c, SGU_WIDTH), full)],
        out_specs=pl.BlockSpec((tm, SGU_WIDTH), row),
        out_shape=jax.ShapeDtypeStruct((t, SGU_WIDTH), F32),
        compiler_params=_cparams("parallel"),
        name="sgu",
    )(sg, ln_g, ln_b, ws_bf, bias_full)


def _router_kernel(x_ref, wr_ref, route_ref, cnt_ref, base_sc):
    @pl.when(pl.program_id(0) == 0)
    def _():
        base_sc[...] = jnp.zeros(base_sc.shape, F32)

    tr = x_ref.shape[0]
    logits = jnp.dot(x_ref[...], wr_ref[...], precision=HI, preferred_element_type=F32)
    lane = lax.broadcasted_iota(jnp.int32, (tr, LANES), 1).astype(F32)
    lg = jnp.where(lane < N_EXPERTS, logits, -jnp.inf)
    m1 = jnp.max(lg, axis=1, keepdims=True)
    i1 = jnp.min(jnp.where(lg == m1, lane, float(LANES)), axis=1, keepdims=True)
    lg2 = jnp.where(lane == i1, -jnp.inf, lg)
    m2 = jnp.max(lg2, axis=1, keepdims=True)
    i2 = jnp.min(jnp.where(lg2 == m2, lane, float(LANES)), axis=1, keepdims=True)
    e = jnp.exp(m2 - m1)
    g1 = 1.0 / (1.0 + e)
    g2 = e / (1.0 + e)
    oh1 = (lane == i1).astype(F32)
    oh2 = (lane == i2).astype(F32)
    oh = oh1 + oh2
    row_i = lax.broadcasted_iota(jnp.int32, (tr, tr), 0)
    col_i = lax.broadcasted_iota(jnp.int32, (tr, tr), 1)
    before = _dot((col_i < row_i).astype(BF), oh.astype(BF)) + base_sc[...]
    r1 = jnp.sum(oh1 * before, axis=1, keepdims=True)
    r2 = jnp.sum(oh2 * before, axis=1, keepdims=True)
    base_sc[...] += jnp.sum(oh, axis=0, keepdims=True)
    route = jnp.zeros((tr, LANES), F32)
    for idx, val in enumerate((i1, i2, r1, r2, g1, g2)):
        route = jnp.where(lane == float(idx), val, route)
    route_ref[...] = route
    cnt_ref[...] = base_sc[...]


def _router(x, wr_pad):
    t, d = x.shape
    tr = min(ROUTE_TILE, t)
    return pl.pallas_call(
        _router_kernel,
        grid=(t // tr,),
        in_specs=[pl.BlockSpec((tr, d), lambda i: (i, 0)), pl.BlockSpec((d, LANES), lambda i: (0, 0))],
        out_specs=[pl.BlockSpec((tr, LANES), lambda i: (i, 0)), pl.BlockSpec((1, LANES), lambda i: (0, 0))],
        out_shape=[jax.ShapeDtypeStruct((t, LANES), F32), jax.ShapeDtypeStruct((1, LANES), F32)],
        scratch_shapes=[pltpu.VMEM((1, LANES), F32)],
        compiler_params=_cparams("arbitrary"),
        name="router",
    )(x, wr_pad)


def _piece_indices(pos, n_rows):
    base = jnp.arange(SC_ROW_SPLIT, dtype=jnp.int32)[:, None, None] * n_rows
    return (base + pos.T[None]).reshape(1, -1)


def _sc_scatter_rows(xp, idx, n_out):
    ns, t, w = xp.shape
    nblk = t // SC_WINDOW
    per_piece = idx.shape[1] // ns // SC_WINDOW
    mesh = plsc.VectorSubcoreMesh(core_axis_name="c", subcore_axis_name="s")

    @pl.kernel(out_type=jax.ShapeDtypeStruct((ns * n_out, w), xp.dtype), mesh=mesh)
    def k(x_hbm, i_hbm, o_hbm):
        def body(x_vmem, i_vmem):
            pltpu.sync_copy(x_vmem, o_hbm.at[i_vmem.at[0]])

        pltpu.emit_pipeline(
            body,
            grid=(idx.shape[1] // SC_WINDOW,),
            in_specs=[pl.BlockSpec((SC_WINDOW, w), index_map=lambda i: ((i // per_piece) * nblk + i % nblk, 0)),
                      pl.BlockSpec((1, SC_WINDOW), index_map=lambda i: (0, i))],
            out_specs=[],
            core_axis_name=("c", "s"),
            dimension_semantics=(pltpu.PARALLEL,),
        )(x_hbm, i_hbm)

    return k(xp.reshape(ns * t, w), idx).reshape(ns, n_out, w)


def _sc_gather_rows(yp, idx):
    ns, n, w = yp.shape
    mesh = plsc.VectorSubcoreMesh(core_axis_name="c", subcore_axis_name="s")

    @pl.kernel(out_type=jax.ShapeDtypeStruct((idx.shape[1], w), yp.dtype), mesh=mesh)
    def k(x_hbm, i_hbm, o_hbm):
        def body(i_vmem, o_vmem):
            pltpu.sync_copy(x_hbm.at[i_vmem.at[0]], o_vmem)

        pltpu.emit_pipeline(
            body,
            grid=(idx.shape[1] // SC_WINDOW,),
            in_specs=[pl.BlockSpec((1, SC_WINDOW), index_map=lambda i: (0, i))],
            out_specs=[pl.BlockSpec((SC_WINDOW, w), index_map=lambda i: (i, 0))],
            core_axis_name=("c", "s"),
            dimension_semantics=(pltpu.PARALLEL,),
        )(i_hbm, o_hbm)

    return k(yp.reshape(ns * n, w), idx)


def _combine_ln_kernel(x_ref, y_ref, route_ref, g_ref, beta_ref, o_ref):
    g1 = route_ref[:, 4:5]
    g2 = route_ref[:, 5:6]
    y = (g1 * jnp.concatenate([y_ref[j, 0] for j in range(SC_ROW_SPLIT)], axis=1)
         + g2 * jnp.concatenate([y_ref[j, 1] for j in range(SC_ROW_SPLIT)], axis=1))
    o_ref[...] = _layernorm(DN_ALPHA * x_ref[...] + y, g_ref[...], beta_ref[...])


def _combine_ln(x, y2, route, g, beta):
    t, d = x.shape
    tm = min(ROW_TILE, t)
    row = lambda i: (i, 0)
    full = lambda i: (0, 0)
    return pl.pallas_call(
        _combine_ln_kernel,
        grid=(t // tm,),
        in_specs=[pl.BlockSpec((tm, d), row),
                  pl.BlockSpec((SC_ROW_SPLIT, 2, tm, d // SC_ROW_SPLIT), lambda i: (0, 0, i, 0)),
                  pl.BlockSpec((tm, LANES), row), pl.BlockSpec((1, d), full), pl.BlockSpec((1, d), full)],
        out_specs=pl.BlockSpec((tm, d), row),
        out_shape=jax.ShapeDtypeStruct((t, d), F32),
        compiler_params=_cparams("parallel"),
        name="moe_combine_ln",
    )(x, y2, route, g, beta)


def _pad_cols(w, n):
    return jnp.pad(w, ((0, 0), (0, n - w.shape[1])))


def _prep_even(p):
    (w_in, q_norm, w_uq, kv_norm, w_ukv, conv_w, conv_b, dt_bias, a_log, d_skip, ssm_norm, w_out,
     ln1_g, ln1_b, w_gate, w_up, w_down, ln2_g, ln2_b) = p
    o = 0
    cq = w_in[:, o:o + Q_RANK]; o += Q_RANK
    ckv = w_in[:, o:o + KV_RANK]; o += KV_RANK
    kr = w_in[:, o:o + QK_ROPE]; o += QK_ROPE
    z = w_in[:, o:o + SSD_INNER]; o += SSD_INNER
    xbc = w_in[:, o:o + CONV_DIM]; o += CONV_DIM
    dt = w_in[:, o:]
    half = QK_ROPE // 2
    zeros = lambda n: jnp.zeros((w_in.shape[0], n), F32)
    kra = jnp.concatenate([zeros(QK_NOPE), kr, zeros(LANES - QK_NOPE - QK_ROPE)], axis=1)
    krb = jnp.concatenate([zeros(QK_NOPE), -kr[:, half:], kr[:, :half], zeros(LANES - QK_NOPE - QK_ROPE)], axis=1)
    w_in_p = jnp.concatenate([cq, ckv, kra, krb, z, xbc, _pad_cols(dt, LANES)], axis=1).astype(BF)

    wq = w_uq.reshape(Q_RANK, MLA_HEADS, QK_NOPE + QK_ROPE)
    nope, rope = wq[..., :QK_NOPE], wq[..., QK_NOPE:]
    zq = lambda n: jnp.zeros((Q_RANK, MLA_HEADS, n), F32)
    wqa = jnp.concatenate([nope, rope, zq(LANES - QK_NOPE - QK_ROPE)], axis=-1)
    wqb = jnp.concatenate([zq(QK_NOPE), -rope[..., half:], rope[..., :half], zq(LANES - QK_NOPE - QK_ROPE)], axis=-1)
    wkv = w_ukv.reshape(KV_RANK, MLA_HEADS, QK_NOPE + V_DIM)
    zk = jnp.zeros((KV_RANK, MLA_HEADS, LANES - QK_NOPE), F32)
    wk = jnp.concatenate([wkv[..., :QK_NOPE], zk], axis=-1)
    vv_t = jnp.transpose(wkv[..., QK_NOPE:], (1, 2, 0))
    wvt = jnp.concatenate([vv_t, jnp.zeros((MLA_HEADS, V_ROWS - V_DIM, KV_RANK), F32)], axis=1)
    vadd = jnp.tile((jnp.arange(V_ROWS) == V_DIM).astype(F32), MLA_HEADS)[:, None]
    hw = MLA_HEADS * LANES
    return dict(
        w_in=w_in_p, q_norm=q_norm[None], kv_norm=kv_norm[None],
        wqa=wqa.reshape(Q_RANK, hw).astype(BF), wqb=wqb.reshape(Q_RANK, hw).astype(BF),
        wk=wk.reshape(KV_RANK, hw).astype(BF), wvt=wvt.reshape(MLA_HEADS * V_ROWS, KV_RANK).astype(BF), vadd=vadd,
        conv_w=conv_w, conv_b=conv_b[None],
        dt_bias=_pad_cols(dt_bias.reshape(1, -1), LANES), a_log=_pad_cols(a_log.reshape(1, -1), LANES),
        d_skip=jnp.repeat(d_skip, SSD_HEAD_DIM)[None], ssm_norm=ssm_norm[None],
        wo_a=w_out[:MLA_HEADS * V_DIM].astype(BF), wo_b=w_out[MLA_HEADS * V_DIM:].astype(BF),
        ln1_g=ln1_g[None], ln1_b=ln1_b[None],
        wg=w_gate.astype(BF), wu=w_up.astype(BF), wd=w_down.astype(BF),
        ln2_g=ln2_g[None], ln2_b=ln2_b[None])


def _prep_odd(p):
    (w_in, gate_w2, gate_b, gla_norm, sgu_ln_g, sgu_ln_b, w_s, b_s, w_out, ln1_g, ln1_b,
     w_router, we_gate, we_up, we_down, ln2_g, ln2_b) = p
    hk = GLA_HEADS * GLA_KDIM
    hv = GLA_HEADS * GLA_VDIM
    o = 2 * hk + 2 * hv
    gl = w_in[:, o:o + 2 * GLA_GATE_RANK]
    w_in_p = jnp.concatenate([w_in[:, :o], _pad_cols(gl, LANES), w_in[:, o + 2 * GLA_GATE_RANK:]], axis=1).astype(BF)
    zr = lambda n: jnp.zeros((n, hk), F32)
    def pieces(w2):
        hi = w2.astype(BF)
        lo = (w2 - hi.astype(F32)).astype(BF)
        return jnp.concatenate([hi, hi, lo], axis=0)

    w2f = pieces(jnp.concatenate([gate_w2[0], zr(LANES - GLA_GATE_RANK)], axis=0))
    w2b = pieces(jnp.concatenate([zr(GLA_GATE_RANK), gate_w2[1], zr(LANES - 2 * GLA_GATE_RANK)], axis=0))
    bias_full = jnp.repeat(b_s.T, SGU_GROUP_DIM, axis=1)
    return dict(
        w_in=w_in_p, w2f=w2f, w2b=w2b, gbf=gate_b[0][None], gbb=gate_b[1][None], gla_norm=gla_norm[None],
        sgu_g=sgu_ln_g[None], sgu_b=sgu_ln_b[None], ws=w_s.astype(BF), sgu_bias=bias_full,
        wo_a=w_out[:hv].astype(BF), wo_b=w_out[hv:].astype(BF), ln1_g=ln1_g[None], ln1_b=ln1_b[None],
        w_router=_pad_cols(w_router, LANES),
        wg=we_gate.astype(BF), wu=we_up.astype(BF), wd=we_down.astype(BF),
        ln2_g=ln2_g[None], ln2_b=ln2_b[None])


def _rope_tables(s):
    half = QK_ROPE // 2
    inv = jnp.exp(-math.log(ROPE_THETA) * jnp.arange(half, dtype=F32) / half)
    ang = jnp.arange(s, dtype=F32)[:, None] * inv[None, :]
    cos, sin = jnp.cos(ang), jnp.sin(ang)
    pad = LANES - QK_NOPE - QK_ROPE
    cos_t = jnp.concatenate([jnp.ones((s, QK_NOPE), F32), cos, cos, jnp.ones((s, pad), F32)], axis=1)
    sin_t = jnp.concatenate([jnp.zeros((s, QK_NOPE), F32), sin, sin, jnp.zeros((s, pad), F32)], axis=1)
    return cos_t, sin_t


def _even_layer(x, w, b, s):
    t = b * s
    mla_in, z, xbc, dt = _proj(x, w["w_in"], (Q_RANK + KV_RANK + 2 * LANES, SSD_INNER, CONV_DIM, LANES),
                               (F32, F32, F32, F32))
    cos_t, sin_t = _rope_tables(s)
    q, k, vt = _mla_prep(mla_in, cos_t, sin_t, w["q_norm"], w["kv_norm"], w["wqa"], w["wqb"], w["wk"], w["wvt"],
                         w["vadd"], b, s)
    hw = MLA_HEADS * LANES
    o_attn = _flash(q.reshape(b, s, hw), k.reshape(b, s, hw), vt)
    y_f, y_b = _ssd(xbc.reshape(b, s, CONV_DIM), dt.reshape(b, s, LANES),
                    w["conv_w"], w["conv_b"], w["dt_bias"], w["a_log"], w["d_skip"])
    mix = (o_attn.reshape(t, -1), y_f.reshape(t, -1), y_b.reshape(t, -1), z, w["ssm_norm"])
    x1 = _outproj_ln(mix, x, w["wo_a"], w["wo_b"], w["ln1_g"], w["ln1_b"], even=True)
    return _ffn_ln(x1, w["wg"], w["wu"], w["wd"], w["ln2_g"], w["ln2_b"])


def _moe(x1, x1p, w):
    t, d = x1.shape
    route, cnt = _router(x1, w["w_router"])
    eid = route[:, 0:2].astype(jnp.int32)
    rank = route[:, 2:4].astype(jnp.int32)
    counts = cnt[0, :N_EXPERTS].astype(jnp.int32)
    tm = MOE_TILE
    padded = ((counts + tm - 1) // tm) * tm
    ends = jnp.cumsum(padded)
    offs = ends - padded
    pos = offs[eid] + rank
    p_rows = 2 * t + N_EXPERTS * tm
    tiles = jnp.arange(p_rows // tm, dtype=jnp.int32)
    tile_ends = ends // tm
    tile_expert = jnp.minimum(jnp.sum(tiles[:, None] >= tile_ends[None, :], axis=1), N_EXPERTS - 1).astype(jnp.int32)
    tile_valid = (tiles < tile_ends[-1]).astype(jnp.int32)
    idx = _piece_indices(pos, p_rows)
    xs = _sc_scatter_rows(x1p, idx, p_rows)
    ys = _moe_ffn(xs, tile_expert, tile_valid, w["wg"], w["wu"], w["wd"])
    y2 = _sc_gather_rows(ys, idx).reshape(SC_ROW_SPLIT, 2, t, d // SC_ROW_SPLIT)
    return _combine_ln(x1, y2, route, w["ln2_g"], w["ln2_b"])


def _odd_layer(x, w, b, s):
    t = b * s
    hk = GLA_HEADS * GLA_KDIM
    hv = GLA_HEADS * GLA_VDIM
    q, k, v, r, gl, sg = _proj(x, w["w_in"], (hk, hk, hv, hv, LANES, 2 * SGU_WIDTH), (F32,) * 6)
    o_f, o_b = _gla(q.reshape(b, s, hk), k.reshape(b, s, hk), v.reshape(b, s, hv), gl.reshape(b, s, LANES),
                    w["w2f"], w["w2b"], w["gbf"], w["gbb"])
    o_sgu = _sgu(sg, w["sgu_g"], w["sgu_b"], w["ws"], w["sgu_bias"])
    mix = (o_f.reshape(t, hv), o_b.reshape(t, hv), r, w["gla_norm"], o_sgu)
    x1, x1p = _outproj_ln(mix, x, w["wo_a"], w["wo_b"], w["ln1_g"], w["ln1_b"], even=False)
    return _moe(x1, x1p, w)


def _trunk(x, ev_w, od_w):
    b, s, d = x.shape
    x = x.reshape(b * s, d)
    for i in range(DEPTH):
        if i % 2 == 0:
            x = _even_layer(x, ev_w[i // 2], b, s)
        else:
            x = _odd_layer(x, od_w[i // 2], b, s)
    return x.reshape(b, s, d)


def kernel(x_prompt, x_sample, ev_w_in, ev_q_norm, ev_w_uq, ev_kv_norm, ev_w_ukv, ev_conv_w, ev_conv_b, ev_dt_bias, ev_a_log, ev_d_skip, ev_ssm_norm, ev_w_out, ev_ln1_g, ev_ln1_b, ev_w_gate, ev_w_up, ev_w_down, ev_ln2_g, ev_ln2_b, od_w_in, od_gate_w2, od_gate_b, od_gla_norm, od_sgu_ln_g, od_sgu_ln_b, od_w_s, od_b_s, od_w_out, od_ln1_g, od_ln1_b, od_w_router, od_we_gate, od_we_up, od_we_down, od_ln2_g, od_ln2_b):
    ev = (ev_w_in, ev_q_norm, ev_w_uq, ev_kv_norm, ev_w_ukv, ev_conv_w, ev_conv_b, ev_dt_bias,
          ev_a_log, ev_d_skip, ev_ssm_norm, ev_w_out, ev_ln1_g, ev_ln1_b, ev_w_gate, ev_w_up,
          ev_w_down, ev_ln2_g, ev_ln2_b)
    od = (od_w_in, od_gate_w2, od_gate_b, od_gla_norm, od_sgu_ln_g, od_sgu_ln_b, od_w_s, od_b_s,
          od_w_out, od_ln1_g, od_ln1_b, od_w_router, od_we_gate, od_we_up, od_we_down,
          od_ln2_g, od_ln2_b)
    ev_w = [_prep_even(tuple(t[i] for t in ev)) for i in range(ev_w_in.shape[0])]
    od_w = [_prep_odd(tuple(t[i] for t in od)) for i in range(od_w_in.shape[0])]
    return (_trunk(x_prompt, ev_w, od_w), _trunk(x_sample, ev_w, od_w))
```

```python
import functools
import math

import jax
import jax.numpy as jnp
from jax import lax
from jax.experimental import pallas as pl
from jax.experimental.pallas import tpu as pltpu
from jax.experimental.pallas import tpu_sc as plsc

BF = jnp.bfloat16
F32 = jnp.float32

D_MODEL = 1024
DEPTH = 4
MLA_HEADS = 8
QK_NOPE = 64
QK_ROPE = 32
V_DIM = 64
Q_RANK = 256
KV_RANK = 128
ROPE_THETA = 10000.0
SSD_HEADS = 8
SSD_HEAD_DIM = 64
SSD_GROUPS = 2
SSD_STATE = 64
SSD_CHUNK = 128
CONV_K = 5
SSD_INNER = SSD_HEADS * SSD_HEAD_DIM
SSD_HPG = SSD_HEADS // SSD_GROUPS
CONV_DIM = SSD_INNER + 2 * SSD_GROUPS * SSD_STATE
GLA_HEADS = 4
GLA_KDIM = 64
GLA_VDIM = 128
GLA_GATE_RANK = 16
GLA_TAU = 16.0
GLA_CHUNK = 64
SGU_GROUPS = 4
SGU_CHUNK = 128
SGU_GROUP_DIM = 128
SGU_WIDTH = SGU_GROUPS * SGU_GROUP_DIM
D_FF = 2816
N_EXPERTS = 8
D_FF_EXPERT = 3584
DN_ALPHA = (2 * DEPTH) ** 0.25
EPS = 1e-5

LANES = 128
SUBLANES = 8
VMEM_LIMIT = 56 * 1024 * 1024
SC_WINDOW = 128
SC_ROW_SPLIT = 4

ROW_TILE = 512
FF_CHUNK = 256
ATT_TQ = 512
ATT_TK = 512
MOE_TILE = 512
ROUTE_TILE = 512
GLA_ROWS = 2 * GLA_CHUNK
V_ROWS = 80


def _cparams(*sem):
    return pltpu.CompilerParams(dimension_semantics=sem, vmem_limit_bytes=VMEM_LIMIT)


def _resident(shape, index_map):
    return pl.BlockSpec(shape, index_map, pipeline_mode=pl.Buffered(1))


def _rms(x):
    return x * lax.rsqrt(jnp.mean(x * x, axis=-1, keepdims=True) + EPS)


def _layernorm(x, g, b):
    mu = jnp.mean(x, axis=-1, keepdims=True)
    xc = x - mu
    var = jnp.mean(xc * xc, axis=-1, keepdims=True)
    return xc * lax.rsqrt(var + EPS) * g + b


def _silu(x):
    return x * jax.nn.sigmoid(x)


def _softplus(x):
    return jnp.maximum(x, 0.0) + jnp.log1p(jnp.exp(-jnp.abs(x)))


def _dot(a, b):
    return jnp.dot(a, b, preferred_element_type=F32)


def _dot_nt(a, b):
    return lax.dot_general(a, b, (((1,), (1,)), ((), ())), preferred_element_type=F32)


def _proj_kernel(x_ref, w_ref, *o_refs, splits):
    xb = x_ref[...].astype(BF)
    off = 0
    for o_ref, n in zip(o_refs, splits):
        o_ref[...] = _dot(xb, w_ref[:, off:off + n]).astype(o_ref.dtype)
        off += n


def _proj(x2d, w_bf, splits, dtypes):
    t, k = x2d.shape
    n = w_bf.shape[1]
    tm = min(ROW_TILE, t)
    return pl.pallas_call(
        functools.partial(_proj_kernel, splits=splits),
        grid=(t // tm,),
        in_specs=[pl.BlockSpec((tm, k), lambda i: (i, 0)), _resident((k, n), lambda i: (0, 0))],
        out_specs=[pl.BlockSpec((tm, s), lambda i: (i, 0)) for s in splits],
        out_shape=[jax.ShapeDtypeStruct((t, s), d) for s, d in zip(splits, dtypes)],
        compiler_params=_cparams("parallel"),
        name="proj_in",
    )(x2d, w_bf)


def _mla_prep_kernel(m_ref, cos_ref, sin_ref, qn_ref, kvn_ref, wqa_ref, wqb_ref, wk_ref, wvt_ref, vadd_ref,
                     q_ref, k_ref, vt_ref):
    cq = m_ref[:, 0:Q_RANK]
    ckv = m_ref[:, Q_RANK:Q_RANK + KV_RANK]
    kra = m_ref[:, Q_RANK + KV_RANK:Q_RANK + KV_RANK + LANES]
    krb = m_ref[:, Q_RANK + KV_RANK + LANES:Q_RANK + KV_RANK + 2 * LANES]
    cos = cos_ref[...]
    sin = sin_ref[...]
    cos8 = jnp.concatenate([cos] * MLA_HEADS, axis=1)
    sin8 = jnp.concatenate([sin] * MLA_HEADS, axis=1)
    cqn = (_rms(cq) * qn_ref[...]).astype(BF)
    q = _dot(cqn, wqa_ref[...]) * cos8 + _dot(cqn, wqb_ref[...]) * sin8
    q_ref[...] = (q * ((QK_NOPE + QK_ROPE) ** -0.5 * math.log2(math.e))).astype(BF)
    ckvn = (_rms(ckv) * kvn_ref[...]).astype(BF)
    kr = kra * cos + krb * sin
    k = _dot(ckvn, wk_ref[...]) + jnp.concatenate([kr] * MLA_HEADS, axis=1)
    k_ref[...] = k.astype(BF)
    vt_ref[0] = (_dot_nt(wvt_ref[...], ckvn) + vadd_ref[...]).astype(BF)


def _mla_prep(mla_in, cos_t, sin_t, q_norm, kv_norm, wqa, wqb, wk, wvt, vadd, batch, seq):
    t = mla_in.shape[0]
    tm = min(ROW_TILE, seq)
    nseq = seq // tm
    hw = MLA_HEADS * LANES
    vr = MLA_HEADS * V_ROWS
    full = lambda i: (0, 0)
    return pl.pallas_call(
        _mla_prep_kernel,
        grid=(t // tm,),
        in_specs=[pl.BlockSpec((tm, mla_in.shape[1]), lambda i: (i, 0)),
                  pl.BlockSpec((tm, LANES), lambda i: (i % nseq, 0)),
                  pl.BlockSpec((tm, LANES), lambda i: (i % nseq, 0)),
                  pl.BlockSpec((1, Q_RANK), full), pl.BlockSpec((1, KV_RANK), full),
                  pl.BlockSpec((Q_RANK, hw), full), pl.BlockSpec((Q_RANK, hw), full),
                  pl.BlockSpec((KV_RANK, hw), full), pl.BlockSpec((vr, KV_RANK), full),
                  pl.BlockSpec((vr, 1), full)],
        out_specs=[pl.BlockSpec((tm, hw), lambda i: (i, 0)), pl.BlockSpec((tm, hw), lambda i: (i, 0)),
                   pl.BlockSpec((1, vr, tm), lambda i: (i // nseq, 0, i % nseq))],
        out_shape=[jax.ShapeDtypeStruct((t, hw), BF), jax.ShapeDtypeStruct((t, hw), BF),
                   jax.ShapeDtypeStruct((batch, vr, seq), BF)],
        compiler_params=_cparams("parallel"),
        name="mla_prep",
    )(mla_in, cos_t, sin_t, q_norm, kv_norm, wqa, wqb, wk, wvt, vadd)


def _flash_kernel(q_ref, k_ref, vt_ref, o_ref, m_sc, acc_sc, s_sc, *, tk, nk, unroll):
    for h in range(2):
        m_sc[h] = jnp.full(m_sc.shape[1:], -jnp.inf, F32)
        acc_sc[h] = jnp.zeros(acc_sc.shape[1:], F32)

    def scores(j, slot):
        off = pl.multiple_of(j * tk, tk)
        for h in range(2):
            lanes = slice(h * LANES, (h + 1) * LANES)
            s_sc[slot, h] = _dot_nt(k_ref[0, pl.ds(off, tk), lanes], q_ref[0, :, lanes])

    def consume(j, slot):
        off = pl.multiple_of(j * tk, tk)
        for h in range(2):
            st = s_sc[slot, h]
            m_prev = m_sc[h]
            m_new = jnp.maximum(m_prev, jnp.max(st, axis=0, keepdims=True))
            p = jnp.exp2(st - m_new[0:1, :]).astype(BF)
            alpha = jnp.exp2(m_prev - m_new)
            pv = _dot(vt_ref[0, h * V_ROWS:(h + 1) * V_ROWS, pl.ds(off, tk)], p)
            acc_sc[h] = alpha[0:1, :] * acc_sc[h] + pv
            m_sc[h] = m_new

    scores(0, 0)

    def body(jj, carry):
        j = unroll * jj
        for u in range(unroll):
            scores(jnp.minimum(j + u + 1, nk - 1), (u + 1) % 2)
            consume(j + u, u % 2)
        return carry

    lax.fori_loop(0, nk // unroll, body, 0)
    outs = []
    for h in range(2):
        acc = acc_sc[h]
        outs.append(acc[0:V_DIM, :] / acc[V_DIM:V_DIM + 1, :])
    o_ref[0] = jnp.concatenate(outs, axis=0).T.astype(o_ref.dtype)


def _flash(q, k, vt):
    b, s, hw = q.shape
    tq = min(ATT_TQ, s)
    tk = min(ATT_TK, s)
    nk = s // tk
    unroll = next(u for u in (8, 4, 2, 1) if nk % u == 0)
    pairs = MLA_HEADS // 2
    return pl.pallas_call(
        functools.partial(_flash_kernel, tk=tk, nk=nk, unroll=unroll),
        grid=(b, pairs, s // tq),
        in_specs=[pl.BlockSpec((1, tq, 2 * LANES), lambda bi, hp, i: (bi, i, hp)),
                  pl.BlockSpec((1, s, 2 * LANES), lambda bi, hp, i: (bi, 0, hp)),
                  pl.BlockSpec((1, 2 * V_ROWS, s), lambda bi, hp, i: (bi, hp, 0))],
        out_specs=pl.BlockSpec((1, tq, LANES), lambda bi, hp, i: (bi, i, hp)),
        out_shape=jax.ShapeDtypeStruct((b, s, MLA_HEADS * V_DIM), BF),
        scratch_shapes=[pltpu.VMEM((2, SUBLANES, tq), F32), pltpu.VMEM((2, V_ROWS, tq), F32),
                        pltpu.VMEM((2, 2, tk, tq), F32)],
        compiler_params=_cparams("parallel", "parallel", "arbitrary"),
        name="mla_flash",
    )(q, k, vt)


def _split_bf16(x, pieces):
    out = []
    for _ in range(pieces):
        p = x.astype(BF)
        out.append(p)
        x = x - p.astype(F32)
    return jnp.concatenate(out, axis=1)


def _spread_matrix(first_lane, heads, width, pieces):
    src = jnp.arange(LANES)[:, None] - first_lane
    dst = jnp.arange(heads * width)[None, :] // width
    return jnp.tile((src == dst).astype(BF), (pieces, 1))


def _ssd_direction(xc_ref, xp_ref, xn_ref, dt_ref, ep_ref, el_ref, cw_ref, cb_ref, dtb_ref, alog_ref, dskip_ref,
                   o_ref, xe_sc, st_sc, *, reverse, cc, nc):
    L = SSD_CHUNK
    N = SSD_STATE
    P = SSD_HEAD_DIM
    xe_sc[0:SUBLANES, :] = jnp.where(cc > 0, xp_ref[0], 0.0)
    xe_sc[SUBLANES:SUBLANES + L, :] = xc_ref[0]
    xe_sc[SUBLANES + L:2 * SUBLANES + L, :] = jnp.where(cc < nc - 1, xn_ref[0], 0.0)
    conv = cb_ref[...] + cw_ref[0:1, :] * xe_sc[pl.ds(SUBLANES - CONV_K // 2, L), :]
    for j in range(1, CONV_K):
        conv = conv + cw_ref[j:j + 1, :] * xe_sc[pl.ds(SUBLANES - CONV_K // 2 + j, L), :]
    yield
    xbc = _silu(conv)
    xs = xbc[:, :SSD_INNER]
    bc = xbc[:, SSD_INNER:]
    bc_t = bc.T

    lane = lax.broadcasted_iota(jnp.int32, (L, LANES), 1)
    dtv = _softplus(dt_ref[0] + dtb_ref[...])
    a = jnp.where(lane[0:1] < 2 * SSD_HEADS, -jnp.exp(alog_ref[...]), 0.0)
    dta = dtv * a
    row_i = lax.broadcasted_iota(jnp.int32, (L, L), 0)
    col_i = lax.broadcasted_iota(jnp.int32, (L, L), 1)
    causal = (col_i >= row_i) if reverse else (col_i <= row_i)
    acs3 = _dot(causal.astype(BF), _split_bf16(dta, 3))
    yield
    acs = acs3[:, 0:LANES] + acs3[:, LANES:2 * LANES] + acs3[:, 2 * LANES:]
    acs_t = acs.T
    d0 = SSD_HEADS if reverse else 0
    end = 0 if reverse else L - 1
    tot = acs[end:end + 1, :]
    yield

    stacked = jnp.concatenate([dtv, jnp.exp(acs), jnp.exp(tot - acs),
                               jnp.broadcast_to(jnp.exp(tot), (2 * SUBLANES, LANES))], axis=0)
    spread = _dot(_split_bf16(stacked, 2), ep_ref[...])
    dt_x = spread[0:L]
    ea_x = spread[L:2 * L]
    eb_x = spread[2 * L:3 * L]
    et_x = spread[3 * L:3 * L + 1]
    col_x = _dot(_split_bf16(acs, 3), el_ref[...])
    yield
    row_x = jnp.concatenate([jnp.broadcast_to(acs_t[d0 + h:d0 + h + 1, :], (L, L)) for h in range(SSD_HEADS)], axis=1)
    causal_x = jnp.concatenate([causal] * SSD_HEADS, axis=1)
    decay_x = jnp.exp(jnp.where(causal_x, col_x - row_x, -jnp.inf))
    xdt = xs * dt_x
    xdt_b = xdt.astype(BF)
    xw_b = (xdt * eb_x).astype(BF)
    gw = SSD_HPG * P
    lane_g = lax.broadcasted_iota(jnp.int32, (L, gw), 1)
    yield

    ys = []
    for g in range(SSD_GROUPS):
        bm_g = bc[:, g * N:(g + 1) * N].astype(BF)
        cm_g = bc[:, (SSD_GROUPS + g) * N:(SSD_GROUPS + g + 1) * N].astype(BF)
        bm_t_g = bc_t[g * N:(g + 1) * N, :].astype(BF)
        cb = _dot_nt(cm_g, bm_g)
        m_g = (jnp.concatenate([cb] * SSD_HPG, axis=1) * decay_x[:, g * SSD_HPG * L:(g + 1) * SSD_HPG * L]).astype(BF)
        xg = xdt_b[:, g * gw:(g + 1) * gw]
        xbd = jnp.concatenate([jnp.where(lane_g // P == j, xg, jnp.zeros_like(xg)) for j in range(SSD_HPG)], axis=0)
        y_diag = _dot(m_g, xbd)
        states_t = _dot(bm_t_g, xw_b[:, g * gw:(g + 1) * gw])
        prev_t = st_sc[g]
        y_off = _dot(cm_g, prev_t.astype(BF)) * ea_x[:, g * gw:(g + 1) * gw]
        st_sc[g] = prev_t * et_x[:, g * gw:(g + 1) * gw] + states_t
        ys.append(y_diag + y_off)
        yield
    y = jnp.concatenate(ys, axis=1)
    if not reverse:
        y = y + dskip_ref[...] * xs
    o_ref[0] = y.astype(o_ref.dtype)


def _interleave(*tracers):
    live = list(tracers)
    while live:
        for g in list(live):
            try:
                next(g)
            except StopIteration:
                live.remove(g)


def _ssd_kernel(fxc, fxp, fxn, fdt, bxc, bxp, bxn, bdt, epf_ref, elf_ref, epb_ref, elb_ref,
                cw_ref, cb_ref, dtb_ref, alog_ref, dskip_ref, yf_ref, yb_ref, xe_sc, st_sc, *, nc):
    c = pl.program_id(1)

    @pl.when(c == 0)
    def _():
        st_sc[...] = jnp.zeros(st_sc.shape, F32)

    shared = (cw_ref, cb_ref, dtb_ref, alog_ref, dskip_ref)
    _interleave(
        _ssd_direction(fxc, fxp, fxn, fdt, epf_ref, elf_ref, *shared, yf_ref, xe_sc.at[0], st_sc.at[0],
                       reverse=False, cc=c, nc=nc),
        _ssd_direction(bxc, bxp, bxn, bdt, epb_ref, elb_ref, *shared, yb_ref, xe_sc.at[1], st_sc.at[1],
                       reverse=True, cc=nc - 1 - c, nc=nc))


def _ssd(xbc, dt, conv_w, conv_b, dt_bias, a_log, d_skip):
    b, s, _ = xbc.shape
    L = SSD_CHUNK
    nc = s // L
    hb = L // SUBLANES
    nhb = s // SUBLANES
    full = lambda bi, c: (0, 0)

    def views(cidx):
        row = lambda bi, c: (bi, cidx(c), 0)
        return [pl.BlockSpec((1, L, CONV_DIM), row),
                pl.BlockSpec((1, SUBLANES, CONV_DIM), lambda bi, c: (bi, jnp.maximum(cidx(c) * hb - 1, 0), 0)),
                pl.BlockSpec((1, SUBLANES, CONV_DIM), lambda bi, c: (bi, jnp.minimum((cidx(c) + 1) * hb, nhb - 1), 0)),
                pl.BlockSpec((1, L, LANES), row)]

    spreads = [_spread_matrix(d0, SSD_HEADS, width, pieces)
               for d0 in (0, SSD_HEADS) for width, pieces in ((SSD_HEAD_DIM, 2), (L, 3))]
    return pl.pallas_call(
        functools.partial(_ssd_kernel, nc=nc),
        grid=(b, nc),
        in_specs=views(lambda c: c) + views(lambda c: nc - 1 - c) + [pl.BlockSpec(m.shape, full) for m in spreads] + [
            pl.BlockSpec((CONV_K, CONV_DIM), full), pl.BlockSpec((1, CONV_DIM), full),
            pl.BlockSpec((1, LANES), full), pl.BlockSpec((1, LANES), full), pl.BlockSpec((1, SSD_INNER), full)],
        out_specs=[pl.BlockSpec((1, L, SSD_INNER), lambda bi, c: (bi, c, 0)),
                   pl.BlockSpec((1, L, SSD_INNER), lambda bi, c: (bi, nc - 1 - c, 0))],
        out_shape=[jax.ShapeDtypeStruct((b, s, SSD_INNER), BF)] * 2,
        scratch_shapes=[pltpu.VMEM((2, L + 2 * SUBLANES, CONV_DIM), F32),
                        pltpu.VMEM((2, SSD_GROUPS, SSD_STATE, SSD_HPG * SSD_HEAD_DIM), F32)],
        compiler_params=_cparams("parallel", "arbitrary"),
        name="ssd",
    )(xbc, xbc, xbc, dt, xbc, xbc, xbc, dt, *spreads, conv_w, conv_b, dt_bias, a_log, d_skip)


def _to_pieces(ref, y):
    w = ref.shape[-1]
    for j in range(SC_ROW_SPLIT):
        ref[j] = y[:, j * w:(j + 1) * w].astype(ref.dtype)


def _from_pieces(ref):
    return jnp.concatenate([ref[j] for j in range(SC_ROW_SPLIT)], axis=1)


def _outproj_ln_kernel(*refs, even):
    if even:
        oa_ref, yf_ref, yb_ref, z_ref, nrm_ref = refs[:5]
        a = oa_ref[...]
        y_ssd = yf_ref[...].astype(F32) + yb_ref[...].astype(F32)
        b = _rms(y_ssd * _silu(z_ref[...].astype(F32))) * nrm_ref[...]
    else:
        of_ref, ob_ref, r_ref, gn_ref, sgu_ref = refs[:5]
        o = of_ref[...].astype(F32) + ob_ref[...].astype(F32)
        o = jnp.concatenate([_rms(o[:, h * GLA_VDIM:(h + 1) * GLA_VDIM]) for h in range(GLA_HEADS)], axis=1)
        a = o * gn_ref[...] * _silu(r_ref[...].astype(F32))
        b = sgu_ref[...]
    x_ref, wa_ref, wb_ref, g_ref, beta_ref, o_ref = refs[5:11]
    y = _dot(a.astype(BF), wa_ref[...]) + _dot(b.astype(BF), wb_ref[...])
    out = _layernorm(DN_ALPHA * x_ref[...] + y, g_ref[...], beta_ref[...])
    o_ref[...] = out
    for p_ref in refs[11:]:
        _to_pieces(p_ref, out)


def _outproj_ln(mix, x, wa, wb, g, beta, even):
    t, d = x.shape
    tm = min(ROW_TILE, t)
    row = lambda i: (i, 0)
    full = lambda i: (0, 0)
    out_specs = [pl.BlockSpec((tm, d), row)]
    out_shape = [jax.ShapeDtypeStruct((t, d), F32)]
    if not even:
        out_specs.append(pl.BlockSpec((SC_ROW_SPLIT, tm, d // SC_ROW_SPLIT), lambda i: (0, i, 0)))
        out_shape.append(jax.ShapeDtypeStruct((SC_ROW_SPLIT, t, d // SC_ROW_SPLIT), F32))
    mix_specs = [pl.BlockSpec((1, m.shape[1]), full) if m.shape[0] == 1 else pl.BlockSpec((tm, m.shape[1]), row)
                 for m in mix]
    res = pl.pallas_call(
        functools.partial(_outproj_ln_kernel, even=even),
        grid=(t // tm,),
        in_specs=mix_specs + [pl.BlockSpec((tm, d), row),
                              _resident(wa.shape, full), _resident(wb.shape, full),
                              pl.BlockSpec((1, d), full), pl.BlockSpec((1, d), full)],
        out_specs=out_specs,
        out_shape=out_shape,
        compiler_params=_cparams("parallel"),
        name="outproj_ln",
    )(*mix, x, wa, wb, g, beta)
    return res[0] if even else res


def _swiglu_acc(xb, wg_ref, wu_ref, wd_ref, acc_sc, nf):
    for f in range(nf):
        cols = slice(f * FF_CHUNK, (f + 1) * FF_CHUNK)
        h = _silu(_dot(xb, wg_ref[:, cols])) * _dot(xb, wu_ref[:, cols])
        part = _dot(h.astype(BF), wd_ref[cols, :])
        if f == 0:
            acc_sc[...] = part
        else:
            acc_sc[...] += part


def _ffn_ln_kernel(x_ref, wg_ref, wu_ref, wd_ref, g_ref, beta_ref, o_ref, acc_sc, *, nf):
    x = x_ref[...]
    _swiglu_acc(x.astype(BF), wg_ref, wu_ref, wd_ref, acc_sc, nf)
    o_ref[...] = _layernorm(DN_ALPHA * x + acc_sc[...], g_ref[...], beta_ref[...])


def _ffn_ln(x, wg, wu, wd, g, beta):
    t, d = x.shape
    f = wg.shape[1]
    tm = min(ROW_TILE, t)
    row = lambda i: (i, 0)
    full = lambda i: (0, 0)
    return pl.pallas_call(
        functools.partial(_ffn_ln_kernel, nf=f // FF_CHUNK),
        grid=(t // tm,),
        in_specs=[pl.BlockSpec((tm, d), row),
                  _resident((d, f), full), _resident((d, f), full), _resident((f, d), full),
                  pl.BlockSpec((1, d), full), pl.BlockSpec((1, d), full)],
        out_specs=pl.BlockSpec((tm, d), row),
        out_shape=jax.ShapeDtypeStruct((t, d), F32),
        scratch_shapes=[pltpu.VMEM((tm, d), F32)],
        compiler_params=_cparams("parallel"),
        name="ffn_ln",
    )(x, wg, wu, wd, g, beta)


def _moe_ffn_kernel(te_ref, tv_ref, xs_ref, wg_ref, wu_ref, wd_ref, o_ref, acc_sc, *, nf):
    @pl.when(tv_ref[pl.program_id(0)] > 0)
    def _():
        _swiglu_acc(_from_pieces(xs_ref).astype(BF), wg_ref, wu_ref, wd_ref, acc_sc, nf)
        _to_pieces(o_ref, acc_sc[...])


def _moe_ffn(xs, tile_expert, tile_valid, wg, wu, wd):
    ns, p, w = xs.shape
    d = ns * w
    f = wg.shape[2]
    tm = MOE_TILE
    row = lambda i, te, tv: (0, i, 0)
    grid_spec = pltpu.PrefetchScalarGridSpec(
        num_scalar_prefetch=2,
        grid=(p // tm,),
        in_specs=[pl.BlockSpec((ns, tm, w), row),
                  _resident((None, d, f), lambda i, te, tv: (te[i], 0, 0)),
                  _resident((None, d, f), lambda i, te, tv: (te[i], 0, 0)),
                  _resident((None, f, d), lambda i, te, tv: (te[i], 0, 0))],
        out_specs=pl.BlockSpec((ns, tm, w), row),
        scratch_shapes=[pltpu.VMEM((tm, d), F32)],
    )
    return pl.pallas_call(
        functools.partial(_moe_ffn_kernel, nf=f // FF_CHUNK),
        grid_spec=grid_spec,
        out_shape=jax.ShapeDtypeStruct((ns, p, w), F32),
        compiler_params=_cparams("arbitrary"),
        name="moe_ffn",
    )(tile_expert, tile_valid, xs, wg, wu, wd)


def _gla_direction(q_ref, k_ref, v_ref, gl_ref, w2_ref, gb_ref, o_ref, st_sc, *, reverse):
    L = GLA_CHUNK
    R = GLA_ROWS
    dk = GLA_KDIM
    dv = GLA_VDIM
    hk = GLA_HEADS * dk
    hv = GLA_HEADS * dv
    g2 = _split_bf16(gl_ref[0], 2)
    pre = _dot(jnp.concatenate([g2, g2[:, :LANES]], axis=1), w2_ref[...]) + gb_ref[...]
    yield
    lg = -_softplus(-pre) * (1.0 / GLA_TAU)
    row_i = lax.broadcasted_iota(jnp.int32, (R, R), 0)
    col_i = lax.broadcasted_iota(jnp.int32, (R, R), 1)
    intra = ((row_i // L) == (col_i // L)) & ((col_i >= row_i) if reverse else (col_i <= row_i))
    bc3 = _dot(intra.astype(BF), _split_bf16(lg, 3))
    yield
    bc = bc3[:, 0:hk] + bc3[:, hk:2 * hk] + bc3[:, 2 * hk:]
    mid = (L // 2 - 1) if reverse else (L // 2)
    end = 0 if reverse else (L - 1)
    ref_b = jnp.concatenate([jnp.broadcast_to(bc[ci * L + mid:ci * L + mid + 1], (L, hk))
                             for ci in range(R // L)], axis=0)
    end_b = jnp.concatenate([jnp.broadcast_to(bc[ci * L + end:ci * L + end + 1], (L, hk))
                             for ci in range(R // L)], axis=0)
    q = q_ref[0] * (dk ** -0.5)
    k = k_ref[0]
    qi = q * jnp.exp(bc - ref_b)
    ki = k * jnp.exp(ref_b - bc)
    qe = q * jnp.exp(bc)
    kd = k * jnp.exp(end_b - bc)
    first = 1 if reverse else 0
    in_first = (lax.broadcasted_iota(jnp.int32, (R, hk), 0) // L) == first
    d_first = jnp.exp(bc[first * L + end:first * L + end + 1])
    d_second = jnp.exp(bc[(1 - first) * L + end:(1 - first) * L + end + 1])
    qx = jnp.where(in_first, qe, qe * d_first).astype(BF)
    kx = jnp.where(in_first, kd * d_second, kd).astype(BF)
    qe_m = jnp.where(in_first, 0.0, qe)
    kd_m = jnp.where(in_first, kd, 0.0)
    yield

    a_heads = []
    for h in range(GLA_HEADS):
        kl = slice(h * dk, (h + 1) * dk)
        lhs = jnp.concatenate([qi[:, kl], qe_m[:, kl]], axis=0).astype(BF)
        rhs = jnp.concatenate([ki[:, kl], kd_m[:, kl]], axis=0).astype(BF)
        full = _dot_nt(lhs, rhs)
        a_heads.append((jnp.where(intra, full[0:R, 0:R], 0.0) + full[R:, R:]).astype(BF))
        yield
    vb = v_ref[0].astype(BF)
    lane_v = lax.broadcasted_iota(jnp.int32, (R, hv), 1)
    vbd = jnp.concatenate([jnp.where(lane_v // dv == h, vb, jnp.zeros_like(vb)) for h in range(GLA_HEADS)], axis=0)
    o = _dot(jnp.concatenate(a_heads, axis=1), vbd)
    yield
    st = st_sc[...]
    o_ref[0] = (o + _dot_nt(qx, st.astype(BF))).astype(o_ref.dtype)
    upd = _dot(v_ref[0].T.astype(BF), kx)
    on_diag = (lax.broadcasted_iota(jnp.int32, (hv, hk), 0) // dv) == (lax.broadcasted_iota(jnp.int32, (hv, hk), 1) // dk)
    st_sc[...] = st * (d_first * d_second) + jnp.where(on_diag, upd, 0.0)
    yield


def _gla_kernel(fq, fk, fv, fgl, bq, bk, bv, bgl, w2f_ref, gbf_ref, w2b_ref, gbb_ref, of_ref, ob_ref, st_sc):
    @pl.when(pl.program_id(1) == 0)
    def _():
        st_sc[...] = jnp.zeros(st_sc.shape, F32)

    _interleave(_gla_direction(fq, fk, fv, fgl, w2f_ref, gbf_ref, of_ref, st_sc.at[0], reverse=False),
                _gla_direction(bq, bk, bv, bgl, w2b_ref, gbb_ref, ob_ref, st_sc.at[1], reverse=True))


def _gla(q, k, v, gl, w2f, w2b, gbf, gbb):
    b, s, _ = q.shape
    R = GLA_ROWS
    nb = s // R
    hk = GLA_HEADS * GLA_KDIM
    hv = GLA_HEADS * GLA_VDIM
    full = lambda bi, c: (0, 0)

    def views(cidx):
        row = lambda bi, c: (bi, cidx(c), 0)
        return [pl.BlockSpec((1, R, hk), row), pl.BlockSpec((1, R, hk), row), pl.BlockSpec((1, R, hv), row),
                pl.BlockSpec((1, R, LANES), row)]

    return pl.pallas_call(
        _gla_kernel,
        grid=(b, nb),
        in_specs=views(lambda c: c) + views(lambda c: nb - 1 - c) + [
            pl.BlockSpec((3 * LANES, hk), full), pl.BlockSpec((1, hk), full),
            pl.BlockSpec((3 * LANES, hk), full), pl.BlockSpec((1, hk), full)],
        out_specs=[pl.BlockSpec((1, R, hv), lambda bi, c: (bi, c, 0)),
                   pl.BlockSpec((1, R, hv), lambda bi, c: (bi, nb - 1 - c, 0))],
        out_shape=[jax.ShapeDtypeStruct((b, s, hv), BF)] * 2,
        scratch_shapes=[pltpu.VMEM((2, hv, hk), F32)],
        compiler_params=_cparams("parallel", "arbitrary"),
        name="gla",
    )(q, k, v, gl, q, k, v, gl, w2f, gbf, w2b, gbb)


def _sgu_kernel(sg_ref, g_ref, b_ref, ws_ref, bias_ref, o_ref):
    x = sg_ref[...]
    gel = x * (0.5 * (1.0 + jnp.tanh(math.sqrt(2.0 / math.pi) * (x + 0.044715 * (x * x * x)))))
    u = gel[:, :SGU_WIDTH]
    svn = _layernorm(gel[:, SGU_WIDTH:], g_ref[...], b_ref[...]).astype(BF)
    c = SGU_CHUNK
    for ci in range(x.shape[0] // c):
        rows = slice(ci * c, (ci + 1) * c)
        for gi in range(SGU_GROUPS):
            cols = slice(gi * SGU_GROUP_DIM, (gi + 1) * SGU_GROUP_DIM)
            sp = _dot(ws_ref[gi], svn[rows, cols]) + bias_ref[:, cols]
            o_ref[rows, cols] = (u[rows, cols] * sp).astype(o_ref.dtype)


def _sgu(sg, ln_g, ln_b, ws_bf, bias_full):
    t = sg.shape[0]
    c = SGU_CHUNK
    tm = min(ROW_TILE, t)
    row = lambda i: (i, 0)
    full = lambda i: (0, 0)
    return pl.pallas_call(
        _sgu_kernel,
        grid=(t // tm,),
        in_specs=[pl.BlockSpec((tm, 2 * SGU_WIDTH), row),
                  pl.BlockSpec((1, SGU_WIDTH), full), pl.BlockSpec((1, SGU_WIDTH), full),
                  pl.BlockSpec((SGU_GROUPS, c, c), lambda i: (0, 0, 0)),
                  pl.BlockSpec((c, SGU_WIDTH), full)],
        out_specs=pl.BlockSpec((tm, SGU_WIDTH), row),
        out_shape=jax.ShapeDtypeStruct((t, SGU_WIDTH), BF),
        compiler_params=_cparams("parallel"),
        name="sgu",
    )(sg, ln_g, ln_b, ws_bf, bias_full)


def _router_kernel(x_ref, wr_ref, route_ref, cnt_ref, base_sc):
    @pl.when(pl.program_id(0) == 0)
    def _():
        base_sc[...] = jnp.zeros(base_sc.shape, F32)

    tr = x_ref.shape[0]
    x2 = _split_bf16(x_ref[...], 2)
    logits = _dot(jnp.concatenate([x2, x2[:, :x_ref.shape[1]]], axis=1), wr_ref[...])
    lane = lax.broadcasted_iota(jnp.int32, (tr, LANES), 1).astype(F32)
    lg = jnp.where(lane < N_EXPERTS, logits, -jnp.inf)
    m1 = jnp.max(lg, axis=1, keepdims=True)
    i1 = jnp.min(jnp.where(lg == m1, lane, float(LANES)), axis=1, keepdims=True)
    lg2 = jnp.where(lane == i1, -jnp.inf, lg)
    m2 = jnp.max(lg2, axis=1, keepdims=True)
    i2 = jnp.min(jnp.where(lg2 == m2, lane, float(LANES)), axis=1, keepdims=True)
    e = jnp.exp(m2 - m1)
    g1 = 1.0 / (1.0 + e)
    g2 = e / (1.0 + e)
    oh1 = (lane == i1).astype(F32)
    oh2 = (lane == i2).astype(F32)
    oh = oh1 + oh2
    row_i = lax.broadcasted_iota(jnp.int32, (tr, tr), 0)
    col_i = lax.broadcasted_iota(jnp.int32, (tr, tr), 1)
    before = _dot((col_i < row_i).astype(BF), oh.astype(BF)) + base_sc[...]
    r1 = jnp.sum(oh1 * before, axis=1, keepdims=True)
    r2 = jnp.sum(oh2 * before, axis=1, keepdims=True)
    base_sc[...] += jnp.sum(oh, axis=0, keepdims=True)
    route = jnp.zeros((tr, LANES), F32)
    for idx, val in enumerate((i1, i2, r1, r2, g1, g2)):
        route = jnp.where(lane == float(idx), val, route)
    route_ref[...] = route
    cnt_ref[...] = base_sc[...]


def _router(x, wr_pad):
    t, d = x.shape
    tr = min(ROUTE_TILE, t)
    return pl.pallas_call(
        _router_kernel,
        grid=(t // tr,),
        in_specs=[pl.BlockSpec((tr, d), lambda i: (i, 0)), pl.BlockSpec((3 * d, LANES), lambda i: (0, 0))],
        out_specs=[pl.BlockSpec((tr, LANES), lambda i: (i, 0)), pl.BlockSpec((1, LANES), lambda i: (0, 0))],
        out_shape=[jax.ShapeDtypeStruct((t, LANES), F32), jax.ShapeDtypeStruct((1, LANES), F32)],
        scratch_shapes=[pltpu.VMEM((1, LANES), F32)],
        compiler_params=_cparams("arbitrary"),
        name="router",
    )(x, wr_pad)


def _piece_indices(pos, n_rows):
    base = jnp.arange(SC_ROW_SPLIT, dtype=jnp.int32)[:, None, None] * n_rows
    return (base + pos.T[None]).reshape(1, -1)


def _sc_scatter_rows(xp, idx, n_out):
    ns, t, w = xp.shape
    nblk = t // SC_WINDOW
    per_piece = idx.shape[1] // ns // SC_WINDOW
    mesh = plsc.VectorSubcoreMesh(core_axis_name="c", subcore_axis_name="s")

    @pl.kernel(out_type=jax.ShapeDtypeStruct((ns * n_out, w), xp.dtype), mesh=mesh)
    def k(x_hbm, i_hbm, o_hbm):
        def body(x_vmem, i_vmem):
            pltpu.sync_copy(x_vmem, o_hbm.at[i_vmem.at[0]])

        pltpu.emit_pipeline(
            body,
            grid=(idx.shape[1] // SC_WINDOW,),
            in_specs=[pl.BlockSpec((SC_WINDOW, w), index_map=lambda i: ((i // per_piece) * nblk + i % nblk, 0)),
                      pl.BlockSpec((1, SC_WINDOW), index_map=lambda i: (0, i))],
            out_specs=[],
            core_axis_name=("c", "s"),
            dimension_semantics=(pltpu.PARALLEL,),
        )(x_hbm, i_hbm)

    return k(xp.reshape(ns * t, w), idx).reshape(ns, n_out, w)


def _sc_gather_rows(yp, idx):
    ns, n, w = yp.shape
    mesh = plsc.VectorSubcoreMesh(core_axis_name="c", subcore_axis_name="s")

    @pl.kernel(out_type=jax.ShapeDtypeStruct((idx.shape[1], w), yp.dtype), mesh=mesh)
    def k(x_hbm, i_hbm, o_hbm):
        def body(i_vmem, o_vmem):
            pltpu.sync_copy(x_hbm.at[i_vmem.at[0]], o_vmem)

        pltpu.emit_pipeline(
            body,
            grid=(idx.shape[1] // SC_WINDOW,),
            in_specs=[pl.BlockSpec((1, SC_WINDOW), index_map=lambda i: (0, i))],
            out_specs=[pl.BlockSpec((SC_WINDOW, w), index_map=lambda i: (i, 0))],
            core_axis_name=("c", "s"),
            dimension_semantics=(pltpu.PARALLEL,),
        )(i_hbm, o_hbm)

    return k(yp.reshape(ns * n, w), idx)


def _combine_ln_kernel(x_ref, y_ref, route_ref, g_ref, beta_ref, o_ref):
    g1 = route_ref[:, 4:5]
    g2 = route_ref[:, 5:6]
    y = (g1 * jnp.concatenate([y_ref[j, 0] for j in range(SC_ROW_SPLIT)], axis=1)
         + g2 * jnp.concatenate([y_ref[j, 1] for j in range(SC_ROW_SPLIT)], axis=1))
    o_ref[...] = _layernorm(DN_ALPHA * x_ref[...] + y, g_ref[...], beta_ref[...])


def _combine_ln(x, y2, route, g, beta):
    t, d = x.shape
    tm = min(ROW_TILE, t)
    row = lambda i: (i, 0)
    full = lambda i: (0, 0)
    return pl.pallas_call(
        _combine_ln_kernel,
        grid=(t // tm,),
        in_specs=[pl.BlockSpec((tm, d), row),
                  pl.BlockSpec((SC_ROW_SPLIT, 2, tm, d // SC_ROW_SPLIT), lambda i: (0, 0, i, 0)),
                  pl.BlockSpec((tm, LANES), row), pl.BlockSpec((1, d), full), pl.BlockSpec((1, d), full)],
        out_specs=pl.BlockSpec((tm, d), row),
        out_shape=jax.ShapeDtypeStruct((t, d), F32),
        compiler_params=_cparams("parallel"),
        name="moe_combine_ln",
    )(x, y2, route, g, beta)


def _pad_cols(w, n):
    return jnp.pad(w, ((0, 0), (0, n - w.shape[1])))


def _prep_even(p):
    (w_in, q_norm, w_uq, kv_norm, w_ukv, conv_w, conv_b, dt_bias, a_log, d_skip, ssm_norm, w_out,
     ln1_g, ln1_b, w_gate, w_up, w_down, ln2_g, ln2_b) = p
    o = 0
    cq = w_in[:, o:o + Q_RANK]; o += Q_RANK
    ckv = w_in[:, o:o + KV_RANK]; o += KV_RANK
    kr = w_in[:, o:o + QK_ROPE]; o += QK_ROPE
    z = w_in[:, o:o + SSD_INNER]; o += SSD_INNER
    xbc = w_in[:, o:o + CONV_DIM]; o += CONV_DIM
    dt = w_in[:, o:]
    half = QK_ROPE // 2
    zeros = lambda n: jnp.zeros((w_in.shape[0], n), F32)
    kra = jnp.concatenate([zeros(QK_NOPE), kr, zeros(LANES - QK_NOPE - QK_ROPE)], axis=1)
    krb = jnp.concatenate([zeros(QK_NOPE), -kr[:, half:], kr[:, :half], zeros(LANES - QK_NOPE - QK_ROPE)], axis=1)
    w_in_p = jnp.concatenate([cq, ckv, kra, krb, z, xbc, _pad_cols(dt, LANES)], axis=1).astype(BF)

    wq = w_uq.reshape(Q_RANK, MLA_HEADS, QK_NOPE + QK_ROPE)
    nope, rope = wq[..., :QK_NOPE], wq[..., QK_NOPE:]
    zq = lambda n: jnp.zeros((Q_RANK, MLA_HEADS, n), F32)
    wqa = jnp.concatenate([nope, rope, zq(LANES - QK_NOPE - QK_ROPE)], axis=-1)
    wqb = jnp.concatenate([zq(QK_NOPE), -rope[..., half:], rope[..., :half], zq(LANES - QK_NOPE - QK_ROPE)], axis=-1)
    wkv = w_ukv.reshape(KV_RANK, MLA_HEADS, QK_NOPE + V_DIM)
    zk = jnp.zeros((KV_RANK, MLA_HEADS, LANES - QK_NOPE), F32)
    wk = jnp.concatenate([wkv[..., :QK_NOPE], zk], axis=-1)
    vv_t = jnp.transpose(wkv[..., QK_NOPE:], (1, 2, 0))
    wvt = jnp.concatenate([vv_t, jnp.zeros((MLA_HEADS, V_ROWS - V_DIM, KV_RANK), F32)], axis=1)
    vadd = jnp.tile((jnp.arange(V_ROWS) == V_DIM).astype(F32), MLA_HEADS)[:, None]
    hw = MLA_HEADS * LANES
    return dict(
        w_in=w_in_p, q_norm=q_norm[None], kv_norm=kv_norm[None],
        wqa=wqa.reshape(Q_RANK, hw).astype(BF), wqb=wqb.reshape(Q_RANK, hw).astype(BF),
        wk=wk.reshape(KV_RANK, hw).astype(BF), wvt=wvt.reshape(MLA_HEADS * V_ROWS, KV_RANK).astype(BF), vadd=vadd,
        conv_w=conv_w, conv_b=conv_b[None],
        dt_bias=_pad_cols(dt_bias.reshape(1, -1), LANES), a_log=_pad_cols(a_log.reshape(1, -1), LANES),
        d_skip=jnp.repeat(d_skip, SSD_HEAD_DIM)[None], ssm_norm=ssm_norm[None],
        wo_a=w_out[:MLA_HEADS * V_DIM].astype(BF), wo_b=w_out[MLA_HEADS * V_DIM:].astype(BF),
        ln1_g=ln1_g[None], ln1_b=ln1_b[None],
        wg=w_gate.astype(BF), wu=w_up.astype(BF), wd=w_down.astype(BF),
        ln2_g=ln2_g[None], ln2_b=ln2_b[None])


def _prep_odd(p):
    (w_in, gate_w2, gate_b, gla_norm, sgu_ln_g, sgu_ln_b, w_s, b_s, w_out, ln1_g, ln1_b,
     w_router, we_gate, we_up, we_down, ln2_g, ln2_b) = p
    hk = GLA_HEADS * GLA_KDIM
    hv = GLA_HEADS * GLA_VDIM
    o = 2 * hk + 2 * hv
    gl = w_in[:, o:o + 2 * GLA_GATE_RANK]
    w_in_p = jnp.concatenate([w_in[:, :o], _pad_cols(gl, LANES), w_in[:, o + 2 * GLA_GATE_RANK:]], axis=1).astype(BF)
    zr = lambda n: jnp.zeros((n, hk), F32)
    def pieces(w):
        hi = w.astype(BF)
        lo = (w - hi.astype(F32)).astype(BF)
        return jnp.concatenate([hi, hi, lo], axis=0)

    w2f = pieces(jnp.concatenate([gate_w2[0], zr(LANES - GLA_GATE_RANK)], axis=0))
    w2b = pieces(jnp.concatenate([zr(GLA_GATE_RANK), gate_w2[1], zr(LANES - 2 * GLA_GATE_RANK)], axis=0))
    bias_full = jnp.repeat(b_s.T, SGU_GROUP_DIM, axis=1)
    return dict(
        w_in=w_in_p, w2f=w2f, w2b=w2b, gbf=gate_b[0][None], gbb=gate_b[1][None], gla_norm=gla_norm[None],
        sgu_g=sgu_ln_g[None], sgu_b=sgu_ln_b[None], ws=w_s.astype(BF), sgu_bias=bias_full,
        wo_a=w_out[:hv].astype(BF), wo_b=w_out[hv:].astype(BF), ln1_g=ln1_g[None], ln1_b=ln1_b[None],
        w_router=pieces(_pad_cols(w_router, LANES)),
        wg=we_gate.astype(BF), wu=we_up.astype(BF), wd=we_down.astype(BF),
        ln2_g=ln2_g[None], ln2_b=ln2_b[None])


def _rope_tables(s):
    half = QK_ROPE // 2
    inv = jnp.exp(-math.log(ROPE_THETA) * jnp.arange(half, dtype=F32) / half)
    ang = jnp.arange(s, dtype=F32)[:, None] * inv[None, :]
    cos, sin = jnp.cos(ang), jnp.sin(ang)
    pad = LANES - QK_NOPE - QK_ROPE
    cos_t = jnp.concatenate([jnp.ones((s, QK_NOPE), F32), cos, cos, jnp.ones((s, pad), F32)], axis=1)
    sin_t = jnp.concatenate([jnp.zeros((s, QK_NOPE), F32), sin, sin, jnp.zeros((s, pad), F32)], axis=1)
    return cos_t, sin_t


def _even_layer(x, w, b, s):
    t = b * s
    mla_in, z, xbc, dt = _proj(x, w["w_in"], (Q_RANK + KV_RANK + 2 * LANES, SSD_INNER, CONV_DIM, LANES),
                               (F32, BF, F32, F32))
    cos_t, sin_t = _rope_tables(s)
    q, k, vt = _mla_prep(mla_in, cos_t, sin_t, w["q_norm"], w["kv_norm"], w["wqa"], w["wqb"], w["wk"], w["wvt"],
                         w["vadd"], b, s)
    hw = MLA_HEADS * LANES
    o_attn = _flash(q.reshape(b, s, hw), k.reshape(b, s, hw), vt)
    y_f, y_b = _ssd(xbc.reshape(b, s, CONV_DIM), dt.reshape(b, s, LANES),
                    w["conv_w"], w["conv_b"], w["dt_bias"], w["a_log"], w["d_skip"])
    mix = (o_attn.reshape(t, -1), y_f.reshape(t, -1), y_b.reshape(t, -1), z, w["ssm_norm"])
    x1 = _outproj_ln(mix, x, w["wo_a"], w["wo_b"], w["ln1_g"], w["ln1_b"], even=True)
    return _ffn_ln(x1, w["wg"], w["wu"], w["wd"], w["ln2_g"], w["ln2_b"])


def _moe(x1, x1p, w):
    t, d = x1.shape
    route, cnt = _router(x1, w["w_router"])
    eid = route[:, 0:2].astype(jnp.int32)
    rank = route[:, 2:4].astype(jnp.int32)
    counts = cnt[0, :N_EXPERTS].astype(jnp.int32)
    tm = MOE_TILE
    padded = ((counts + tm - 1) // tm) * tm
    ends = jnp.cumsum(padded)
    offs = ends - padded
    pos = offs[eid] + rank
    p_rows = 2 * t + N_EXPERTS * tm
    tiles = jnp.arange(p_rows // tm, dtype=jnp.int32)
    tile_ends = ends // tm
    tile_expert = jnp.minimum(jnp.sum(tiles[:, None] >= tile_ends[None, :], axis=1), N_EXPERTS - 1).astype(jnp.int32)
    tile_valid = (tiles < tile_ends[-1]).astype(jnp.int32)
    idx = _piece_indices(pos, p_rows)
    xs = _sc_scatter_rows(x1p, idx, p_rows)
    ys = _moe_ffn(xs, tile_expert, tile_valid, w["wg"], w["wu"], w["wd"])
    y2 = _sc_gather_rows(ys, idx).reshape(SC_ROW_SPLIT, 2, t, d // SC_ROW_SPLIT)
    return _combine_ln(x1, y2, route, w["ln2_g"], w["ln2_b"])


def _odd_layer(x, w, b, s):
    t = b * s
    hk = GLA_HEADS * GLA_KDIM
    hv = GLA_HEADS * GLA_VDIM
    q, k, v, r, gl, sg = _proj(x, w["w_in"], (hk, hk, hv, hv, LANES, 2 * SGU_WIDTH), (F32, F32, F32, BF, F32, F32))
    o_f, o_b = _gla(q.reshape(b, s, hk), k.reshape(b, s, hk), v.reshape(b, s, hv), gl.reshape(b, s, LANES),
                    w["w2f"], w["w2b"], w["gbf"], w["gbb"])
    o_sgu = _sgu(sg, w["sgu_g"], w["sgu_b"], w["ws"], w["sgu_bias"])
    mix = (o_f.reshape(t, hv), o_b.reshape(t, hv), r, w["gla_norm"], o_sgu)
    x1, x1p = _outproj_ln(mix, x, w["wo_a"], w["wo_b"], w["ln1_g"], w["ln1_b"], even=False)
    return _moe(x1, x1p, w)


def _trunk(x, ev_w, od_w):
    b, s, d = x.shape
    x = x.reshape(b * s, d)
    for i in range(DEPTH):
        if i % 2 == 0:
            x = _even_layer(x, ev_w[i // 2], b, s)
        else:
            x = _odd_layer(x, od_w[i // 2], b, s)
    return x.reshape(b, s, d)


def kernel(x_prompt, x_sample, ev_w_in, ev_q_norm, ev_w_uq, ev_kv_norm, ev_w_ukv, ev_conv_w, ev_conv_b, ev_dt_bias, ev_a_log, ev_d_skip, ev_ssm_norm, ev_w_out, ev_ln1_g, ev_ln1_b, ev_w_gate, ev_w_up, ev_w_down, ev_ln2_g, ev_ln2_b, od_w_in, od_gate_w2, od_gate_b, od_gla_norm, od_sgu_ln_g, od_sgu_ln_b, od_w_s, od_b_s, od_w_out, od_ln1_g, od_ln1_b, od_w_router, od_we_gate, od_we_up, od_we_down, od_ln2_g, od_ln2_b):
    ev = (ev_w_in, ev_q_norm, ev_w_uq, ev_kv_norm, ev_w_ukv, ev_conv_w, ev_conv_b, ev_dt_bias,
          ev_a_log, ev_d_skip, ev_ssm_norm, ev_w_out, ev_ln1_g, ev_ln1_b, ev_w_gate, ev_w_up,
          ev_w_down, ev_ln2_g, ev_ln2_b)
    od = (od_w_in, od_gate_w2, od_gate_b, od_gla_norm, od_sgu_ln_g, od_sgu_ln_b, od_w_s, od_b_s,
          od_w_out, od_ln1_g, od_ln1_b, od_w_router, od_we_gate, od_we_up, od_we_down,
          od_ln2_g, od_ln2_b)
    ev_w = [_prep_even(tuple(t[i] for t in ev)) for i in range(ev_w_in.shape[0])]
    od_w = [_prep_odd(tuple(t[i] for t in od)) for i in range(od_w_in.shape[0])]
    return (_trunk(x_prompt, ev_w, od_w), _trunk(x_sample, ev_w, od_w))
```

```python
import functools
import math

import jax
import jax.numpy as jnp
from jax import lax
from jax.experimental import pallas as pl
from jax.experimental.pallas import tpu as pltpu
from jax.experimental.pallas import tpu_sc as plsc

BF = jnp.bfloat16
F32 = jnp.float32

D_MODEL = 1024
DEPTH = 4
MLA_HEADS = 8
QK_NOPE = 64
QK_ROPE = 32
V_DIM = 64
Q_RANK = 256
KV_RANK = 128
ROPE_THETA = 10000.0
SSD_HEADS = 8
SSD_HEAD_DIM = 64
SSD_GROUPS = 2
SSD_STATE = 64
SSD_CHUNK = 128
CONV_K = 5
SSD_INNER = SSD_HEADS * SSD_HEAD_DIM
SSD_HPG = SSD_HEADS // SSD_GROUPS
CONV_DIM = SSD_INNER + 2 * SSD_GROUPS * SSD_STATE
GLA_HEADS = 4
GLA_KDIM = 64
GLA_VDIM = 128
GLA_GATE_RANK = 16
GLA_TAU = 16.0
GLA_CHUNK = 64
SGU_GROUPS = 4
SGU_CHUNK = 128
SGU_GROUP_DIM = 128
SGU_WIDTH = SGU_GROUPS * SGU_GROUP_DIM
D_FF = 2816
N_EXPERTS = 8
D_FF_EXPERT = 3584
DN_ALPHA = (2 * DEPTH) ** 0.25
EPS = 1e-5

LANES = 128
SUBLANES = 8
VMEM_LIMIT = 56 * 1024 * 1024
SC_WINDOW = 128
SC_ROW_SPLIT = 4

ROW_TILE = 512
FF_CHUNK = 256
ATT_TQ = 512
ATT_TK = 512
MOE_TILE = 512
ROUTE_TILE = 512
GLA_ROWS = 2 * GLA_CHUNK
V_ROWS = 80


def _cparams(*sem):
    return pltpu.CompilerParams(dimension_semantics=sem, vmem_limit_bytes=VMEM_LIMIT)


def _resident(shape, index_map):
    return pl.BlockSpec(shape, index_map, pipeline_mode=pl.Buffered(1))


def _rms(x):
    return x * lax.rsqrt(jnp.mean(x * x, axis=-1, keepdims=True) + EPS)


def _layernorm(x, g, b):
    mu = jnp.mean(x, axis=-1, keepdims=True)
    xc = x - mu
    var = jnp.mean(xc * xc, axis=-1, keepdims=True)
    return xc * lax.rsqrt(var + EPS) * g + b


def _silu(x):
    return x * jax.nn.sigmoid(x)


def _softplus(x):
    return jnp.maximum(x, 0.0) + jnp.log1p(jnp.exp(-jnp.abs(x)))


def _dot(a, b):
    return jnp.dot(a, b, preferred_element_type=F32)


def _dot_nt(a, b):
    return lax.dot_general(a, b, (((1,), (1,)), ((), ())), preferred_element_type=F32)


def _project_into(xb, w_ref, off, o_refs):
    for o_ref in o_refs:
        n = o_ref.shape[1]
        o_ref[...] = _dot(xb, w_ref[:, off:off + n]).astype(o_ref.dtype)
        off += n
    return off


MLA_IN = Q_RANK + KV_RANK + 2 * LANES


def _proj_even_kernel(x_ref, w_ref, cos_ref, sin_ref, qn_ref, kvn_ref, wqa_ref, wqb_ref, wk_ref, wvt_ref, vadd_ref,
                      q_ref, k_ref, vt_ref, z_ref, xbc_ref, dt_ref):
    xb = x_ref[...].astype(BF)
    _project_into(xb, w_ref, MLA_IN, (z_ref, xbc_ref, dt_ref))
    m = _dot(xb, w_ref[:, 0:MLA_IN])
    cq = m[:, 0:Q_RANK]
    ckv = m[:, Q_RANK:Q_RANK + KV_RANK]
    kra = m[:, Q_RANK + KV_RANK:Q_RANK + KV_RANK + LANES]
    krb = m[:, Q_RANK + KV_RANK + LANES:Q_RANK + KV_RANK + 2 * LANES]
    cos = cos_ref[...]
    sin = sin_ref[...]
    cos8 = jnp.concatenate([cos] * MLA_HEADS, axis=1)
    sin8 = jnp.concatenate([sin] * MLA_HEADS, axis=1)
    cqn = (_rms(cq) * qn_ref[...]).astype(BF)
    q = _dot(cqn, wqa_ref[...]) * cos8 + _dot(cqn, wqb_ref[...]) * sin8
    q_ref[...] = (q * ((QK_NOPE + QK_ROPE) ** -0.5 * math.log2(math.e))).astype(BF)
    ckvn = (_rms(ckv) * kvn_ref[...]).astype(BF)
    kr = kra * cos + krb * sin
    k = _dot(ckvn, wk_ref[...]) + jnp.concatenate([kr] * MLA_HEADS, axis=1)
    k_ref[...] = k.astype(BF)
    vt_ref[0] = (_dot_nt(wvt_ref[...], ckvn) + vadd_ref[...]).astype(BF)


def _proj_even(x, w_in, cos_t, sin_t, q_norm, kv_norm, wqa, wqb, wk, wvt, vadd, batch, seq):
    t, d = x.shape
    tm = min(ROW_TILE, seq)
    nseq = seq // tm
    hw = MLA_HEADS * LANES
    vr = MLA_HEADS * V_ROWS
    full = lambda i: (0, 0)
    row = lambda i: (i, 0)
    return pl.pallas_call(
        _proj_even_kernel,
        grid=(t // tm,),
        in_specs=[pl.BlockSpec((tm, d), row), _resident(w_in.shape, full),
                  pl.BlockSpec((tm, LANES), lambda i: (i % nseq, 0)),
                  pl.BlockSpec((tm, LANES), lambda i: (i % nseq, 0)),
                  pl.BlockSpec((1, Q_RANK), full), pl.BlockSpec((1, KV_RANK), full),
                  pl.BlockSpec((Q_RANK, hw), full), pl.BlockSpec((Q_RANK, hw), full),
                  pl.BlockSpec((KV_RANK, hw), full), pl.BlockSpec((vr, KV_RANK), full),
                  pl.BlockSpec((vr, 1), full)],
        out_specs=[pl.BlockSpec((tm, hw), row), pl.BlockSpec((tm, hw), row),
                   pl.BlockSpec((1, vr, tm), lambda i: (i // nseq, 0, i % nseq)),
                   pl.BlockSpec((tm, SSD_INNER), row), pl.BlockSpec((tm, CONV_DIM), row), pl.BlockSpec((tm, LANES), row)],
        out_shape=[jax.ShapeDtypeStruct((t, hw), BF), jax.ShapeDtypeStruct((t, hw), BF),
                   jax.ShapeDtypeStruct((batch, vr, seq), BF),
                   jax.ShapeDtypeStruct((t, SSD_INNER), BF), jax.ShapeDtypeStruct((t, CONV_DIM), F32),
                   jax.ShapeDtypeStruct((t, LANES), F32)],
        compiler_params=_cparams("parallel"),
        name="proj_even",
    )(x, w_in, cos_t, sin_t, q_norm, kv_norm, wqa, wqb, wk, wvt, vadd)


def _flash_kernel(q_ref, k_ref, vt_ref, o_ref, m_sc, acc_sc, s_sc, *, tk, nk, unroll):
    for h in range(2):
        m_sc[h] = jnp.full(m_sc.shape[1:], -jnp.inf, F32)
        acc_sc[h] = jnp.zeros(acc_sc.shape[1:], F32)

    def scores(j, slot):
        off = pl.multiple_of(j * tk, tk)
        for h in range(2):
            lanes = slice(h * LANES, (h + 1) * LANES)
            s_sc[slot, h] = _dot_nt(k_ref[0, pl.ds(off, tk), lanes], q_ref[0, :, lanes])

    def consume(j, slot):
        off = pl.multiple_of(j * tk, tk)
        for h in range(2):
            st = s_sc[slot, h]
            m_prev = m_sc[h]
            m_new = jnp.maximum(m_prev, jnp.max(st, axis=0, keepdims=True))
            p = jnp.exp2(st - m_new[0:1, :]).astype(BF)
            alpha = jnp.exp2(m_prev - m_new)
            pv = _dot(vt_ref[0, h * V_ROWS:(h + 1) * V_ROWS, pl.ds(off, tk)], p)
            acc_sc[h] = alpha[0:1, :] * acc_sc[h] + pv
            m_sc[h] = m_new

    scores(0, 0)

    def body(jj, carry):
        j = unroll * jj
        for u in range(unroll):
            scores(jnp.minimum(j + u + 1, nk - 1), (u + 1) % 2)
            consume(j + u, u % 2)
        return carry

    lax.fori_loop(0, nk // unroll, body, 0)
    outs = []
    for h in range(2):
        acc = acc_sc[h]
        outs.append(acc[0:V_DIM, :] / acc[V_DIM:V_DIM + 1, :])
    o_ref[0] = jnp.concatenate(outs, axis=0).T.astype(o_ref.dtype)


def _flash(q, k, vt):
    b, s, hw = q.shape
    tq = min(ATT_TQ, s)
    tk = min(ATT_TK, s)
    nk = s // tk
    unroll = next(u for u in (8, 4, 2, 1) if nk % u == 0)
    pairs = MLA_HEADS // 2
    return pl.pallas_call(
        functools.partial(_flash_kernel, tk=tk, nk=nk, unroll=unroll),
        grid=(b, pairs, s // tq),
        in_specs=[pl.BlockSpec((1, tq, 2 * LANES), lambda bi, hp, i: (bi, i, hp)),
                  pl.BlockSpec((1, s, 2 * LANES), lambda bi, hp, i: (bi, 0, hp)),
                  pl.BlockSpec((1, 2 * V_ROWS, s), lambda bi, hp, i: (bi, hp, 0))],
        out_specs=pl.BlockSpec((1, tq, LANES), lambda bi, hp, i: (bi, i, hp)),
        out_shape=jax.ShapeDtypeStruct((b, s, MLA_HEADS * V_DIM), BF),
        scratch_shapes=[pltpu.VMEM((2, SUBLANES, tq), F32), pltpu.VMEM((2, V_ROWS, tq), F32),
                        pltpu.VMEM((2, 2, tk, tq), F32)],
        compiler_params=_cparams("parallel", "parallel", "arbitrary"),
        name="mla_flash",
    )(q, k, vt)


def _split_bf16(x, pieces):
    out = []
    for _ in range(pieces):
        p = x.astype(BF)
        out.append(p)
        x = x - p.astype(F32)
    return jnp.concatenate(out, axis=1)


def _spread_matrix(first_lane, heads, width, pieces):
    src = jnp.arange(LANES)[:, None] - first_lane
    dst = jnp.arange(heads * width)[None, :] // width
    return jnp.tile((src == dst).astype(BF), (pieces, 1))


def _ssd_direction(xc_ref, xp_ref, xn_ref, dt_ref, ep_ref, el_ref, cw_ref, cb_ref, dtb_ref, alog_ref, dskip_ref,
                   o_ref, xe_sc, st_sc, *, reverse, cc, nc):
    L = SSD_CHUNK
    N = SSD_STATE
    P = SSD_HEAD_DIM
    xe_sc[0:SUBLANES, :] = jnp.where(cc > 0, xp_ref[0], 0.0)
    xe_sc[SUBLANES:SUBLANES + L, :] = xc_ref[0]
    xe_sc[SUBLANES + L:2 * SUBLANES + L, :] = jnp.where(cc < nc - 1, xn_ref[0], 0.0)
    conv = cb_ref[...] + cw_ref[0:1, :] * xe_sc[pl.ds(SUBLANES - CONV_K // 2, L), :]
    for j in range(1, CONV_K):
        conv = conv + cw_ref[j:j + 1, :] * xe_sc[pl.ds(SUBLANES - CONV_K // 2 + j, L), :]
    yield
    xbc = _silu(conv)
    xs = xbc[:, :SSD_INNER]
    bc = xbc[:, SSD_INNER:]
    bc_t = bc.T

    lane = lax.broadcasted_iota(jnp.int32, (L, LANES), 1)
    dtv = _softplus(dt_ref[0] + dtb_ref[...])
    a = jnp.where(lane[0:1] < 2 * SSD_HEADS, -jnp.exp(alog_ref[...]), 0.0)
    dta = dtv * a
    row_i = lax.broadcasted_iota(jnp.int32, (L, L), 0)
    col_i = lax.broadcasted_iota(jnp.int32, (L, L), 1)
    causal = (col_i >= row_i) if reverse else (col_i <= row_i)
    acs3 = _dot(causal.astype(BF), _split_bf16(dta, 3))
    yield
    acs = acs3[:, 0:LANES] + acs3[:, LANES:2 * LANES] + acs3[:, 2 * LANES:]
    acs_t = acs.T
    d0 = SSD_HEADS if reverse else 0
    end = 0 if reverse else L - 1
    tot = acs[end:end + 1, :]
    yield

    stacked = jnp.concatenate([dtv, jnp.exp(acs), jnp.exp(tot - acs),
                               jnp.broadcast_to(jnp.exp(tot), (2 * SUBLANES, LANES))], axis=0)
    spread = _dot(_split_bf16(stacked, 2), ep_ref[...])
    dt_x = spread[0:L]
    ea_x = spread[L:2 * L]
    eb_x = spread[2 * L:3 * L]
    et_x = spread[3 * L:3 * L + 1]
    col_x = _dot(_split_bf16(acs, 3), el_ref[...])
    yield
    row_x = jnp.concatenate([jnp.broadcast_to(acs_t[d0 + h:d0 + h + 1, :], (L, L)) for h in range(SSD_HEADS)], axis=1)
    causal_x = jnp.concatenate([causal] * SSD_HEADS, axis=1)
    decay_x = jnp.exp(jnp.where(causal_x, col_x - row_x, -jnp.inf))
    xdt = xs * dt_x
    xdt_b = xdt.astype(BF)
    xw_b = (xdt * eb_x).astype(BF)
    gw = SSD_HPG * P
    lane_g = lax.broadcasted_iota(jnp.int32, (L, gw), 1)
    yield

    ys = []
    for g in range(SSD_GROUPS):
        bm_g = bc[:, g * N:(g + 1) * N].astype(BF)
        cm_g = bc[:, (SSD_GROUPS + g) * N:(SSD_GROUPS + g + 1) * N].astype(BF)
        bm_t_g = bc_t[g * N:(g + 1) * N, :].astype(BF)
        cb = _dot_nt(cm_g, bm_g)
        m_g = (jnp.concatenate([cb] * SSD_HPG, axis=1) * decay_x[:, g * SSD_HPG * L:(g + 1) * SSD_HPG * L]).astype(BF)
        xg = xdt_b[:, g * gw:(g + 1) * gw]
        xbd = jnp.concatenate([jnp.where(lane_g // P == j, xg, jnp.zeros_like(xg)) for j in range(SSD_HPG)], axis=0)
        y_diag = _dot(m_g, xbd)
        states_t = _dot(bm_t_g, xw_b[:, g * gw:(g + 1) * gw])
        prev_t = st_sc[g]
        y_off = _dot(cm_g, prev_t.astype(BF)) * ea_x[:, g * gw:(g + 1) * gw]
        st_sc[g] = prev_t * et_x[:, g * gw:(g + 1) * gw] + states_t
        ys.append(y_diag + y_off)
        yield
    y = jnp.concatenate(ys, axis=1)
    if not reverse:
        y = y + dskip_ref[...] * xs
    o_ref[0] = y.astype(o_ref.dtype)


def _interleave(*tracers):
    live = list(tracers)
    while live:
        for g in list(live):
            try:
                next(g)
            except StopIteration:
                live.remove(g)


def _ssd_kernel(fxc, fxp, fxn, fdt, bxc, bxp, bxn, bdt, epf_ref, elf_ref, epb_ref, elb_ref,
                cw_ref, cb_ref, dtb_ref, alog_ref, dskip_ref, yf_ref, yb_ref, xe_sc, st_sc, *, nc):
    c = pl.program_id(1)

    @pl.when(c == 0)
    def _():
        st_sc[...] = jnp.zeros(st_sc.shape, F32)

    shared = (cw_ref, cb_ref, dtb_ref, alog_ref, dskip_ref)
    _interleave(
        _ssd_direction(fxc, fxp, fxn, fdt, epf_ref, elf_ref, *shared, yf_ref, xe_sc.at[0], st_sc.at[0],
                       reverse=False, cc=c, nc=nc),
        _ssd_direction(bxc, bxp, bxn, bdt, epb_ref, elb_ref, *shared, yb_ref, xe_sc.at[1], st_sc.at[1],
                       reverse=True, cc=nc - 1 - c, nc=nc))


def _ssd(xbc, dt, conv_w, conv_b, dt_bias, a_log, d_skip):
    b, s, _ = xbc.shape
    L = SSD_CHUNK
    nc = s // L
    hb = L // SUBLANES
    nhb = s // SUBLANES
    full = lambda bi, c: (0, 0)

    def views(cidx):
        row = lambda bi, c: (bi, cidx(c), 0)
        return [pl.BlockSpec((1, L, CONV_DIM), row),
                pl.BlockSpec((1, SUBLANES, CONV_DIM), lambda bi, c: (bi, jnp.maximum(cidx(c) * hb - 1, 0), 0)),
                pl.BlockSpec((1, SUBLANES, CONV_DIM), lambda bi, c: (bi, jnp.minimum((cidx(c) + 1) * hb, nhb - 1), 0)),
                pl.BlockSpec((1, L, LANES), row)]

    spreads = [_spread_matrix(d0, SSD_HEADS, width, pieces)
               for d0 in (0, SSD_HEADS) for width, pieces in ((SSD_HEAD_DIM, 2), (L, 3))]
    return pl.pallas_call(
        functools.partial(_ssd_kernel, nc=nc),
        grid=(b, nc),
        in_specs=views(lambda c: c) + views(lambda c: nc - 1 - c) + [pl.BlockSpec(m.shape, full) for m in spreads] + [
            pl.BlockSpec((CONV_K, CONV_DIM), full), pl.BlockSpec((1, CONV_DIM), full),
            pl.BlockSpec((1, LANES), full), pl.BlockSpec((1, LANES), full), pl.BlockSpec((1, SSD_INNER), full)],
        out_specs=[pl.BlockSpec((1, L, SSD_INNER), lambda bi, c: (bi, c, 0)),
                   pl.BlockSpec((1, L, SSD_INNER), lambda bi, c: (bi, nc - 1 - c, 0))],
        out_shape=[jax.ShapeDtypeStruct((b, s, SSD_INNER), BF)] * 2,
        scratch_shapes=[pltpu.VMEM((2, L + 2 * SUBLANES, CONV_DIM), F32),
                        pltpu.VMEM((2, SSD_GROUPS, SSD_STATE, SSD_HPG * SSD_HEAD_DIM), F32)],
        compiler_params=_cparams("parallel", "arbitrary"),
        name="ssd",
    )(xbc, xbc, xbc, dt, xbc, xbc, xbc, dt, *spreads, conv_w, conv_b, dt_bias, a_log, d_skip)


def _to_pieces(ref, y):
    w = ref.shape[-1]
    for j in range(SC_ROW_SPLIT):
        ref[j] = y[:, j * w:(j + 1) * w].astype(ref.dtype)


def _from_pieces(ref):
    return jnp.concatenate([ref[j] for j in range(SC_ROW_SPLIT)], axis=1)


def _outproj_ln_kernel(*refs, even):
    if even:
        oa_ref, yf_ref, yb_ref, z_ref, nrm_ref = refs[:5]
        a = oa_ref[...]
        y_ssd = yf_ref[...].astype(F32) + yb_ref[...].astype(F32)
        b = _rms(y_ssd * _silu(z_ref[...].astype(F32))) * nrm_ref[...]
    else:
        of_ref, ob_ref, r_ref, gn_ref, sgu_ref = refs[:5]
        o = of_ref[...].astype(F32) + ob_ref[...].astype(F32)
        o = jnp.concatenate([_rms(o[:, h * GLA_VDIM:(h + 1) * GLA_VDIM]) for h in range(GLA_HEADS)], axis=1)
        a = o * gn_ref[...] * _silu(r_ref[...].astype(F32))
        b = sgu_ref[...]
    x_ref, wa_ref, wb_ref, g_ref, beta_ref, o_ref = refs[5:11]
    y = _dot(a.astype(BF), wa_ref[...]) + _dot(b.astype(BF), wb_ref[...])
    out = _layernorm(DN_ALPHA * x_ref[...] + y, g_ref[...], beta_ref[...])
    o_ref[...] = out
    for p_ref in refs[11:]:
        _to_pieces(p_ref, out)


def _outproj_ln(mix, x, wa, wb, g, beta, even):
    t, d = x.shape
    tm = min(ROW_TILE, t)
    row = lambda i: (i, 0)
    full = lambda i: (0, 0)
    out_specs = [pl.BlockSpec((tm, d), row)]
    out_shape = [jax.ShapeDtypeStruct((t, d), F32)]
    if not even:
        out_specs.append(pl.BlockSpec((SC_ROW_SPLIT, tm, d // SC_ROW_SPLIT), lambda i: (0, i, 0)))
        out_shape.append(jax.ShapeDtypeStruct((SC_ROW_SPLIT, t, d // SC_ROW_SPLIT), F32))
    mix_specs = [pl.BlockSpec((1, m.shape[1]), full) if m.shape[0] == 1 else pl.BlockSpec((tm, m.shape[1]), row)
                 for m in mix]
    res = pl.pallas_call(
        functools.partial(_outproj_ln_kernel, even=even),
        grid=(t // tm,),
        in_specs=mix_specs + [pl.BlockSpec((tm, d), row),
                              _resident(wa.shape, full), _resident(wb.shape, full),
                              pl.BlockSpec((1, d), full), pl.BlockSpec((1, d), full)],
        out_specs=out_specs,
        out_shape=out_shape,
        compiler_params=_cparams("parallel"),
        name="outproj_ln",
    )(*mix, x, wa, wb, g, beta)
    return res[0] if even else res


def _swiglu_acc(xb, wg_ref, wu_ref, wd_ref, acc_sc, nf):
    for f in range(nf):
        cols = slice(f * FF_CHUNK, (f + 1) * FF_CHUNK)
        h = _silu(_dot(xb, wg_ref[:, cols])) * _dot(xb, wu_ref[:, cols])
        part = _dot(h.astype(BF), wd_ref[cols, :])
        if f == 0:
            acc_sc[...] = part
        else:
            acc_sc[...] += part


def _ffn_ln_kernel(x_ref, wg_ref, wu_ref, wd_ref, g_ref, beta_ref, o_ref, acc_sc, *, nf):
    x = x_ref[...]
    _swiglu_acc(x.astype(BF), wg_ref, wu_ref, wd_ref, acc_sc, nf)
    o_ref[...] = _layernorm(DN_ALPHA * x + acc_sc[...], g_ref[...], beta_ref[...])


def _ffn_ln(x, wg, wu, wd, g, beta):
    t, d = x.shape
    f = wg.shape[1]
    tm = min(ROW_TILE, t)
    row = lambda i: (i, 0)
    full = lambda i: (0, 0)
    return pl.pallas_call(
        functools.partial(_ffn_ln_kernel, nf=f // FF_CHUNK),
        grid=(t // tm,),
        in_specs=[pl.BlockSpec((tm, d), row),
                  _resident((d, f), full), _resident((d, f), full), _resident((f, d), full),
                  pl.BlockSpec((1, d), full), pl.BlockSpec((1, d), full)],
        out_specs=pl.BlockSpec((tm, d), row),
        out_shape=jax.ShapeDtypeStruct((t, d), F32),
        scratch_shapes=[pltpu.VMEM((tm, d), F32)],
        compiler_params=_cparams("parallel"),
        name="ffn_ln",
    )(x, wg, wu, wd, g, beta)


def _moe_ffn_kernel(te_ref, tv_ref, xs_ref, wg_ref, wu_ref, wd_ref, o_ref, acc_sc, *, nf):
    @pl.when(tv_ref[pl.program_id(0)] > 0)
    def _():
        _swiglu_acc(_from_pieces(xs_ref).astype(BF), wg_ref, wu_ref, wd_ref, acc_sc, nf)
        _to_pieces(o_ref, acc_sc[...])


def _moe_ffn(xs, tile_expert, tile_valid, wg, wu, wd):
    ns, p, w = xs.shape
    d = ns * w
    f = wg.shape[2]
    tm = MOE_TILE
    row = lambda i, te, tv: (0, i, 0)
    grid_spec = pltpu.PrefetchScalarGridSpec(
        num_scalar_prefetch=2,
        grid=(p // tm,),
        in_specs=[pl.BlockSpec((ns, tm, w), row),
                  _resident((None, d, f), lambda i, te, tv: (te[i], 0, 0)),
                  _resident((None, d, f), lambda i, te, tv: (te[i], 0, 0)),
                  _resident((None, f, d), lambda i, te, tv: (te[i], 0, 0))],
        out_specs=pl.BlockSpec((ns, tm, w), row),
        scratch_shapes=[pltpu.VMEM((tm, d), F32)],
    )
    return pl.pallas_call(
        functools.partial(_moe_ffn_kernel, nf=f // FF_CHUNK),
        grid_spec=grid_spec,
        out_shape=jax.ShapeDtypeStruct((ns, p, w), F32),
        compiler_params=_cparams("arbitrary"),
        name="moe_ffn",
    )(tile_expert, tile_valid, xs, wg, wu, wd)


def _gla_direction(q_ref, k_ref, v_ref, gl_ref, w2_ref, gb_ref, o_ref, st_sc, *, reverse):
    L = GLA_CHUNK
    R = GLA_ROWS
    dk = GLA_KDIM
    dv = GLA_VDIM
    hk = GLA_HEADS * dk
    hv = GLA_HEADS * dv
    g2 = _split_bf16(gl_ref[0], 2)
    pre = _dot(jnp.concatenate([g2, g2[:, :LANES]], axis=1), w2_ref[...]) + gb_ref[...]
    yield
    lg = -_softplus(-pre) * (1.0 / GLA_TAU)
    row_i = lax.broadcasted_iota(jnp.int32, (R, R), 0)
    col_i = lax.broadcasted_iota(jnp.int32, (R, R), 1)
    intra = ((row_i // L) == (col_i // L)) & ((col_i >= row_i) if reverse else (col_i <= row_i))
    bc3 = _dot(intra.astype(BF), _split_bf16(lg, 3))
    yield
    bc = bc3[:, 0:hk] + bc3[:, hk:2 * hk] + bc3[:, 2 * hk:]
    mid = (L // 2 - 1) if reverse else (L // 2)
    end = 0 if reverse else (L - 1)
    ref_b = jnp.concatenate([jnp.broadcast_to(bc[ci * L + mid:ci * L + mid + 1], (L, hk))
                             for ci in range(R // L)], axis=0)
    end_b = jnp.concatenate([jnp.broadcast_to(bc[ci * L + end:ci * L + end + 1], (L, hk))
                             for ci in range(R // L)], axis=0)
    q = q_ref[0] * (dk ** -0.5)
    k = k_ref[0]
    qi = q * jnp.exp(bc - ref_b)
    ki = k * jnp.exp(ref_b - bc)
    qe = q * jnp.exp(bc)
    kd = k * jnp.exp(end_b - bc)
    first = 1 if reverse else 0
    in_first = (lax.broadcasted_iota(jnp.int32, (R, hk), 0) // L) == first
    d_first = jnp.exp(bc[first * L + end:first * L + end + 1])
    d_second = jnp.exp(bc[(1 - first) * L + end:(1 - first) * L + end + 1])
    qx = jnp.where(in_first, qe, qe * d_first).astype(BF)
    kx = jnp.where(in_first, kd * d_second, kd).astype(BF)
    qe_m = jnp.where(in_first, 0.0, qe)
    kd_m = jnp.where(in_first, kd, 0.0)
    yield

    a_heads = []
    for h in range(GLA_HEADS):
        kl = slice(h * dk, (h + 1) * dk)
        lhs = jnp.concatenate([qi[:, kl], qe_m[:, kl]], axis=0).astype(BF)
        rhs = jnp.concatenate([ki[:, kl], kd_m[:, kl]], axis=0).astype(BF)
        full = _dot_nt(lhs, rhs)
        a_heads.append((jnp.where(intra, full[0:R, 0:R], 0.0) + full[R:, R:]).astype(BF))
        yield
    vb = v_ref[0].astype(BF)
    lane_v = lax.broadcasted_iota(jnp.int32, (R, hv), 1)
    vbd = jnp.concatenate([jnp.where(lane_v // dv == h, vb, jnp.zeros_like(vb)) for h in range(GLA_HEADS)], axis=0)
    o = _dot(jnp.concatenate(a_heads, axis=1), vbd)
    yield
    st = st_sc[...]
    o_ref[0] = (o + _dot_nt(qx, st.astype(BF))).astype(o_ref.dtype)
    upd = _dot(v_ref[0].T.astype(BF), kx)
    on_diag = (lax.broadcasted_iota(jnp.int32, (hv, hk), 0) // dv) == (lax.broadcasted_iota(jnp.int32, (hv, hk), 1) // dk)
    st_sc[...] = st * (d_first * d_second) + jnp.where(on_diag, upd, 0.0)
    yield


def _gla_kernel(fq, fk, fv, fgl, bq, bk, bv, bgl, w2f_ref, gbf_ref, w2b_ref, gbb_ref, of_ref, ob_ref, st_sc):
    @pl.when(pl.program_id(1) == 0)
    def _():
        st_sc[...] = jnp.zeros(st_sc.shape, F32)

    _interleave(_gla_direction(fq, fk, fv, fgl, w2f_ref, gbf_ref, of_ref, st_sc.at[0], reverse=False),
                _gla_direction(bq, bk, bv, bgl, w2b_ref, gbb_ref, ob_ref, st_sc.at[1], reverse=True))


def _gla(q, k, v, gl, w2f, w2b, gbf, gbb):
    b, s, _ = q.shape
    R = GLA_ROWS
    nb = s // R
    hk = GLA_HEADS * GLA_KDIM
    hv = GLA_HEADS * GLA_VDIM
    full = lambda bi, c: (0, 0)

    def views(cidx):
        row = lambda bi, c: (bi, cidx(c), 0)
        return [pl.BlockSpec((1, R, hk), row), pl.BlockSpec((1, R, hk), row), pl.BlockSpec((1, R, hv), row),
                pl.BlockSpec((1, R, LANES), row)]

    return pl.pallas_call(
        _gla_kernel,
        grid=(b, nb),
        in_specs=views(lambda c: c) + views(lambda c: nb - 1 - c) + [
            pl.BlockSpec((3 * LANES, hk), full), pl.BlockSpec((1, hk), full),
            pl.BlockSpec((3 * LANES, hk), full), pl.BlockSpec((1, hk), full)],
        out_specs=[pl.BlockSpec((1, R, hv), lambda bi, c: (bi, c, 0)),
                   pl.BlockSpec((1, R, hv), lambda bi, c: (bi, nb - 1 - c, 0))],
        out_shape=[jax.ShapeDtypeStruct((b, s, hv), BF)] * 2,
        scratch_shapes=[pltpu.VMEM((2, hv, hk), F32)],
        compiler_params=_cparams("parallel", "arbitrary"),
        name="gla",
    )(q, k, v, gl, q, k, v, gl, w2f, gbf, w2b, gbb)


def _proj_odd_kernel(x_ref, w_ref, g_ref, b_ref, ws_ref, bias_ref, q_ref, k_ref, v_ref, r_ref, gl_ref, o_ref):
    xb = x_ref[...].astype(BF)
    off = _project_into(xb, w_ref, 0, (q_ref, k_ref, v_ref, r_ref, gl_ref))
    x = _dot(xb, w_ref[:, off:off + 2 * SGU_WIDTH])
    gel = x * (0.5 * (1.0 + jnp.tanh(math.sqrt(2.0 / math.pi) * (x + 0.044715 * (x * x * x)))))
    u = gel[:, :SGU_WIDTH]
    svn = _layernorm(gel[:, SGU_WIDTH:], g_ref[...], b_ref[...]).astype(BF)
    c = SGU_CHUNK
    for ci in range(x.shape[0] // c):
        rows = slice(ci * c, (ci + 1) * c)
        for gi in range(SGU_GROUPS):
            cols = slice(gi * SGU_GROUP_DIM, (gi + 1) * SGU_GROUP_DIM)
            sp = _dot(ws_ref[gi], svn[rows, cols]) + bias_ref[:, cols]
            o_ref[rows, cols] = (u[rows, cols] * sp).astype(o_ref.dtype)


def _proj_odd(x, w_in, ln_g, ln_b, ws_bf, bias_full):
    t, d = x.shape
    c = SGU_CHUNK
    tm = min(ROW_TILE, t)
    hk = GLA_HEADS * GLA_KDIM
    hv = GLA_HEADS * GLA_VDIM
    row = lambda i: (i, 0)
    full = lambda i: (0, 0)
    widths = (hk, hk, hv, hv, LANES, SGU_WIDTH)
    dtypes = (F32, F32, F32, BF, F32, BF)
    return pl.pallas_call(
        _proj_odd_kernel,
        grid=(t // tm,),
        in_specs=[pl.BlockSpec((tm, d), row), _resident(w_in.shape, full),
                  pl.BlockSpec((1, SGU_WIDTH), full), pl.BlockSpec((1, SGU_WIDTH), full),
                  pl.BlockSpec((SGU_GROUPS, c, c), lambda i: (0, 0, 0)),
                  pl.BlockSpec((c, SGU_WIDTH), full)],
        out_specs=[pl.BlockSpec((tm, n), row) for n in widths],
        out_shape=[jax.ShapeDtypeStruct((t, n), dt) for n, dt in zip(widths, dtypes)],
        compiler_params=_cparams("parallel"),
        name="proj_odd",
    )(x, w_in, ln_g, ln_b, ws_bf, bias_full)


def _router_kernel(x_ref, wr_ref, route_ref, cnt_ref, base_sc):
    @pl.when(pl.program_id(0) == 0)
    def _():
        base_sc[...] = jnp.zeros(base_sc.shape, F32)

    tr = x_ref.shape[0]
    x2 = _split_bf16(x_ref[...], 2)
    logits = _dot(jnp.concatenate([x2, x2[:, :x_ref.shape[1]]], axis=1), wr_ref[...])
    lane = lax.broadcasted_iota(jnp.int32, (tr, LANES), 1).astype(F32)
    lg = jnp.where(lane < N_EXPERTS, logits, -jnp.inf)
    m1 = jnp.max(lg, axis=1, keepdims=True)
    i1 = jnp.min(jnp.where(lg == m1, lane, float(LANES)), axis=1, keepdims=True)
    lg2 = jnp.where(lane == i1, -jnp.inf, lg)
    m2 = jnp.max(lg2, axis=1, keepdims=True)
    i2 = jnp.min(jnp.where(lg2 == m2, lane, float(LANES)), axis=1, keepdims=True)
    e = jnp.exp(m2 - m1)
    g1 = 1.0 / (1.0 + e)
    g2 = e / (1.0 + e)
    oh1 = (lane == i1).astype(F32)
    oh2 = (lane == i2).astype(F32)
    oh = oh1 + oh2
    row_i = lax.broadcasted_iota(jnp.int32, (tr, tr), 0)
    col_i = lax.broadcasted_iota(jnp.int32, (tr, tr), 1)
    before = _dot((col_i < row_i).astype(BF), oh.astype(BF)) + base_sc[...]
    r1 = jnp.sum(oh1 * before, axis=1, keepdims=True)
    r2 = jnp.sum(oh2 * before, axis=1, keepdims=True)
    base_sc[...] += jnp.sum(oh, axis=0, keepdims=True)
    route = jnp.zeros((tr, LANES), F32)
    for idx, val in enumerate((i1, i2, r1, r2, g1, g2)):
        route = jnp.where(lane == float(idx), val, route)
    route_ref[...] = route
    cnt_ref[...] = base_sc[...]


def _router(x, wr_pad):
    t, d = x.shape
    tr = min(ROUTE_TILE, t)
    return pl.pallas_call(
        _router_kernel,
        grid=(t // tr,),
        in_specs=[pl.BlockSpec((tr, d), lambda i: (i, 0)), pl.BlockSpec((3 * d, LANES), lambda i: (0, 0))],
        out_specs=[pl.BlockSpec((tr, LANES), lambda i: (i, 0)), pl.BlockSpec((1, LANES), lambda i: (0, 0))],
        out_shape=[jax.ShapeDtypeStruct((t, LANES), F32), jax.ShapeDtypeStruct((1, LANES), F32)],
        scratch_shapes=[pltpu.VMEM((1, LANES), F32)],
        compiler_params=_cparams("arbitrary"),
        name="router",
    )(x, wr_pad)


def _piece_indices(pos, n_rows):
    base = jnp.arange(SC_ROW_SPLIT, dtype=jnp.int32)[:, None, None] * n_rows
    return (base + pos.T[None]).reshape(1, -1)


def _sc_scatter_rows(xp, idx, n_out):
    ns, t, w = xp.shape
    nblk = t // SC_WINDOW
    per_piece = idx.shape[1] // ns // SC_WINDOW
    mesh = plsc.VectorSubcoreMesh(core_axis_name="c", subcore_axis_name="s")

    @pl.kernel(out_type=jax.ShapeDtypeStruct((ns * n_out, w), xp.dtype), mesh=mesh)
    def k(x_hbm, i_hbm, o_hbm):
        def body(x_vmem, i_vmem):
            pltpu.sync_copy(x_vmem, o_hbm.at[i_vmem.at[0]])

        pltpu.emit_pipeline(
            body,
            grid=(idx.shape[1] // SC_WINDOW,),
            in_specs=[pl.BlockSpec((SC_WINDOW, w), index_map=lambda i: ((i // per_piece) * nblk + i % nblk, 0)),
                      pl.BlockSpec((1, SC_WINDOW), index_map=lambda i: (0, i))],
            out_specs=[],
            core_axis_name=("c", "s"),
            dimension_semantics=(pltpu.PARALLEL,),
        )(x_hbm, i_hbm)

    return k(xp.reshape(ns * t, w), idx).reshape(ns, n_out, w)


def _sc_gather_rows(yp, idx):
    ns, n, w = yp.shape
    mesh = plsc.VectorSubcoreMesh(core_axis_name="c", subcore_axis_name="s")

    @pl.kernel(out_type=jax.ShapeDtypeStruct((idx.shape[1], w), yp.dtype), mesh=mesh)
    def k(x_hbm, i_hbm, o_hbm):
        def body(i_vmem, o_vmem):
            pltpu.sync_copy(x_hbm.at[i_vmem.at[0]], o_vmem)

        pltpu.emit_pipeline(
            body,
            grid=(idx.shape[1] // SC_WINDOW,),
            in_specs=[pl.BlockSpec((1, SC_WINDOW), index_map=lambda i: (0, i))],
            out_specs=[pl.BlockSpec((SC_WINDOW, w), index_map=lambda i: (i, 0))],
            core_axis_name=("c", "s"),
            dimension_semantics=(pltpu.PARALLEL,),
        )(i_hbm, o_hbm)

    return k(yp.reshape(ns * n, w), idx)


def _combine_ln_kernel(x_ref, y_ref, route_ref, g_ref, beta_ref, o_ref):
    g1 = route_ref[:, 4:5]
    g2 = route_ref[:, 5:6]
    y = (g1 * jnp.concatenate([y_ref[j, 0] for j in range(SC_ROW_SPLIT)], axis=1)
         + g2 * jnp.concatenate([y_ref[j, 1] for j in range(SC_ROW_SPLIT)], axis=1))
    o_ref[...] = _layernorm(DN_ALPHA * x_ref[...] + y, g_ref[...], beta_ref[...])


def _combine_ln(x, y2, route, g, beta):
    t, d = x.shape
    tm = min(ROW_TILE, t)
    row = lambda i: (i, 0)
    full = lambda i: (0, 0)
    return pl.pallas_call(
        _combine_ln_kernel,
        grid=(t // tm,),
        in_specs=[pl.BlockSpec((tm, d), row),
                  pl.BlockSpec((SC_ROW_SPLIT, 2, tm, d // SC_ROW_SPLIT), lambda i: (0, 0, i, 0)),
                  pl.BlockSpec((tm, LANES), row), pl.BlockSpec((1, d), full), pl.BlockSpec((1, d), full)],
        out_specs=pl.BlockSpec((tm, d), row),
        out_shape=jax.ShapeDtypeStruct((t, d), F32),
        compiler_params=_cparams("parallel"),
        name="moe_combine_ln",
    )(x, y2, route, g, beta)


def _pad_cols(w, n):
    return jnp.pad(w, ((0, 0), (0, n - w.shape[1])))


def _prep_even(p):
    (w_in, q_norm, w_uq, kv_norm, w_ukv, conv_w, conv_b, dt_bias, a_log, d_skip, ssm_norm, w_out,
     ln1_g, ln1_b, w_gate, w_up, w_down, ln2_g, ln2_b) = p
    o = 0
    cq = w_in[:, o:o + Q_RANK]; o += Q_RANK
    ckv = w_in[:, o:o + KV_RANK]; o += KV_RANK
    kr = w_in[:, o:o + QK_ROPE]; o += QK_ROPE
    z = w_in[:, o:o + SSD_INNER]; o += SSD_INNER
    xbc = w_in[:, o:o + CONV_DIM]; o += CONV_DIM
    dt = w_in[:, o:]
    half = QK_ROPE // 2
    zeros = lambda n: jnp.zeros((w_in.shape[0], n), F32)
    kra = jnp.concatenate([zeros(QK_NOPE), kr, zeros(LANES - QK_NOPE - QK_ROPE)], axis=1)
    krb = jnp.concatenate([zeros(QK_NOPE), -kr[:, half:], kr[:, :half], zeros(LANES - QK_NOPE - QK_ROPE)], axis=1)
    w_in_p = jnp.concatenate([cq, ckv, kra, krb, z, xbc, _pad_cols(dt, LANES)], axis=1).astype(BF)

    wq = w_uq.reshape(Q_RANK, MLA_HEADS, QK_NOPE + QK_ROPE)
    nope, rope = wq[..., :QK_NOPE], wq[..., QK_NOPE:]
    zq = lambda n: jnp.zeros((Q_RANK, MLA_HEADS, n), F32)
    wqa = jnp.concatenate([nope, rope, zq(LANES - QK_NOPE - QK_ROPE)], axis=-1)
    wqb = jnp.concatenate([zq(QK_NOPE), -rope[..., half:], rope[..., :half], zq(LANES - QK_NOPE - QK_ROPE)], axis=-1)
    wkv = w_ukv.reshape(KV_RANK, MLA_HEADS, QK_NOPE + V_DIM)
    zk = jnp.zeros((KV_RANK, MLA_HEADS, LANES - QK_NOPE), F32)
    wk = jnp.concatenate([wkv[..., :QK_NOPE], zk], axis=-1)
    vv_t = jnp.transpose(wkv[..., QK_NOPE:], (1, 2, 0))
    wvt = jnp.concatenate([vv_t, jnp.zeros((MLA_HEADS, V_ROWS - V_DIM, KV_RANK), F32)], axis=1)
    vadd = jnp.tile((jnp.arange(V_ROWS) == V_DIM).astype(F32), MLA_HEADS)[:, None]
    hw = MLA_HEADS * LANES
    return dict(
        w_in=w_in_p, q_norm=q_norm[None], kv_norm=kv_norm[None],
        wqa=wqa.reshape(Q_RANK, hw).astype(BF), wqb=wqb.reshape(Q_RANK, hw).astype(BF),
        wk=wk.reshape(KV_RANK, hw).astype(BF), wvt=wvt.reshape(MLA_HEADS * V_ROWS, KV_RANK).astype(BF), vadd=vadd,
        conv_w=conv_w, conv_b=conv_b[None],
        dt_bias=_pad_cols(dt_bias.reshape(1, -1), LANES), a_log=_pad_cols(a_log.reshape(1, -1), LANES),
        d_skip=jnp.repeat(d_skip, SSD_HEAD_DIM)[None], ssm_norm=ssm_norm[None],
        wo_a=w_out[:MLA_HEADS * V_DIM].astype(BF), wo_b=w_out[MLA_HEADS * V_DIM:].astype(BF),
        ln1_g=ln1_g[None], ln1_b=ln1_b[None],
        wg=w_gate.astype(BF), wu=w_up.astype(BF), wd=w_down.astype(BF),
        ln2_g=ln2_g[None], ln2_b=ln2_b[None])


def _prep_odd(p):
    (w_in, gate_w2, gate_b, gla_norm, sgu_ln_g, sgu_ln_b, w_s, b_s, w_out, ln1_g, ln1_b,
     w_router, we_gate, we_up, we_down, ln2_g, ln2_b) = p
    hk = GLA_HEADS * GLA_KDIM
    hv = GLA_HEADS * GLA_VDIM
    o = 2 * hk + 2 * hv
    gl = w_in[:, o:o + 2 * GLA_GATE_RANK]
    w_in_p = jnp.concatenate([w_in[:, :o], _pad_cols(gl, LANES), w_in[:, o + 2 * GLA_GATE_RANK:]], axis=1).astype(BF)
    zr = lambda n: jnp.zeros((n, hk), F32)
    def pieces(w):
        hi = w.astype(BF)
        lo = (w - hi.astype(F32)).astype(BF)
        return jnp.concatenate([hi, hi, lo], axis=0)

    w2f = pieces(jnp.concatenate([gate_w2[0], zr(LANES - GLA_GATE_RANK)], axis=0))
    w2b = pieces(jnp.concatenate([zr(GLA_GATE_RANK), gate_w2[1], zr(LANES - 2 * GLA_GATE_RANK)], axis=0))
    bias_full = jnp.repeat(b_s.T, SGU_GROUP_DIM, axis=1)
    return dict(
        w_in=w_in_p, w2f=w2f, w2b=w2b, gbf=gate_b[0][None], gbb=gate_b[1][None], gla_norm=gla_norm[None],
        sgu_g=sgu_ln_g[None], sgu_b=sgu_ln_b[None], ws=w_s.astype(BF), sgu_bias=bias_full,
        wo_a=w_out[:hv].astype(BF), wo_b=w_out[hv:].astype(BF), ln1_g=ln1_g[None], ln1_b=ln1_b[None],
        w_router=pieces(_pad_cols(w_router, LANES)),
        wg=we_gate.astype(BF), wu=we_up.astype(BF), wd=we_down.astype(BF),
        ln2_g=ln2_g[None], ln2_b=ln2_b[None])


def _rope_tables(s):
    half = QK_ROPE // 2
    inv = jnp.exp(-math.log(ROPE_THETA) * jnp.arange(half, dtype=F32) / half)
    ang = jnp.arange(s, dtype=F32)[:, None] * inv[None, :]
    cos, sin = jnp.cos(ang), jnp.sin(ang)
    pad = LANES - QK_NOPE - QK_ROPE
    cos_t = jnp.concatenate([jnp.ones((s, QK_NOPE), F32), cos, cos, jnp.ones((s, pad), F32)], axis=1)
    sin_t = jnp.concatenate([jnp.zeros((s, QK_NOPE), F32), sin, sin, jnp.zeros((s, pad), F32)], axis=1)
    return cos_t, sin_t


def _even_layer(x, w, b, s):
    t = b * s
    cos_t, sin_t = _rope_tables(s)
    q, k, vt, z, xbc, dt = _proj_even(x, w["w_in"], cos_t, sin_t, w["q_norm"], w["kv_norm"], w["wqa"], w["wqb"],
                                      w["wk"], w["wvt"], w["vadd"], b, s)
    hw = MLA_HEADS * LANES
    o_attn = _flash(q.reshape(b, s, hw), k.reshape(b, s, hw), vt)
    y_f, y_b = _ssd(xbc.reshape(b, s, CONV_DIM), dt.reshape(b, s, LANES),
                    w["conv_w"], w["conv_b"], w["dt_bias"], w["a_log"], w["d_skip"])
    mix = (o_attn.reshape(t, -1), y_f.reshape(t, -1), y_b.reshape(t, -1), z, w["ssm_norm"])
    x1 = _outproj_ln(mix, x, w["wo_a"], w["wo_b"], w["ln1_g"], w["ln1_b"], even=True)
    return _ffn_ln(x1, w["wg"], w["wu"], w["wd"], w["ln2_g"], w["ln2_b"])


def _moe(x1, x1p, w):
    t, d = x1.shape
    route, cnt = _router(x1, w["w_router"])
    eid = route[:, 0:2].astype(jnp.int32)
    rank = route[:, 2:4].astype(jnp.int32)
    counts = cnt[0, :N_EXPERTS].astype(jnp.int32)
    tm = MOE_TILE
    padded = ((counts + tm - 1) // tm) * tm
    ends = jnp.cumsum(padded)
    offs = ends - padded
    pos = offs[eid] + rank
    p_rows = 2 * t + N_EXPERTS * tm
    tiles = jnp.arange(p_rows // tm, dtype=jnp.int32)
    tile_ends = ends // tm
    tile_expert = jnp.minimum(jnp.sum(tiles[:, None] >= tile_ends[None, :], axis=1), N_EXPERTS - 1).astype(jnp.int32)
    tile_valid = (tiles < tile_ends[-1]).astype(jnp.int32)
    idx = _piece_indices(pos, p_rows)
    xs = _sc_scatter_rows(x1p, idx, p_rows)
    ys = _moe_ffn(xs, tile_expert, tile_valid, w["wg"], w["wu"], w["wd"])
    y2 = _sc_gather_rows(ys, idx).reshape(SC_ROW_SPLIT, 2, t, d // SC_ROW_SPLIT)
    return _combine_ln(x1, y2, route, w["ln2_g"], w["ln2_b"])


def _odd_layer(x, w, b, s):
    t = b * s
    hk = GLA_HEADS * GLA_KDIM
    hv = GLA_HEADS * GLA_VDIM
    q, k, v, r, gl, o_sgu = _proj_odd(x, w["w_in"], w["sgu_g"], w["sgu_b"], w["ws"], w["sgu_bias"])
    o_f, o_b = _gla(q.reshape(b, s, hk), k.reshape(b, s, hk), v.reshape(b, s, hv), gl.reshape(b, s, LANES),
                    w["w2f"], w["w2b"], w["gbf"], w["gbb"])
    mix = (o_f.reshape(t, hv), o_b.reshape(t, hv), r, w["gla_norm"], o_sgu)
    x1, x1p = _outproj_ln(mix, x, w["wo_a"], w["wo_b"], w["ln1_g"], w["ln1_b"], even=False)
    return _moe(x1, x1p, w)


def _trunk(x, ev_w, od_w):
    b, s, d = x.shape
    x = x.reshape(b * s, d)
    for i in range(DEPTH):
        if i % 2 == 0:
            x = _even_layer(x, ev_w[i // 2], b, s)
        else:
            x = _odd_layer(x, od_w[i // 2], b, s)
    return x.reshape(b, s, d)


def kernel(x_prompt, x_sample, ev_w_in, ev_q_norm, ev_w_uq, ev_kv_norm, ev_w_ukv, ev_conv_w, ev_conv_b, ev_dt_bias, ev_a_log, ev_d_skip, ev_ssm_norm, ev_w_out, ev_ln1_g, ev_ln1_b, ev_w_gate, ev_w_up, ev_w_down, ev_ln2_g, ev_ln2_b, od_w_in, od_gate_w2, od_gate_b, od_gla_norm, od_sgu_ln_g, od_sgu_ln_b, od_w_s, od_b_s, od_w_out, od_ln1_g, od_ln1_b, od_w_router, od_we_gate, od_we_up, od_we_down, od_ln2_g, od_ln2_b):
    ev = (ev_w_in, ev_q_norm, ev_w_uq, ev_kv_norm, ev_w_ukv, ev_conv_w, ev_conv_b, ev_dt_bias,
          ev_a_log, ev_d_skip, ev_ssm_norm, ev_w_out, ev_ln1_g, ev_ln1_b, ev_w_gate, ev_w_up,
          ev_w_down, ev_ln2_g, ev_ln2_b)
    od = (od_w_in, od_gate_w2, od_gate_b, od_gla_norm, od_sgu_ln_g, od_sgu_ln_b, od_w_s, od_b_s,
          od_w_out, od_ln1_g, od_ln1_b, od_w_router, od_we_gate, od_we_up, od_we_down,
          od_ln2_g, od_ln2_b)
    ev_w = [_prep_even(tuple(t[i] for t in ev)) for i in range(ev_w_in.shape[0])]
    od_w = [_prep_odd(tuple(t[i] for t in od)) for i in range(od_w_in.shape[0])]
    return (_trunk(x_prompt, ev_w, od_w), _trunk(x_sample, ev_w, od_w))
```

```python
import functools
import math

import jax
import jax.numpy as jnp
from jax import lax
from jax.experimental import pallas as pl
from jax.experimental.pallas import tpu as pltpu
from jax.experimental.pallas import tpu_sc as plsc

BF = jnp.bfloat16
F32 = jnp.float32

D_MODEL = 1024
DEPTH = 4
MLA_HEADS = 8
QK_NOPE = 64
QK_ROPE = 32
V_DIM = 64
Q_RANK = 256
KV_RANK = 128
ROPE_THETA = 10000.0
SSD_HEADS = 8
SSD_HEAD_DIM = 64
SSD_GROUPS = 2
SSD_STATE = 64
SSD_CHUNK = 128
CONV_K = 5
SSD_INNER = SSD_HEADS * SSD_HEAD_DIM
SSD_HPG = SSD_HEADS // SSD_GROUPS
CONV_DIM = SSD_INNER + 2 * SSD_GROUPS * SSD_STATE
GLA_HEADS = 4
GLA_KDIM = 64
GLA_VDIM = 128
GLA_GATE_RANK = 16
GLA_TAU = 16.0
GLA_CHUNK = 64
SGU_GROUPS = 4
SGU_CHUNK = 128
SGU_GROUP_DIM = 128
SGU_WIDTH = SGU_GROUPS * SGU_GROUP_DIM
D_FF = 2816
N_EXPERTS = 8
D_FF_EXPERT = 3584
DN_ALPHA = (2 * DEPTH) ** 0.25
EPS = 1e-5

LANES = 128
SUBLANES = 8
VMEM_LIMIT = 56 * 1024 * 1024
SC_WINDOW = 128
SC_ROW_SPLIT = 4

ROW_TILE = 512
FF_CHUNK = 256
ATT_TQ = 512
ATT_TK = 512
MOE_TILE = 512
ROUTE_TILE = 512
GLA_ROWS = 2 * GLA_CHUNK
V_ROWS = 80


def _cparams(*sem):
    return pltpu.CompilerParams(dimension_semantics=sem, vmem_limit_bytes=VMEM_LIMIT)


def _resident(shape, index_map):
    return pl.BlockSpec(shape, index_map, pipeline_mode=pl.Buffered(1))


def _rms(x):
    return x * lax.rsqrt(jnp.mean(x * x, axis=-1, keepdims=True) + EPS)


def _layernorm(x, g, b):
    mu = jnp.mean(x, axis=-1, keepdims=True)
    xc = x - mu
    var = jnp.mean(xc * xc, axis=-1, keepdims=True)
    return xc * lax.rsqrt(var + EPS) * g + b


def _silu(x):
    return x * jax.nn.sigmoid(x)


def _softplus(x):
    return jnp.maximum(x, 0.0) + jnp.log1p(jnp.exp(-jnp.abs(x)))


def _dot(a, b):
    return jnp.dot(a, b, preferred_element_type=F32)


def _dot_nt(a, b):
    return lax.dot_general(a, b, (((1,), (1,)), ((), ())), preferred_element_type=F32)


def _project_into(xb, w_ref, off, o_refs):
    for o_ref in o_refs:
        n = o_ref.shape[1]
        o_ref[...] = _dot(xb, w_ref[:, off:off + n]).astype(o_ref.dtype)
        off += n
    return off


MLA_IN = Q_RANK + KV_RANK + 2 * LANES


def _proj_even_kernel(x_ref, w_ref, cos_ref, sin_ref, qn_ref, kvn_ref, wqa_ref, wqb_ref, wk_ref, wvt_ref, vadd_ref,
                      q_ref, k_ref, vt_ref, z_ref, xbc_ref, dt_ref):
    xb = x_ref[...].astype(BF)
    _project_into(xb, w_ref, MLA_IN, (z_ref, xbc_ref, dt_ref))
    m = _dot(xb, w_ref[:, 0:MLA_IN])
    cq = m[:, 0:Q_RANK]
    ckv = m[:, Q_RANK:Q_RANK + KV_RANK]
    kra = m[:, Q_RANK + KV_RANK:Q_RANK + KV_RANK + LANES]
    krb = m[:, Q_RANK + KV_RANK + LANES:Q_RANK + KV_RANK + 2 * LANES]
    cos = cos_ref[...]
    sin = sin_ref[...]
    cos8 = jnp.concatenate([cos] * MLA_HEADS, axis=1)
    sin8 = jnp.concatenate([sin] * MLA_HEADS, axis=1)
    cqn = (_rms(cq) * qn_ref[...]).astype(BF)
    q = _dot(cqn, wqa_ref[...]) * cos8 + _dot(cqn, wqb_ref[...]) * sin8
    q_ref[...] = (q * ((QK_NOPE + QK_ROPE) ** -0.5 * math.log2(math.e))).astype(BF)
    ckvn = (_rms(ckv) * kvn_ref[...]).astype(BF)
    kr = kra * cos + krb * sin
    k = _dot(ckvn, wk_ref[...]) + jnp.concatenate([kr] * MLA_HEADS, axis=1)
    k_ref[...] = k.astype(BF)
    vt_ref[0] = (_dot_nt(wvt_ref[...], ckvn) + vadd_ref[...]).astype(BF)


def _proj_even(x, w_in, cos_t, sin_t, q_norm, kv_norm, wqa, wqb, wk, wvt, vadd, batch, seq):
    t, d = x.shape
    tm = min(ROW_TILE, seq)
    nseq = seq // tm
    hw = MLA_HEADS * LANES
    vr = MLA_HEADS * V_ROWS
    full = lambda i: (0, 0)
    row = lambda i: (i, 0)
    return pl.pallas_call(
        _proj_even_kernel,
        grid=(t // tm,),
        in_specs=[pl.BlockSpec((tm, d), row), _resident(w_in.shape, full),
                  pl.BlockSpec((tm, LANES), lambda i: (i % nseq, 0)),
                  pl.BlockSpec((tm, LANES), lambda i: (i % nseq, 0)),
                  pl.BlockSpec((1, Q_RANK), full), pl.BlockSpec((1, KV_RANK), full),
                  pl.BlockSpec((Q_RANK, hw), full), pl.BlockSpec((Q_RANK, hw), full),
                  pl.BlockSpec((KV_RANK, hw), full), pl.BlockSpec((vr, KV_RANK), full),
                  pl.BlockSpec((vr, 1), full)],
        out_specs=[pl.BlockSpec((tm, hw), row), pl.BlockSpec((tm, hw), row),
                   pl.BlockSpec((1, vr, tm), lambda i: (i // nseq, 0, i % nseq)),
                   pl.BlockSpec((tm, SSD_INNER), row), pl.BlockSpec((tm, CONV_DIM), row), pl.BlockSpec((tm, LANES), row)],
        out_shape=[jax.ShapeDtypeStruct((t, hw), BF), jax.ShapeDtypeStruct((t, hw), BF),
                   jax.ShapeDtypeStruct((batch, vr, seq), BF),
                   jax.ShapeDtypeStruct((t, SSD_INNER), BF), jax.ShapeDtypeStruct((t, CONV_DIM), F32),
                   jax.ShapeDtypeStruct((t, LANES), F32)],
        compiler_params=_cparams("parallel"),
        name="proj_even",
    )(x, w_in, cos_t, sin_t, q_norm, kv_norm, wqa, wqb, wk, wvt, vadd)


def _flash_kernel(q_ref, k_ref, vt_ref, o_ref, m_sc, acc_sc, s_sc, *, tk, nk, unroll):
    for h in range(2):
        m_sc[h] = jnp.full(m_sc.shape[1:], -jnp.inf, F32)
        acc_sc[h] = jnp.zeros(acc_sc.shape[1:], F32)

    def scores(j, slot):
        off = pl.multiple_of(j * tk, tk)
        for h in range(2):
            lanes = slice(h * LANES, (h + 1) * LANES)
            s_sc[slot, h] = _dot_nt(k_ref[0, pl.ds(off, tk), lanes], q_ref[0, :, lanes])

    def consume(j, slot):
        off = pl.multiple_of(j * tk, tk)
        for h in range(2):
            st = s_sc[slot, h]
            m_prev = m_sc[h]
            m_new = jnp.maximum(m_prev, jnp.max(st, axis=0, keepdims=True))
            p = jnp.exp2(st - m_new[0:1, :]).astype(BF)
            alpha = jnp.exp2(m_prev - m_new)
            pv = _dot(vt_ref[0, h * V_ROWS:(h + 1) * V_ROWS, pl.ds(off, tk)], p)
            acc_sc[h] = alpha[0:1, :] * acc_sc[h] + pv
            m_sc[h] = m_new

    scores(0, 0)

    def body(jj, carry):
        j = unroll * jj
        for u in range(unroll):
            scores(jnp.minimum(j + u + 1, nk - 1), (u + 1) % 2)
            consume(j + u, u % 2)
        return carry

    lax.fori_loop(0, nk // unroll, body, 0)
    outs = []
    for h in range(2):
        acc = acc_sc[h]
        outs.append(acc[0:V_DIM, :] / acc[V_DIM:V_DIM + 1, :])
    o_ref[0] = jnp.concatenate(outs, axis=0).T.astype(o_ref.dtype)


def _flash(q, k, vt):
    b, s, hw = q.shape
    tq = min(ATT_TQ, s)
    tk = min(ATT_TK, s)
    nk = s // tk
    unroll = next(u for u in (8, 4, 2, 1) if nk % u == 0)
    pairs = MLA_HEADS // 2
    return pl.pallas_call(
        functools.partial(_flash_kernel, tk=tk, nk=nk, unroll=unroll),
        grid=(b, pairs, s // tq),
        in_specs=[pl.BlockSpec((1, tq, 2 * LANES), lambda bi, hp, i: (bi, i, hp)),
                  pl.BlockSpec((1, s, 2 * LANES), lambda bi, hp, i: (bi, 0, hp)),
                  pl.BlockSpec((1, 2 * V_ROWS, s), lambda bi, hp, i: (bi, hp, 0))],
        out_specs=pl.BlockSpec((1, tq, LANES), lambda bi, hp, i: (bi, i, hp)),
        out_shape=jax.ShapeDtypeStruct((b, s, MLA_HEADS * V_DIM), BF),
        scratch_shapes=[pltpu.VMEM((2, SUBLANES, tq), F32), pltpu.VMEM((2, V_ROWS, tq), F32),
                        pltpu.VMEM((2, 2, tk, tq), F32)],
        compiler_params=_cparams("parallel", "parallel", "arbitrary"),
        name="mla_flash",
    )(q, k, vt)


def _split_bf16(x, pieces):
    out = []
    for _ in range(pieces):
        p = x.astype(BF)
        out.append(p)
        x = x - p.astype(F32)
    return jnp.concatenate(out, axis=1)


def _spread_matrix(first_lane, heads, width, pieces):
    src = jnp.arange(LANES)[:, None] - first_lane
    dst = jnp.arange(heads * width)[None, :] // width
    return jnp.tile((src == dst).astype(BF), (pieces, 1))


def _ssd_direction(xc_ref, xp_ref, xn_ref, dt_ref, ep_ref, el_ref, cw_ref, cb_ref, dtb_ref, alog_ref, dskip_ref,
                   o_ref, xe_sc, st_sc, *, reverse, cc, nc):
    L = SSD_CHUNK
    N = SSD_STATE
    P = SSD_HEAD_DIM
    xe_sc[0:SUBLANES, :] = jnp.where(cc > 0, xp_ref[0], 0.0)
    xe_sc[SUBLANES:SUBLANES + L, :] = xc_ref[0]
    xe_sc[SUBLANES + L:2 * SUBLANES + L, :] = jnp.where(cc < nc - 1, xn_ref[0], 0.0)
    conv = cb_ref[...] + cw_ref[0:1, :] * xe_sc[pl.ds(SUBLANES - CONV_K // 2, L), :]
    for j in range(1, CONV_K):
        conv = conv + cw_ref[j:j + 1, :] * xe_sc[pl.ds(SUBLANES - CONV_K // 2 + j, L), :]
    yield
    xbc = _silu(conv)
    xs = xbc[:, :SSD_INNER]
    bc = xbc[:, SSD_INNER:]
    bc_t = bc.T

    lane = lax.broadcasted_iota(jnp.int32, (L, LANES), 1)
    dtv = _softplus(dt_ref[0] + dtb_ref[...])
    a = jnp.where(lane[0:1] < 2 * SSD_HEADS, -jnp.exp(alog_ref[...]), 0.0)
    dta = dtv * a
    row_i = lax.broadcasted_iota(jnp.int32, (L, L), 0)
    col_i = lax.broadcasted_iota(jnp.int32, (L, L), 1)
    causal = (col_i >= row_i) if reverse else (col_i <= row_i)
    acs3 = _dot(causal.astype(BF), _split_bf16(dta, 3))
    yield
    acs = acs3[:, 0:LANES] + acs3[:, LANES:2 * LANES] + acs3[:, 2 * LANES:]
    acs_t = acs.T
    d0 = SSD_HEADS if reverse else 0
    end = 0 if reverse else L - 1
    tot = acs[end:end + 1, :]
    yield

    stacked = jnp.concatenate([dtv, jnp.exp(acs), jnp.exp(tot - acs),
                               jnp.broadcast_to(jnp.exp(tot), (2 * SUBLANES, LANES))], axis=0)
    spread = _dot(_split_bf16(stacked, 2), ep_ref[...])
    dt_x = spread[0:L]
    ea_x = spread[L:2 * L]
    eb_x = spread[2 * L:3 * L]
    et_x = spread[3 * L:3 * L + 1]
    col_x = _dot(_split_bf16(acs, 3), el_ref[...])
    yield
    row_x = jnp.concatenate([jnp.broadcast_to(acs_t[d0 + h:d0 + h + 1, :], (L, L)) for h in range(SSD_HEADS)], axis=1)
    causal_x = jnp.concatenate([causal] * SSD_HEADS, axis=1)
    decay_x = jnp.exp(jnp.where(causal_x, col_x - row_x, -jnp.inf))
    xdt = xs * dt_x
    xdt_b = xdt.astype(BF)
    xw_b = (xdt * eb_x).astype(BF)
    gw = SSD_HPG * P
    lane_g = lax.broadcasted_iota(jnp.int32, (L, gw), 1)
    yield

    ys = []
    for g in range(SSD_GROUPS):
        bm_g = bc[:, g * N:(g + 1) * N].astype(BF)
        cm_g = bc[:, (SSD_GROUPS + g) * N:(SSD_GROUPS + g + 1) * N].astype(BF)
        bm_t_g = bc_t[g * N:(g + 1) * N, :].astype(BF)
        cb = _dot_nt(cm_g, bm_g)
        m_g = (jnp.concatenate([cb] * SSD_HPG, axis=1) * decay_x[:, g * SSD_HPG * L:(g + 1) * SSD_HPG * L]).astype(BF)
        xg = xdt_b[:, g * gw:(g + 1) * gw]
        xbd = jnp.concatenate([jnp.where(lane_g // P == j, xg, jnp.zeros_like(xg)) for j in range(SSD_HPG)], axis=0)
        y_diag = _dot(m_g, xbd)
        states_t = _dot(bm_t_g, xw_b[:, g * gw:(g + 1) * gw])
        prev_t = st_sc[g]
        y_off = _dot(cm_g, prev_t.astype(BF)) * ea_x[:, g * gw:(g + 1) * gw]
        st_sc[g] = prev_t * et_x[:, g * gw:(g + 1) * gw] + states_t
        ys.append(y_diag + y_off)
        yield
    y = jnp.concatenate(ys, axis=1)
    if not reverse:
        y = y + dskip_ref[...] * xs
    o_ref[0] = y.astype(o_ref.dtype)


def _interleave(*tracers):
    live = list(tracers)
    while live:
        for g in list(live):
            try:
                next(g)
            except StopIteration:
                live.remove(g)


def _ssd_kernel(fxc, fxp, fxn, fdt, bxc, bxp, bxn, bdt, epf_ref, elf_ref, epb_ref, elb_ref,
                cw_ref, cb_ref, dtb_ref, alog_ref, dskip_ref, yf_ref, yb_ref, xe_sc, st_sc, *, nc):
    c = pl.program_id(1)

    @pl.when(c == 0)
    def _():
        st_sc[...] = jnp.zeros(st_sc.shape, F32)

    shared = (cw_ref, cb_ref, dtb_ref, alog_ref, dskip_ref)
    _interleave(
        _ssd_direction(fxc, fxp, fxn, fdt, epf_ref, elf_ref, *shared, yf_ref, xe_sc.at[0], st_sc.at[0],
                       reverse=False, cc=c, nc=nc),
        _ssd_direction(bxc, bxp, bxn, bdt, epb_ref, elb_ref, *shared, yb_ref, xe_sc.at[1], st_sc.at[1],
                       reverse=True, cc=nc - 1 - c, nc=nc))


def _ssd(xbc, dt, conv_w, conv_b, dt_bias, a_log, d_skip):
    b, s, _ = xbc.shape
    L = SSD_CHUNK
    nc = s // L
    hb = L // SUBLANES
    nhb = s // SUBLANES
    full = lambda bi, c: (0, 0)

    def views(cidx):
        row = lambda bi, c: (bi, cidx(c), 0)
        return [pl.BlockSpec((1, L, CONV_DIM), row),
                pl.BlockSpec((1, SUBLANES, CONV_DIM), lambda bi, c: (bi, jnp.maximum(cidx(c) * hb - 1, 0), 0)),
                pl.BlockSpec((1, SUBLANES, CONV_DIM), lambda bi, c: (bi, jnp.minimum((cidx(c) + 1) * hb, nhb - 1), 0)),
                pl.BlockSpec((1, L, LANES), row)]

    spreads = [_spread_matrix(d0, SSD_HEADS, width, pieces)
               for d0 in (0, SSD_HEADS) for width, pieces in ((SSD_HEAD_DIM, 2), (L, 3))]
    return pl.pallas_call(
        functools.partial(_ssd_kernel, nc=nc),
        grid=(b, nc),
        in_specs=views(lambda c: c) + views(lambda c: nc - 1 - c) + [pl.BlockSpec(m.shape, full) for m in spreads] + [
            pl.BlockSpec((CONV_K, CONV_DIM), full), pl.BlockSpec((1, CONV_DIM), full),
            pl.BlockSpec((1, LANES), full), pl.BlockSpec((1, LANES), full), pl.BlockSpec((1, SSD_INNER), full)],
        out_specs=[pl.BlockSpec((1, L, SSD_INNER), lambda bi, c: (bi, c, 0)),
                   pl.BlockSpec((1, L, SSD_INNER), lambda bi, c: (bi, nc - 1 - c, 0))],
        out_shape=[jax.ShapeDtypeStruct((b, s, SSD_INNER), BF)] * 2,
        scratch_shapes=[pltpu.VMEM((2, L + 2 * SUBLANES, CONV_DIM), F32),
                        pltpu.VMEM((2, SSD_GROUPS, SSD_STATE, SSD_HPG * SSD_HEAD_DIM), F32)],
        compiler_params=_cparams("parallel", "arbitrary"),
        name="ssd",
    )(xbc, xbc, xbc, dt, xbc, xbc, xbc, dt, *spreads, conv_w, conv_b, dt_bias, a_log, d_skip)


def _to_pieces(ref, y):
    w = ref.shape[-1]
    for j in range(SC_ROW_SPLIT):
        ref[j] = y[:, j * w:(j + 1) * w].astype(ref.dtype)


def _from_pieces(ref):
    return jnp.concatenate([ref[j] for j in range(SC_ROW_SPLIT)], axis=1)


def _outproj_ln_kernel(*refs, even):
    if even:
        oa_ref, yf_ref, yb_ref, z_ref, nrm_ref = refs[:5]
        a = oa_ref[...]
        y_ssd = yf_ref[...].astype(F32) + yb_ref[...].astype(F32)
        b = _rms(y_ssd * _silu(z_ref[...].astype(F32))) * nrm_ref[...]
    else:
        of_ref, ob_ref, r_ref, gn_ref, sgu_ref = refs[:5]
        o = of_ref[...].astype(F32) + ob_ref[...].astype(F32)
        o = jnp.concatenate([_rms(o[:, h * GLA_VDIM:(h + 1) * GLA_VDIM]) for h in range(GLA_HEADS)], axis=1)
        a = o * gn_ref[...] * _silu(r_ref[...].astype(F32))
        b = sgu_ref[...]
    x_ref, wa_ref, wb_ref, g_ref, beta_ref, o_ref = refs[5:11]
    y = _dot(a.astype(BF), wa_ref[...]) + _dot(b.astype(BF), wb_ref[...])
    out = _layernorm(DN_ALPHA * x_ref[...] + y, g_ref[...], beta_ref[...])
    o_ref[...] = out
    for p_ref in refs[11:]:
        _to_pieces(p_ref, out)


def _outproj_ln(mix, x, wa, wb, g, beta, even):
    t, d = x.shape
    tm = min(ROW_TILE, t)
    row = lambda i: (i, 0)
    full = lambda i: (0, 0)
    out_specs = [pl.BlockSpec((tm, d), row)]
    out_shape = [jax.ShapeDtypeStruct((t, d), F32)]
    if not even:
        out_specs.append(pl.BlockSpec((SC_ROW_SPLIT, tm, d // SC_ROW_SPLIT), lambda i: (0, i, 0)))
        out_shape.append(jax.ShapeDtypeStruct((SC_ROW_SPLIT, t, d // SC_ROW_SPLIT), F32))
    mix_specs = [pl.BlockSpec((1, m.shape[1]), full) if m.shape[0] == 1 else pl.BlockSpec((tm, m.shape[1]), row)
                 for m in mix]
    res = pl.pallas_call(
        functools.partial(_outproj_ln_kernel, even=even),
        grid=(t // tm,),
        in_specs=mix_specs + [pl.BlockSpec((tm, d), row),
                              _resident(wa.shape, full), _resident(wb.shape, full),
                              pl.BlockSpec((1, d), full), pl.BlockSpec((1, d), full)],
        out_specs=out_specs,
        out_shape=out_shape,
        compiler_params=_cparams("parallel"),
        name="outproj_ln",
    )(*mix, x, wa, wb, g, beta)
    return res[0] if even else res


def _swiglu_acc(xb, wg_ref, wu_ref, wd_ref, acc_sc, nf):
    for f in range(nf):
        cols = slice(f * FF_CHUNK, (f + 1) * FF_CHUNK)
        h = _silu(_dot(xb, wg_ref[:, cols])) * _dot(xb, wu_ref[:, cols])
        part = _dot(h.astype(BF), wd_ref[cols, :])
        if f == 0:
            acc_sc[...] = part
        else:
            acc_sc[...] += part


def _ffn_ln_kernel(x_ref, wg_ref, wu_ref, wd_ref, g_ref, beta_ref, o_ref, acc_sc, *, nf):
    x = x_ref[...]
    _swiglu_acc(x.astype(BF), wg_ref, wu_ref, wd_ref, acc_sc, nf)
    o_ref[...] = _layernorm(DN_ALPHA * x + acc_sc[...], g_ref[...], beta_ref[...])


def _ffn_ln(x, wg, wu, wd, g, beta):
    t, d = x.shape
    f = wg.shape[1]
    tm = min(ROW_TILE, t)
    row = lambda i: (i, 0)
    full = lambda i: (0, 0)
    return pl.pallas_call(
        functools.partial(_ffn_ln_kernel, nf=f // FF_CHUNK),
        grid=(t // tm,),
        in_specs=[pl.BlockSpec((tm, d), row),
                  _resident((d, f), full), _resident((d, f), full), _resident((f, d), full),
                  pl.BlockSpec((1, d), full), pl.BlockSpec((1, d), full)],
        out_specs=pl.BlockSpec((tm, d), row),
        out_shape=jax.ShapeDtypeStruct((t, d), F32),
        scratch_shapes=[pltpu.VMEM((tm, d), F32)],
        compiler_params=_cparams("parallel"),
        name="ffn_ln",
    )(x, wg, wu, wd, g, beta)


def _moe_ffn_kernel(te_ref, tv_ref, xs_ref, wg_ref, wu_ref, wd_ref, o_ref, acc_sc, *, nf):
    @pl.when(tv_ref[pl.program_id(0)] > 0)
    def _():
        _swiglu_acc(_from_pieces(xs_ref).astype(BF), wg_ref, wu_ref, wd_ref, acc_sc, nf)
        _to_pieces(o_ref, acc_sc[...])


def _moe_ffn(xs, tile_expert, tile_valid, wg, wu, wd):
    ns, p, w = xs.shape
    d = ns * w
    f = wg.shape[2]
    tm = MOE_TILE
    row = lambda i, te, tv: (0, i, 0)
    grid_spec = pltpu.PrefetchScalarGridSpec(
        num_scalar_prefetch=2,
        grid=(p // tm,),
        in_specs=[pl.BlockSpec((ns, tm, w), row),
                  pl.BlockSpec((None, d, f), lambda i, te, tv: (te[i], 0, 0)),
                  pl.BlockSpec((None, d, f), lambda i, te, tv: (te[i], 0, 0)),
                  pl.BlockSpec((None, f, d), lambda i, te, tv: (te[i], 0, 0))],
        out_specs=pl.BlockSpec((ns, tm, w), row),
        scratch_shapes=[pltpu.VMEM((tm, d), F32)],
    )
    return pl.pallas_call(
        functools.partial(_moe_ffn_kernel, nf=f // FF_CHUNK),
        grid_spec=grid_spec,
        out_shape=jax.ShapeDtypeStruct((ns, p, w), F32),
        compiler_params=_cparams("arbitrary"),
        name="moe_ffn",
    )(tile_expert, tile_valid, xs, wg, wu, wd)


def _gla_direction(q_ref, k_ref, v_ref, gl_ref, w2_ref, gb_ref, o_ref, st_sc, *, reverse):
    L = GLA_CHUNK
    R = GLA_ROWS
    dk = GLA_KDIM
    dv = GLA_VDIM
    hk = GLA_HEADS * dk
    hv = GLA_HEADS * dv
    g2 = _split_bf16(gl_ref[0], 2)
    pre = _dot(jnp.concatenate([g2, g2[:, :LANES]], axis=1), w2_ref[...]) + gb_ref[...]
    yield
    lg = -_softplus(-pre) * (1.0 / GLA_TAU)
    row_i = lax.broadcasted_iota(jnp.int32, (R, R), 0)
    col_i = lax.broadcasted_iota(jnp.int32, (R, R), 1)
    intra = ((row_i // L) == (col_i // L)) & ((col_i >= row_i) if reverse else (col_i <= row_i))
    bc3 = _dot(intra.astype(BF), _split_bf16(lg, 3))
    yield
    bc = bc3[:, 0:hk] + bc3[:, hk:2 * hk] + bc3[:, 2 * hk:]
    mid = (L // 2 - 1) if reverse else (L // 2)
    end = 0 if reverse else (L - 1)
    ref_b = jnp.concatenate([jnp.broadcast_to(bc[ci * L + mid:ci * L + mid + 1], (L, hk))
                             for ci in range(R // L)], axis=0)
    end_b = jnp.concatenate([jnp.broadcast_to(bc[ci * L + end:ci * L + end + 1], (L, hk))
                             for ci in range(R // L)], axis=0)
    q = q_ref[0] * (dk ** -0.5)
    k = k_ref[0]
    qi = q * jnp.exp(bc - ref_b)
    ki = k * jnp.exp(ref_b - bc)
    qe = q * jnp.exp(bc)
    kd = k * jnp.exp(end_b - bc)
    first = 1 if reverse else 0
    in_first = (lax.broadcasted_iota(jnp.int32, (R, hk), 0) // L) == first
    d_first = jnp.exp(bc[first * L + end:first * L + end + 1])
    d_second = jnp.exp(bc[(1 - first) * L + end:(1 - first) * L + end + 1])
    qx = jnp.where(in_first, qe, qe * d_first).astype(BF)
    kx = jnp.where(in_first, kd * d_second, kd).astype(BF)
    qe_m = jnp.where(in_first, 0.0, qe)
    kd_m = jnp.where(in_first, kd, 0.0)
    yield

    a_heads = []
    for h in range(GLA_HEADS):
        kl = slice(h * dk, (h + 1) * dk)
        lhs = jnp.concatenate([qi[:, kl], qe_m[:, kl]], axis=0).astype(BF)
        rhs = jnp.concatenate([ki[:, kl], kd_m[:, kl]], axis=0).astype(BF)
        full = _dot_nt(lhs, rhs)
        a_heads.append((jnp.where(intra, full[0:R, 0:R], 0.0) + full[R:, R:]).astype(BF))
        yield
    vb = v_ref[0].astype(BF)
    lane_v = lax.broadcasted_iota(jnp.int32, (R, hv), 1)
    vbd = jnp.concatenate([jnp.where(lane_v // dv == h, vb, jnp.zeros_like(vb)) for h in range(GLA_HEADS)], axis=0)
    o = _dot(jnp.concatenate(a_heads, axis=1), vbd)
    yield
    st = st_sc[...]
    o_ref[0] = (o + _dot_nt(qx, st.astype(BF))).astype(o_ref.dtype)
    upd = _dot(v_ref[0].T.astype(BF), kx)
    on_diag = (lax.broadcasted_iota(jnp.int32, (hv, hk), 0) // dv) == (lax.broadcasted_iota(jnp.int32, (hv, hk), 1) // dk)
    st_sc[...] = st * (d_first * d_second) + jnp.where(on_diag, upd, 0.0)
    yield


def _gla_kernel(fq, fk, fv, fgl, bq, bk, bv, bgl, w2f_ref, gbf_ref, w2b_ref, gbb_ref, of_ref, ob_ref, st_sc):
    @pl.when(pl.program_id(1) == 0)
    def _():
        st_sc[...] = jnp.zeros(st_sc.shape, F32)

    _interleave(_gla_direction(fq, fk, fv, fgl, w2f_ref, gbf_ref, of_ref, st_sc.at[0], reverse=False),
                _gla_direction(bq, bk, bv, bgl, w2b_ref, gbb_ref, ob_ref, st_sc.at[1], reverse=True))


def _gla(q, k, v, gl, w2f, w2b, gbf, gbb):
    b, s, _ = q.shape
    R = GLA_ROWS
    nb = s // R
    hk = GLA_HEADS * GLA_KDIM
    hv = GLA_HEADS * GLA_VDIM
    full = lambda bi, c: (0, 0)

    def views(cidx):
        row = lambda bi, c: (bi, cidx(c), 0)
        return [pl.BlockSpec((1, R, hk), row), pl.BlockSpec((1, R, hk), row), pl.BlockSpec((1, R, hv), row),
                pl.BlockSpec((1, R, LANES), row)]

    return pl.pallas_call(
        _gla_kernel,
        grid=(b, nb),
        in_specs=views(lambda c: c) + views(lambda c: nb - 1 - c) + [
            pl.BlockSpec((3 * LANES, hk), full), pl.BlockSpec((1, hk), full),
            pl.BlockSpec((3 * LANES, hk), full), pl.BlockSpec((1, hk), full)],
        out_specs=[pl.BlockSpec((1, R, hv), lambda bi, c: (bi, c, 0)),
                   pl.BlockSpec((1, R, hv), lambda bi, c: (bi, nb - 1 - c, 0))],
        out_shape=[jax.ShapeDtypeStruct((b, s, hv), BF)] * 2,
        scratch_shapes=[pltpu.VMEM((2, hv, hk), F32)],
        compiler_params=_cparams("parallel", "arbitrary"),
        name="gla",
    )(q, k, v, gl, q, k, v, gl, w2f, gbf, w2b, gbb)


def _proj_odd_kernel(x_ref, w_ref, g_ref, b_ref, ws_ref, bias_ref, q_ref, k_ref, v_ref, r_ref, gl_ref, o_ref):
    xb = x_ref[...].astype(BF)
    off = _project_into(xb, w_ref, 0, (q_ref, k_ref, v_ref, r_ref, gl_ref))
    x = _dot(xb, w_ref[:, off:off + 2 * SGU_WIDTH])
    gel = x * (0.5 * (1.0 + jnp.tanh(math.sqrt(2.0 / math.pi) * (x + 0.044715 * (x * x * x)))))
    u = gel[:, :SGU_WIDTH]
    svn = _layernorm(gel[:, SGU_WIDTH:], g_ref[...], b_ref[...]).astype(BF)
    c = SGU_CHUNK
    for ci in range(x.shape[0] // c):
        rows = slice(ci * c, (ci + 1) * c)
        for gi in range(SGU_GROUPS):
            cols = slice(gi * SGU_GROUP_DIM, (gi + 1) * SGU_GROUP_DIM)
            sp = _dot(ws_ref[gi], svn[rows, cols]) + bias_ref[:, cols]
            o_ref[rows, cols] = (u[rows, cols] * sp).astype(o_ref.dtype)


def _proj_odd(x, w_in, ln_g, ln_b, ws_bf, bias_full):
    t, d = x.shape
    c = SGU_CHUNK
    tm = min(ROW_TILE, t)
    hk = GLA_HEADS * GLA_KDIM
    hv = GLA_HEADS * GLA_VDIM
    row = lambda i: (i, 0)
    full = lambda i: (0, 0)
    widths = (hk, hk, hv, hv, LANES, SGU_WIDTH)
    dtypes = (F32, F32, F32, BF, F32, BF)
    return pl.pallas_call(
        _proj_odd_kernel,
        grid=(t // tm,),
        in_specs=[pl.BlockSpec((tm, d), row), _resident(w_in.shape, full),
                  pl.BlockSpec((1, SGU_WIDTH), full), pl.BlockSpec((1, SGU_WIDTH), full),
                  pl.BlockSpec((SGU_GROUPS, c, c), lambda i: (0, 0, 0)),
                  pl.BlockSpec((c, SGU_WIDTH), full)],
        out_specs=[pl.BlockSpec((tm, n), row) for n in widths],
        out_shape=[jax.ShapeDtypeStruct((t, n), dt) for n, dt in zip(widths, dtypes)],
        compiler_params=_cparams("parallel"),
        name="proj_odd",
    )(x, w_in, ln_g, ln_b, ws_bf, bias_full)


def _router_kernel(x_ref, wr_ref, route_ref, cnt_ref, base_sc):
    @pl.when(pl.program_id(0) == 0)
    def _():
        base_sc[...] = jnp.zeros(base_sc.shape, F32)

    tr = x_ref.shape[0]
    x2 = _split_bf16(x_ref[...], 2)
    logits = _dot(jnp.concatenate([x2, x2[:, :x_ref.shape[1]]], axis=1), wr_ref[...])
    lane = lax.broadcasted_iota(jnp.int32, (tr, LANES), 1).astype(F32)
    lg = jnp.where(lane < N_EXPERTS, logits, -jnp.inf)
    m1 = jnp.max(lg, axis=1, keepdims=True)
    i1 = jnp.min(jnp.where(lg == m1, lane, float(LANES)), axis=1, keepdims=True)
    lg2 = jnp.where(lane == i1, -jnp.inf, lg)
    m2 = jnp.max(lg2, axis=1, keepdims=True)
    i2 = jnp.min(jnp.where(lg2 == m2, lane, float(LANES)), axis=1, keepdims=True)
    e = jnp.exp(m2 - m1)
    g1 = 1.0 / (1.0 + e)
    g2 = e / (1.0 + e)
    oh1 = (lane == i1).astype(F32)
    oh2 = (lane == i2).astype(F32)
    oh = oh1 + oh2
    row_i = lax.broadcasted_iota(jnp.int32, (tr, tr), 0)
    col_i = lax.broadcasted_iota(jnp.int32, (tr, tr), 1)
    before = _dot((col_i < row_i).astype(BF), oh.astype(BF)) + base_sc[...]
    r1 = jnp.sum(oh1 * before, axis=1, keepdims=True)
    r2 = jnp.sum(oh2 * before, axis=1, keepdims=True)
    base_sc[...] += jnp.sum(oh, axis=0, keepdims=True)
    route = jnp.zeros((tr, LANES), F32)
    for idx, val in enumerate((i1, i2, r1, r2, g1, g2)):
        route = jnp.where(lane == float(idx), val, route)
    route_ref[...] = route
    cnt_ref[...] = base_sc[...]


def _router(x, wr_pad):
    t, d = x.shape
    tr = min(ROUTE_TILE, t)
    return pl.pallas_call(
        _router_kernel,
        grid=(t // tr,),
        in_specs=[pl.BlockSpec((tr, d), lambda i: (i, 0)), pl.BlockSpec((3 * d, LANES), lambda i: (0, 0))],
        out_specs=[pl.BlockSpec((tr, LANES), lambda i: (i, 0)), pl.BlockSpec((1, LANES), lambda i: (0, 0))],
        out_shape=[jax.ShapeDtypeStruct((t, LANES), F32), jax.ShapeDtypeStruct((1, LANES), F32)],
        scratch_shapes=[pltpu.VMEM((1, LANES), F32)],
        compiler_params=_cparams("arbitrary"),
        name="router",
    )(x, wr_pad)


def _piece_indices(pos, n_rows):
    base = jnp.arange(SC_ROW_SPLIT, dtype=jnp.int32)[:, None, None] * n_rows
    return (base + pos.T[None]).reshape(1, -1)


def _sc_scatter_rows(xp, idx, n_out):
    ns, t, w = xp.shape
    nblk = t // SC_WINDOW
    per_piece = idx.shape[1] // ns // SC_WINDOW
    mesh = plsc.VectorSubcoreMesh(core_axis_name="c", subcore_axis_name="s")

    @pl.kernel(out_type=jax.ShapeDtypeStruct((ns * n_out, w), xp.dtype), mesh=mesh)
    def k(x_hbm, i_hbm, o_hbm):
        def body(x_vmem, i_vmem):
            pltpu.sync_copy(x_vmem, o_hbm.at[i_vmem.at[0]])

        pltpu.emit_pipeline(
            body,
            grid=(idx.shape[1] // SC_WINDOW,),
            in_specs=[pl.BlockSpec((SC_WINDOW, w), index_map=lambda i: ((i // per_piece) * nblk + i % nblk, 0)),
                      pl.BlockSpec((1, SC_WINDOW), index_map=lambda i: (0, i))],
            out_specs=[],
            core_axis_name=("c", "s"),
            dimension_semantics=(pltpu.PARALLEL,),
        )(x_hbm, i_hbm)

    return k(xp.reshape(ns * t, w), idx).reshape(ns, n_out, w)


def _sc_gather_rows(yp, idx):
    ns, n, w = yp.shape
    mesh = plsc.VectorSubcoreMesh(core_axis_name="c", subcore_axis_name="s")

    @pl.kernel(out_type=jax.ShapeDtypeStruct((idx.shape[1], w), yp.dtype), mesh=mesh)
    def k(x_hbm, i_hbm, o_hbm):
        def body(i_vmem, o_vmem):
            pltpu.sync_copy(x_hbm.at[i_vmem.at[0]], o_vmem)

        pltpu.emit_pipeline(
            body,
            grid=(idx.shape[1] // SC_WINDOW,),
            in_specs=[pl.BlockSpec((1, SC_WINDOW), index_map=lambda i: (0, i))],
            out_specs=[pl.BlockSpec((SC_WINDOW, w), index_map=lambda i: (i, 0))],
            core_axis_name=("c", "s"),
            dimension_semantics=(pltpu.PARALLEL,),
        )(i_hbm, o_hbm)

    return k(yp.reshape(ns * n, w), idx)


def _combine_ln_kernel(x_ref, y_ref, route_ref, g_ref, beta_ref, o_ref):
    g1 = route_ref[:, 4:5]
    g2 = route_ref[:, 5:6]
    y = (g1 * jnp.concatenate([y_ref[j, 0] for j in range(SC_ROW_SPLIT)], axis=1)
         + g2 * jnp.concatenate([y_ref[j, 1] for j in range(SC_ROW_SPLIT)], axis=1))
    o_ref[...] = _layernorm(DN_ALPHA * x_ref[...] + y, g_ref[...], beta_ref[...])


def _combine_ln(x, y2, route, g, beta):
    t, d = x.shape
    tm = min(ROW_TILE, t)
    row = lambda i: (i, 0)
    full = lambda i: (0, 0)
    return pl.pallas_call(
        _combine_ln_kernel,
        grid=(t // tm,),
        in_specs=[pl.BlockSpec((tm, d), row),
                  pl.BlockSpec((SC_ROW_SPLIT, 2, tm, d // SC_ROW_SPLIT), lambda i: (0, 0, i, 0)),
                  pl.BlockSpec((tm, LANES), row), pl.BlockSpec((1, d), full), pl.BlockSpec((1, d), full)],
        out_specs=pl.BlockSpec((tm, d), row),
        out_shape=jax.ShapeDtypeStruct((t, d), F32),
        compiler_params=_cparams("parallel"),
        name="moe_combine_ln",
    )(x, y2, route, g, beta)


def _pad_cols(w, n):
    return jnp.pad(w, ((0, 0), (0, n - w.shape[1])))


def _prep_even(p):
    (w_in, q_norm, w_uq, kv_norm, w_ukv, conv_w, conv_b, dt_bias, a_log, d_skip, ssm_norm, w_out,
     ln1_g, ln1_b, w_gate, w_up, w_down, ln2_g, ln2_b) = p
    o = 0
    cq = w_in[:, o:o + Q_RANK]; o += Q_RANK
    ckv = w_in[:, o:o + KV_RANK]; o += KV_RANK
    kr = w_in[:, o:o + QK_ROPE]; o += QK_ROPE
    z = w_in[:, o:o + SSD_INNER]; o += SSD_INNER
    xbc = w_in[:, o:o + CONV_DIM]; o += CONV_DIM
    dt = w_in[:, o:]
    half = QK_ROPE // 2
    zeros = lambda n: jnp.zeros((w_in.shape[0], n), F32)
    kra = jnp.concatenate([zeros(QK_NOPE), kr, zeros(LANES - QK_NOPE - QK_ROPE)], axis=1)
    krb = jnp.concatenate([zeros(QK_NOPE), -kr[:, half:], kr[:, :half], zeros(LANES - QK_NOPE - QK_ROPE)], axis=1)
    w_in_p = jnp.concatenate([cq, ckv, kra, krb, z, xbc, _pad_cols(dt, LANES)], axis=1).astype(BF)

    wq = w_uq.reshape(Q_RANK, MLA_HEADS, QK_NOPE + QK_ROPE)
    nope, rope = wq[..., :QK_NOPE], wq[..., QK_NOPE:]
    zq = lambda n: jnp.zeros((Q_RANK, MLA_HEADS, n), F32)
    wqa = jnp.concatenate([nope, rope, zq(LANES - QK_NOPE - QK_ROPE)], axis=-1)
    wqb = jnp.concatenate([zq(QK_NOPE), -rope[..., half:], rope[..., :half], zq(LANES - QK_NOPE - QK_ROPE)], axis=-1)
    wkv = w_ukv.reshape(KV_RANK, MLA_HEADS, QK_NOPE + V_DIM)
    zk = jnp.zeros((KV_RANK, MLA_HEADS, LANES - QK_NOPE), F32)
    wk = jnp.concatenate([wkv[..., :QK_NOPE], zk], axis=-1)
    vv_t = jnp.transpose(wkv[..., QK_NOPE:], (1, 2, 0))
    wvt = jnp.concatenate([vv_t, jnp.zeros((MLA_HEADS, V_ROWS - V_DIM, KV_RANK), F32)], axis=1)
    vadd = jnp.tile((jnp.arange(V_ROWS) == V_DIM).astype(F32), MLA_HEADS)[:, None]
    hw = MLA_HEADS * LANES
    return dict(
        w_in=w_in_p, q_norm=q_norm[None], kv_norm=kv_norm[None],
        wqa=wqa.reshape(Q_RANK, hw).astype(BF), wqb=wqb.reshape(Q_RANK, hw).astype(BF),
        wk=wk.reshape(KV_RANK, hw).astype(BF), wvt=wvt.reshape(MLA_HEADS * V_ROWS, KV_RANK).astype(BF), vadd=vadd,
        conv_w=conv_w, conv_b=conv_b[None],
        dt_bias=_pad_cols(dt_bias.reshape(1, -1), LANES), a_log=_pad_cols(a_log.reshape(1, -1), LANES),
        d_skip=jnp.repeat(d_skip, SSD_HEAD_DIM)[None], ssm_norm=ssm_norm[None],
        wo_a=w_out[:MLA_HEADS * V_DIM].astype(BF), wo_b=w_out[MLA_HEADS * V_DIM:].astype(BF),
        ln1_g=ln1_g[None], ln1_b=ln1_b[None],
        wg=w_gate.astype(BF), wu=w_up.astype(BF), wd=w_down.astype(BF),
        ln2_g=ln2_g[None], ln2_b=ln2_b[None])


def _prep_odd(p):
    (w_in, gate_w2, gate_b, gla_norm, sgu_ln_g, sgu_ln_b, w_s, b_s, w_out, ln1_g, ln1_b,
     w_router, we_gate, we_up, we_down, ln2_g, ln2_b) = p
    hk = GLA_HEADS * GLA_KDIM
    hv = GLA_HEADS * GLA_VDIM
    o = 2 * hk + 2 * hv
    gl = w_in[:, o:o + 2 * GLA_GATE_RANK]
    w_in_p = jnp.concatenate([w_in[:, :o], _pad_cols(gl, LANES), w_in[:, o + 2 * GLA_GATE_RANK:]], axis=1).astype(BF)
    zr = lambda n: jnp.zeros((n, hk), F32)
    def pieces(w):
        hi = w.astype(BF)
        lo = (w - hi.astype(F32)).astype(BF)
        return jnp.concatenate([hi, hi, lo], axis=0)

    w2f = pieces(jnp.concatenate([gate_w2[0], zr(LANES - GLA_GATE_RANK)], axis=0))
    w2b = pieces(jnp.concatenate([zr(GLA_GATE_RANK), gate_w2[1], zr(LANES - 2 * GLA_GATE_RANK)], axis=0))
    bias_full = jnp.repeat(b_s.T, SGU_GROUP_DIM, axis=1)
    return dict(
        w_in=w_in_p, w2f=w2f, w2b=w2b, gbf=gate_b[0][None], gbb=gate_b[1][None], gla_norm=gla_norm[None],
        sgu_g=sgu_ln_g[None], sgu_b=sgu_ln_b[None], ws=w_s.astype(BF), sgu_bias=bias_full,
        wo_a=w_out[:hv].astype(BF), wo_b=w_out[hv:].astype(BF), ln1_g=ln1_g[None], ln1_b=ln1_b[None],
        w_router=pieces(_pad_cols(w_router, LANES)),
        wg=we_gate.astype(BF), wu=we_up.astype(BF), wd=we_down.astype(BF),
        ln2_g=ln2_g[None], ln2_b=ln2_b[None])


def _rope_tables(s):
    half = QK_ROPE // 2
    inv = jnp.exp(-math.log(ROPE_THETA) * jnp.arange(half, dtype=F32) / half)
    ang = jnp.arange(s, dtype=F32)[:, None] * inv[None, :]
    cos, sin = jnp.cos(ang), jnp.sin(ang)
    pad = LANES - QK_NOPE - QK_ROPE
    cos_t = jnp.concatenate([jnp.ones((s, QK_NOPE), F32), cos, cos, jnp.ones((s, pad), F32)], axis=1)
    sin_t = jnp.concatenate([jnp.zeros((s, QK_NOPE), F32), sin, sin, jnp.zeros((s, pad), F32)], axis=1)
    return cos_t, sin_t


def _even_layer(x, w, b, s):
    t = b * s
    cos_t, sin_t = _rope_tables(s)
    q, k, vt, z, xbc, dt = _proj_even(x, w["w_in"], cos_t, sin_t, w["q_norm"], w["kv_norm"], w["wqa"], w["wqb"],
                                      w["wk"], w["wvt"], w["vadd"], b, s)
    hw = MLA_HEADS * LANES
    o_attn = _flash(q.reshape(b, s, hw), k.reshape(b, s, hw), vt)
    y_f, y_b = _ssd(xbc.reshape(b, s, CONV_DIM), dt.reshape(b, s, LANES),
                    w["conv_w"], w["conv_b"], w["dt_bias"], w["a_log"], w["d_skip"])
    mix = (o_attn.reshape(t, -1), y_f.reshape(t, -1), y_b.reshape(t, -1), z, w["ssm_norm"])
    x1 = _outproj_ln(mix, x, w["wo_a"], w["wo_b"], w["ln1_g"], w["ln1_b"], even=True)
    return _ffn_ln(x1, w["wg"], w["wu"], w["wd"], w["ln2_g"], w["ln2_b"])


def _moe(x1, x1p, w):
    t, d = x1.shape
    route, cnt = _router(x1, w["w_router"])
    eid = route[:, 0:2].astype(jnp.int32)
    rank = route[:, 2:4].astype(jnp.int32)
    counts = cnt[0, :N_EXPERTS].astype(jnp.int32)
    tm = MOE_TILE
    padded = ((counts + tm - 1) // tm) * tm
    ends = jnp.cumsum(padded)
    offs = ends - padded
    pos = offs[eid] + rank
    p_rows = 2 * t + N_EXPERTS * tm
    tiles = jnp.arange(p_rows // tm, dtype=jnp.int32)
    tile_ends = ends // tm
    tile_expert = jnp.minimum(jnp.sum(tiles[:, None] >= tile_ends[None, :], axis=1), N_EXPERTS - 1).astype(jnp.int32)
    tile_valid = (tiles < tile_ends[-1]).astype(jnp.int32)
    idx = _piece_indices(pos, p_rows)
    xs = _sc_scatter_rows(x1p, idx, p_rows)
    ys = _moe_ffn(xs, tile_expert, tile_valid, w["wg"], w["wu"], w["wd"])
    y2 = _sc_gather_rows(ys, idx).reshape(SC_ROW_SPLIT, 2, t, d // SC_ROW_SPLIT)
    return _combine_ln(x1, y2, route, w["ln2_g"], w["ln2_b"])


def _odd_layer(x, w, b, s):
    t = b * s
    hk = GLA_HEADS * GLA_KDIM
    hv = GLA_HEADS * GLA_VDIM
    q, k, v, r, gl, o_sgu = _proj_odd(x, w["w_in"], w["sgu_g"], w["sgu_b"], w["ws"], w["sgu_bias"])
    o_f, o_b = _gla(q.reshape(b, s, hk), k.reshape(b, s, hk), v.reshape(b, s, hv), gl.reshape(b, s, LANES),
                    w["w2f"], w["w2b"], w["gbf"], w["gbb"])
    mix = (o_f.reshape(t, hv), o_b.reshape(t, hv), r, w["gla_norm"], o_sgu)
    x1, x1p = _outproj_ln(mix, x, w["wo_a"], w["wo_b"], w["ln1_g"], w["ln1_b"], even=False)
    return _moe(x1, x1p, w)


def _trunk(x, ev_w, od_w):
    b, s, d = x.shape
    x = x.reshape(b * s, d)
    for i in range(DEPTH):
        if i % 2 == 0:
            x = _even_layer(x, ev_w[i // 2], b, s)
        else:
            x = _odd_layer(x, od_w[i // 2], b, s)
    return x.reshape(b, s, d)


def kernel(x_prompt, x_sample, ev_w_in, ev_q_norm, ev_w_uq, ev_kv_norm, ev_w_ukv, ev_conv_w, ev_conv_b, ev_dt_bias, ev_a_log, ev_d_skip, ev_ssm_norm, ev_w_out, ev_ln1_g, ev_ln1_b, ev_w_gate, ev_w_up, ev_w_down, ev_ln2_g, ev_ln2_b, od_w_in, od_gate_w2, od_gate_b, od_gla_norm, od_sgu_ln_g, od_sgu_ln_b, od_w_s, od_b_s, od_w_out, od_ln1_g, od_ln1_b, od_w_router, od_we_gate, od_we_up, od_we_down, od_ln2_g, od_ln2_b):
    ev = (ev_w_in, ev_q_norm, ev_w_uq, ev_kv_norm, ev_w_ukv, ev_conv_w, ev_conv_b, ev_dt_bias,
          ev_a_log, ev_d_skip, ev_ssm_norm, ev_w_out, ev_ln1_g, ev_ln1_b, ev_w_gate, ev_w_up,
          ev_w_down, ev_ln2_g, ev_ln2_b)
    od = (od_w_in, od_gate_w2, od_gate_b, od_gla_norm, od_sgu_ln_g, od_sgu_ln_b, od_w_s, od_b_s,
          od_w_out, od_ln1_g, od_ln1_b, od_w_router, od_we_gate, od_we_up, od_we_down,
          od_ln2_g, od_ln2_b)
    ev_w = [_prep_even(tuple(t[i] for t in ev)) for i in range(ev_w_in.shape[0])]
    od_w = [_prep_odd(tuple(t[i] for t in od)) for i in range(od_w_in.shape[0])]
    return (_trunk(x_prompt, ev_w, od_w), _trunk(x_sample, ev_w, od_w))
```

```python
import functools
import math

import jax
import jax.numpy as jnp
from jax import lax
from jax.experimental import pallas as pl
from jax.experimental.pallas import tpu as pltpu
from jax.experimental.pallas import tpu_sc as plsc

BF = jnp.bfloat16
F32 = jnp.float32

D_MODEL = 1024
DEPTH = 4
MLA_HEADS = 8
QK_NOPE = 64
QK_ROPE = 32
V_DIM = 64
Q_RANK = 256
KV_RANK = 128
ROPE_THETA = 10000.0
SSD_HEADS = 8
SSD_HEAD_DIM = 64
SSD_GROUPS = 2
SSD_STATE = 64
SSD_CHUNK = 128
CONV_K = 5
SSD_INNER = SSD_HEADS * SSD_HEAD_DIM
SSD_HPG = SSD_HEADS // SSD_GROUPS
CONV_DIM = SSD_INNER + 2 * SSD_GROUPS * SSD_STATE
GLA_HEADS = 4
GLA_KDIM = 64
GLA_VDIM = 128
GLA_GATE_RANK = 16
GLA_TAU = 16.0
GLA_CHUNK = 64
SGU_GROUPS = 4
SGU_CHUNK = 128
SGU_GROUP_DIM = 128
SGU_WIDTH = SGU_GROUPS * SGU_GROUP_DIM
D_FF = 2816
N_EXPERTS = 8
D_FF_EXPERT = 3584
DN_ALPHA = (2 * DEPTH) ** 0.25
EPS = 1e-5

LANES = 128
SUBLANES = 8
VMEM_LIMIT = 56 * 1024 * 1024
SC_WINDOW = 128
SC_ROW_SPLIT = 4

ROW_TILE = 512
NORM_TILE = 1024
FF_CHUNK = 256
ATT_TQ = 512
ATT_TK = 512
MOE_TILE = 512
ROUTE_TILE = 512
GLA_ROWS = 2 * GLA_CHUNK
V_ROWS = 80


def _cparams(*sem):
    return pltpu.CompilerParams(dimension_semantics=sem, vmem_limit_bytes=VMEM_LIMIT)


def _resident(shape, index_map):
    return pl.BlockSpec(shape, index_map, pipeline_mode=pl.Buffered(1))


def _rms(x):
    return x * lax.rsqrt(jnp.mean(x * x, axis=-1, keepdims=True) + EPS)


def _layernorm(x, g, b):
    mu = jnp.mean(x, axis=-1, keepdims=True)
    xc = x - mu
    var = jnp.mean(xc * xc, axis=-1, keepdims=True)
    return xc * lax.rsqrt(var + EPS) * g + b


def _silu(x):
    return x * jax.nn.sigmoid(x)


def _softplus(x):
    return jnp.maximum(x, 0.0) + jnp.log1p(jnp.exp(-jnp.abs(x)))


def _dot(a, b):
    return jnp.dot(a, b, preferred_element_type=F32)


def _dot_nt(a, b):
    return lax.dot_general(a, b, (((1,), (1,)), ((), ())), preferred_element_type=F32)


def _project_into(xb, w_ref, off, o_refs):
    for o_ref in o_refs:
        n = o_ref.shape[1]
        o_ref[...] = _dot(xb, w_ref[:, off:off + n]).astype(o_ref.dtype)
        off += n
    return off


MLA_IN = Q_RANK + KV_RANK + 2 * LANES


def _proj_even_kernel(x_ref, w_ref, cos_ref, sin_ref, qn_ref, kvn_ref, wqa_ref, wqb_ref, wk_ref, wvt_ref, vadd_ref,
                      q_ref, k_ref, vt_ref, z_ref, xbc_ref, dt_ref):
    xb = x_ref[...].astype(BF)
    _project_into(xb, w_ref, MLA_IN, (z_ref, xbc_ref, dt_ref))
    m = _dot(xb, w_ref[:, 0:MLA_IN])
    cq = m[:, 0:Q_RANK]
    ckv = m[:, Q_RANK:Q_RANK + KV_RANK]
    kra = m[:, Q_RANK + KV_RANK:Q_RANK + KV_RANK + LANES]
    krb = m[:, Q_RANK + KV_RANK + LANES:Q_RANK + KV_RANK + 2 * LANES]
    cos = cos_ref[...]
    sin = sin_ref[...]
    cos8 = jnp.concatenate([cos] * MLA_HEADS, axis=1)
    sin8 = jnp.concatenate([sin] * MLA_HEADS, axis=1)
    cqn = (_rms(cq) * qn_ref[...]).astype(BF)
    q = _dot(cqn, wqa_ref[...]) * cos8 + _dot(cqn, wqb_ref[...]) * sin8
    q_ref[...] = (q * ((QK_NOPE + QK_ROPE) ** -0.5 * math.log2(math.e))).astype(BF)
    ckvn = (_rms(ckv) * kvn_ref[...]).astype(BF)
    kr = kra * cos + krb * sin
    k = _dot(ckvn, wk_ref[...]) + jnp.concatenate([kr] * MLA_HEADS, axis=1)
    k_ref[...] = k.astype(BF)
    vt_ref[0] = (_dot_nt(wvt_ref[...], ckvn) + vadd_ref[...]).astype(BF)


def _proj_even(x, w_in, cos_t, sin_t, q_norm, kv_norm, wqa, wqb, wk, wvt, vadd, batch, seq):
    t, d = x.shape
    tm = min(ROW_TILE, seq)
    nseq = seq // tm
    hw = MLA_HEADS * LANES
    vr = MLA_HEADS * V_ROWS
    full = lambda i: (0, 0)
    row = lambda i: (i, 0)
    return pl.pallas_call(
        _proj_even_kernel,
        grid=(t // tm,),
        in_specs=[pl.BlockSpec((tm, d), row), _resident(w_in.shape, full),
                  pl.BlockSpec((tm, LANES), lambda i: (i % nseq, 0)),
                  pl.BlockSpec((tm, LANES), lambda i: (i % nseq, 0)),
                  pl.BlockSpec((1, Q_RANK), full), pl.BlockSpec((1, KV_RANK), full),
                  pl.BlockSpec((Q_RANK, hw), full), pl.BlockSpec((Q_RANK, hw), full),
                  pl.BlockSpec((KV_RANK, hw), full), pl.BlockSpec((vr, KV_RANK), full),
                  pl.BlockSpec((vr, 1), full)],
        out_specs=[pl.BlockSpec((tm, hw), row), pl.BlockSpec((tm, hw), row),
                   pl.BlockSpec((1, vr, tm), lambda i: (i // nseq, 0, i % nseq)),
                   pl.BlockSpec((tm, SSD_INNER), row), pl.BlockSpec((tm, CONV_DIM), row), pl.BlockSpec((tm, LANES), row)],
        out_shape=[jax.ShapeDtypeStruct((t, hw), BF), jax.ShapeDtypeStruct((t, hw), BF),
                   jax.ShapeDtypeStruct((batch, vr, seq), BF),
                   jax.ShapeDtypeStruct((t, SSD_INNER), BF), jax.ShapeDtypeStruct((t, CONV_DIM), F32),
                   jax.ShapeDtypeStruct((t, LANES), F32)],
        compiler_params=_cparams("parallel"),
        name="proj_even",
    )(x, w_in, cos_t, sin_t, q_norm, kv_norm, wqa, wqb, wk, wvt, vadd)


def _flash_kernel(q_ref, k_ref, vt_ref, o_ref, m_sc, acc_sc, s_sc, *, tk, nk, unroll):
    for h in range(2):
        m_sc[h] = jnp.full(m_sc.shape[1:], -jnp.inf, F32)
        acc_sc[h] = jnp.zeros(acc_sc.shape[1:], F32)

    def scores(j, slot):
        off = pl.multiple_of(j * tk, tk)
        for h in range(2):
            lanes = slice(h * LANES, (h + 1) * LANES)
            s_sc[slot, h] = _dot_nt(k_ref[0, pl.ds(off, tk), lanes], q_ref[0, :, lanes])

    def consume(j, slot):
        off = pl.multiple_of(j * tk, tk)
        for h in range(2):
            st = s_sc[slot, h]
            m_prev = m_sc[h]
            m_new = jnp.maximum(m_prev, jnp.max(st, axis=0, keepdims=True))
            p = jnp.exp2(st - m_new[0:1, :]).astype(BF)
            alpha = jnp.exp2(m_prev - m_new)
            pv = _dot(vt_ref[0, h * V_ROWS:(h + 1) * V_ROWS, pl.ds(off, tk)], p)
            acc_sc[h] = alpha[0:1, :] * acc_sc[h] + pv
            m_sc[h] = m_new

    scores(0, 0)

    def body(jj, carry):
        j = unroll * jj
        for u in range(unroll):
            scores(jnp.minimum(j + u + 1, nk - 1), (u + 1) % 2)
            consume(j + u, u % 2)
        return carry

    lax.fori_loop(0, nk // unroll, body, 0)
    outs = []
    for h in range(2):
        acc = acc_sc[h]
        outs.append(acc[0:V_DIM, :] / acc[V_DIM:V_DIM + 1, :])
    o_ref[0] = jnp.concatenate(outs, axis=0).T.astype(o_ref.dtype)


def _flash(q, k, vt):
    b, s, hw = q.shape
    tq = min(ATT_TQ, s)
    tk = min(ATT_TK, s)
    nk = s // tk
    unroll = next(u for u in (8, 4, 2, 1) if nk % u == 0)
    pairs = MLA_HEADS // 2
    return pl.pallas_call(
        functools.partial(_flash_kernel, tk=tk, nk=nk, unroll=unroll),
        grid=(b, pairs, s // tq),
        in_specs=[pl.BlockSpec((1, tq, 2 * LANES), lambda bi, hp, i: (bi, i, hp)),
                  pl.BlockSpec((1, s, 2 * LANES), lambda bi, hp, i: (bi, 0, hp)),
                  pl.BlockSpec((1, 2 * V_ROWS, s), lambda bi, hp, i: (bi, hp, 0))],
        out_specs=pl.BlockSpec((1, tq, LANES), lambda bi, hp, i: (bi, i, hp)),
        out_shape=jax.ShapeDtypeStruct((b, s, MLA_HEADS * V_DIM), BF),
        scratch_shapes=[pltpu.VMEM((2, SUBLANES, tq), F32), pltpu.VMEM((2, V_ROWS, tq), F32),
                        pltpu.VMEM((2, 2, tk, tq), F32)],
        compiler_params=_cparams("parallel", "parallel", "arbitrary"),
        name="mla_flash",
    )(q, k, vt)


def _split_bf16(x, pieces):
    out = []
    for _ in range(pieces):
        p = x.astype(BF)
        out.append(p)
        x = x - p.astype(F32)
    return jnp.concatenate(out, axis=1)


def _spread_matrix(first_lane, heads, width, pieces):
    src = jnp.arange(LANES)[:, None] - first_lane
    dst = jnp.arange(heads * width)[None, :] // width
    return jnp.tile((src == dst).astype(BF), (pieces, 1))


def _ssd_direction(xc_ref, xp_ref, xn_ref, dt_ref, ep_ref, el_ref, cw_ref, cb_ref, dtb_ref, alog_ref, dskip_ref,
                   o_ref, xe_sc, st_sc, *, reverse, cc, nc):
    L = SSD_CHUNK
    N = SSD_STATE
    P = SSD_HEAD_DIM
    xe_sc[0:SUBLANES, :] = jnp.where(cc > 0, xp_ref[0], 0.0)
    xe_sc[SUBLANES:SUBLANES + L, :] = xc_ref[0]
    xe_sc[SUBLANES + L:2 * SUBLANES + L, :] = jnp.where(cc < nc - 1, xn_ref[0], 0.0)
    conv = cb_ref[...] + cw_ref[0:1, :] * xe_sc[pl.ds(SUBLANES - CONV_K // 2, L), :]
    for j in range(1, CONV_K):
        conv = conv + cw_ref[j:j + 1, :] * xe_sc[pl.ds(SUBLANES - CONV_K // 2 + j, L), :]
    yield
    xbc = _silu(conv)
    xs = xbc[:, :SSD_INNER]
    bc = xbc[:, SSD_INNER:]
    bc_t = bc.T

    lane = lax.broadcasted_iota(jnp.int32, (L, LANES), 1)
    dtv = _softplus(dt_ref[0] + dtb_ref[...])
    a = jnp.where(lane[0:1] < 2 * SSD_HEADS, -jnp.exp(alog_ref[...]), 0.0)
    dta = dtv * a
    row_i = lax.broadcasted_iota(jnp.int32, (L, L), 0)
    col_i = lax.broadcasted_iota(jnp.int32, (L, L), 1)
    causal = (col_i >= row_i) if reverse else (col_i <= row_i)
    acs3 = _dot(causal.astype(BF), _split_bf16(dta, 3))
    yield
    acs = acs3[:, 0:LANES] + acs3[:, LANES:2 * LANES] + acs3[:, 2 * LANES:]
    acs_t = acs.T
    d0 = SSD_HEADS if reverse else 0
    end = 0 if reverse else L - 1
    tot = acs[end:end + 1, :]
    yield

    stacked = jnp.concatenate([dtv, jnp.exp(acs), jnp.exp(tot - acs),
                               jnp.broadcast_to(jnp.exp(tot), (2 * SUBLANES, LANES))], axis=0)
    spread = _dot(_split_bf16(stacked, 2), ep_ref[...])
    dt_x = spread[0:L]
    ea_x = spread[L:2 * L]
    eb_x = spread[2 * L:3 * L]
    et_x = spread[3 * L:3 * L + 1]
    col_x = _dot(_split_bf16(acs, 3), el_ref[...])
    yield
    row_x = jnp.concatenate([jnp.broadcast_to(acs_t[d0 + h:d0 + h + 1, :], (L, L)) for h in range(SSD_HEADS)], axis=1)
    causal_x = jnp.concatenate([causal] * SSD_HEADS, axis=1)
    decay_x = jnp.exp(jnp.where(causal_x, col_x - row_x, -jnp.inf))
    xdt = xs * dt_x
    xdt_b = xdt.astype(BF)
    xw_b = (xdt * eb_x).astype(BF)
    gw = SSD_HPG * P
    lane_g = lax.broadcasted_iota(jnp.int32, (L, gw), 1)
    yield

    ys = []
    for g in range(SSD_GROUPS):
        bm_g = bc[:, g * N:(g + 1) * N].astype(BF)
        cm_g = bc[:, (SSD_GROUPS + g) * N:(SSD_GROUPS + g + 1) * N].astype(BF)
        bm_t_g = bc_t[g * N:(g + 1) * N, :].astype(BF)
        cb = _dot_nt(cm_g, bm_g)
        m_g = (jnp.concatenate([cb] * SSD_HPG, axis=1) * decay_x[:, g * SSD_HPG * L:(g + 1) * SSD_HPG * L]).astype(BF)
        xg = xdt_b[:, g * gw:(g + 1) * gw]
        xbd = jnp.concatenate([jnp.where(lane_g // P == j, xg, jnp.zeros_like(xg)) for j in range(SSD_HPG)], axis=0)
        y_diag = _dot(m_g, xbd)
        states_t = _dot(bm_t_g, xw_b[:, g * gw:(g + 1) * gw])
        prev_t = st_sc[g]
        y_off = _dot(cm_g, prev_t.astype(BF)) * ea_x[:, g * gw:(g + 1) * gw]
        st_sc[g] = prev_t * et_x[:, g * gw:(g + 1) * gw] + states_t
        ys.append(y_diag + y_off)
        yield
    y = jnp.concatenate(ys, axis=1)
    if not reverse:
        y = y + dskip_ref[...] * xs
    o_ref[0] = y.astype(o_ref.dtype)


def _interleave(*tracers):
    live = list(tracers)
    while live:
        for g in list(live):
            try:
                next(g)
            except StopIteration:
                live.remove(g)


def _ssd_kernel(fxc, fxp, fxn, fdt, bxc, bxp, bxn, bdt, epf_ref, elf_ref, epb_ref, elb_ref,
                cw_ref, cb_ref, dtb_ref, alog_ref, dskip_ref, yf_ref, yb_ref, xe_sc, st_sc, *, nc):
    c = pl.program_id(1)

    @pl.when(c == 0)
    def _():
        st_sc[...] = jnp.zeros(st_sc.shape, F32)

    shared = (cw_ref, cb_ref, dtb_ref, alog_ref, dskip_ref)
    _interleave(
        _ssd_direction(fxc, fxp, fxn, fdt, epf_ref, elf_ref, *shared, yf_ref, xe_sc.at[0], st_sc.at[0],
                       reverse=False, cc=c, nc=nc),
        _ssd_direction(bxc, bxp, bxn, bdt, epb_ref, elb_ref, *shared, yb_ref, xe_sc.at[1], st_sc.at[1],
                       reverse=True, cc=nc - 1 - c, nc=nc))


def _ssd(xbc, dt, conv_w, conv_b, dt_bias, a_log, d_skip):
    b, s, _ = xbc.shape
    L = SSD_CHUNK
    nc = s // L
    hb = L // SUBLANES
    nhb = s // SUBLANES
    full = lambda bi, c: (0, 0)

    def views(cidx):
        row = lambda bi, c: (bi, cidx(c), 0)
        return [pl.BlockSpec((1, L, CONV_DIM), row),
                pl.BlockSpec((1, SUBLANES, CONV_DIM), lambda bi, c: (bi, jnp.maximum(cidx(c) * hb - 1, 0), 0)),
                pl.BlockSpec((1, SUBLANES, CONV_DIM), lambda bi, c: (bi, jnp.minimum((cidx(c) + 1) * hb, nhb - 1), 0)),
                pl.BlockSpec((1, L, LANES), row)]

    spreads = [_spread_matrix(d0, SSD_HEADS, width, pieces)
               for d0 in (0, SSD_HEADS) for width, pieces in ((SSD_HEAD_DIM, 2), (L, 3))]
    return pl.pallas_call(
        functools.partial(_ssd_kernel, nc=nc),
        grid=(b, nc),
        in_specs=views(lambda c: c) + views(lambda c: nc - 1 - c) + [pl.BlockSpec(m.shape, full) for m in spreads] + [
            pl.BlockSpec((CONV_K, CONV_DIM), full), pl.BlockSpec((1, CONV_DIM), full),
            pl.BlockSpec((1, LANES), full), pl.BlockSpec((1, LANES), full), pl.BlockSpec((1, SSD_INNER), full)],
        out_specs=[pl.BlockSpec((1, L, SSD_INNER), lambda bi, c: (bi, c, 0)),
                   pl.BlockSpec((1, L, SSD_INNER), lambda bi, c: (bi, nc - 1 - c, 0))],
        out_shape=[jax.ShapeDtypeStruct((b, s, SSD_INNER), BF)] * 2,
        scratch_shapes=[pltpu.VMEM((2, L + 2 * SUBLANES, CONV_DIM), F32),
                        pltpu.VMEM((2, SSD_GROUPS, SSD_STATE, SSD_HPG * SSD_HEAD_DIM), F32)],
        compiler_params=_cparams("parallel", "arbitrary"),
        name="ssd",
    )(xbc, xbc, xbc, dt, xbc, xbc, xbc, dt, *spreads, conv_w, conv_b, dt_bias, a_log, d_skip)


def _to_pieces(ref, y):
    w = ref.shape[-1]
    for j in range(SC_ROW_SPLIT):
        ref[j] = y[:, j * w:(j + 1) * w].astype(ref.dtype)


def _from_pieces(ref):
    return jnp.concatenate([ref[j] for j in range(SC_ROW_SPLIT)], axis=1)


def _outproj_ln_kernel(*refs, even):
    if even:
        oa_ref, yf_ref, yb_ref, z_ref, nrm_ref = refs[:5]
        a = oa_ref[...]
        y_ssd = yf_ref[...].astype(F32) + yb_ref[...].astype(F32)
        b = _rms(y_ssd * _silu(z_ref[...].astype(F32))) * nrm_ref[...]
    else:
        of_ref, ob_ref, r_ref, gn_ref, sgu_ref = refs[:5]
        o = of_ref[...].astype(F32) + ob_ref[...].astype(F32)
        o = jnp.concatenate([_rms(o[:, h * GLA_VDIM:(h + 1) * GLA_VDIM]) for h in range(GLA_HEADS)], axis=1)
        a = o * gn_ref[...] * _silu(r_ref[...].astype(F32))
        b = sgu_ref[...]
    x_ref, wa_ref, wb_ref, g_ref, beta_ref, o_ref = refs[5:11]
    y = _dot(a.astype(BF), wa_ref[...]) + _dot(b.astype(BF), wb_ref[...])
    out = _layernorm(DN_ALPHA * x_ref[...] + y, g_ref[...], beta_ref[...])
    o_ref[...] = out
    for p_ref in refs[11:]:
        _to_pieces(p_ref, out)


def _outproj_ln(mix, x, wa, wb, g, beta, even):
    t, d = x.shape
    tm = min(NORM_TILE, t)
    row = lambda i: (i, 0)
    full = lambda i: (0, 0)
    out_specs = [pl.BlockSpec((tm, d), row)]
    out_shape = [jax.ShapeDtypeStruct((t, d), F32)]
    if not even:
        out_specs.append(pl.BlockSpec((SC_ROW_SPLIT, tm, d // SC_ROW_SPLIT), lambda i: (0, i, 0)))
        out_shape.append(jax.ShapeDtypeStruct((SC_ROW_SPLIT, t, d // SC_ROW_SPLIT), F32))
    mix_specs = [pl.BlockSpec((1, m.shape[1]), full) if m.shape[0] == 1 else pl.BlockSpec((tm, m.shape[1]), row)
                 for m in mix]
    res = pl.pallas_call(
        functools.partial(_outproj_ln_kernel, even=even),
        grid=(t // tm,),
        in_specs=mix_specs + [pl.BlockSpec((tm, d), row),
                              _resident(wa.shape, full), _resident(wb.shape, full),
                              pl.BlockSpec((1, d), full), pl.BlockSpec((1, d), full)],
        out_specs=out_specs,
        out_shape=out_shape,
        compiler_params=_cparams("parallel"),
        name="outproj_ln",
    )(*mix, x, wa, wb, g, beta)
    return res[0] if even else res


def _swiglu_acc(xb, wg_ref, wu_ref, wd_ref, acc_sc, nf):
    for f in range(nf):
        cols = slice(f * FF_CHUNK, (f + 1) * FF_CHUNK)
        h = _silu(_dot(xb, wg_ref[:, cols])) * _dot(xb, wu_ref[:, cols])
        part = _dot(h.astype(BF), wd_ref[cols, :])
        if f == 0:
            acc_sc[...] = part
        else:
            acc_sc[...] += part


def _ffn_ln_kernel(x_ref, wg_ref, wu_ref, wd_ref, g_ref, beta_ref, o_ref, acc_sc, *, nf):
    x = x_ref[...]
    _swiglu_acc(x.astype(BF), wg_ref, wu_ref, wd_ref, acc_sc, nf)
    o_ref[...] = _layernorm(DN_ALPHA * x + acc_sc[...], g_ref[...], beta_ref[...])


def _ffn_ln(x, wg, wu, wd, g, beta):
    t, d = x.shape
    f = wg.shape[1]
    tm = min(ROW_TILE, t)
    row = lambda i: (i, 0)
    full = lambda i: (0, 0)
    return pl.pallas_call(
        functools.partial(_ffn_ln_kernel, nf=f // FF_CHUNK),
        grid=(t // tm,),
        in_specs=[pl.BlockSpec((tm, d), row),
                  _resident((d, f), full), _resident((d, f), full), _resident((f, d), full),
                  pl.BlockSpec((1, d), full), pl.BlockSpec((1, d), full)],
        out_specs=pl.BlockSpec((tm, d), row),
        out_shape=jax.ShapeDtypeStruct((t, d), F32),
        scratch_shapes=[pltpu.VMEM((tm, d), F32)],
        compiler_params=_cparams("parallel"),
        name="ffn_ln",
    )(x, wg, wu, wd, g, beta)


def _moe_ffn_kernel(te_ref, tv_ref, xs_ref, wg_ref, wu_ref, wd_ref, o_ref, acc_sc, *, nf):
    @pl.when(tv_ref[pl.program_id(0)] > 0)
    def _():
        _swiglu_acc(_from_pieces(xs_ref).astype(BF), wg_ref, wu_ref, wd_ref, acc_sc, nf)
        _to_pieces(o_ref, acc_sc[...])


def _moe_ffn(xs, tile_expert, tile_valid, wg, wu, wd):
    ns, p, w = xs.shape
    d = ns * w
    f = wg.shape[2]
    tm = MOE_TILE
    row = lambda i, te, tv: (0, i, 0)
    grid_spec = pltpu.PrefetchScalarGridSpec(
        num_scalar_prefetch=2,
        grid=(p // tm,),
        in_specs=[pl.BlockSpec((ns, tm, w), row),
                  pl.BlockSpec((None, d, f), lambda i, te, tv: (te[i], 0, 0)),
                  pl.BlockSpec((None, d, f), lambda i, te, tv: (te[i], 0, 0)),
                  pl.BlockSpec((None, f, d), lambda i, te, tv: (te[i], 0, 0))],
        out_specs=pl.BlockSpec((ns, tm, w), row),
        scratch_shapes=[pltpu.VMEM((tm, d), F32)],
    )
    return pl.pallas_call(
        functools.partial(_moe_ffn_kernel, nf=f // FF_CHUNK),
        grid_spec=grid_spec,
        out_shape=jax.ShapeDtypeStruct((ns, p, w), F32),
        compiler_params=_cparams("arbitrary"),
        name="moe_ffn",
    )(tile_expert, tile_valid, xs, wg, wu, wd)


def _gla_direction(q_ref, k_ref, v_ref, gl_ref, w2_ref, gb_ref, o_ref, st_sc, *, reverse):
    L = GLA_CHUNK
    R = GLA_ROWS
    dk = GLA_KDIM
    dv = GLA_VDIM
    hk = GLA_HEADS * dk
    hv = GLA_HEADS * dv
    g2 = _split_bf16(gl_ref[0], 2)
    pre = _dot(jnp.concatenate([g2, g2[:, :LANES]], axis=1), w2_ref[...]) + gb_ref[...]
    yield
    lg = -_softplus(-pre) * (1.0 / GLA_TAU)
    row_i = lax.broadcasted_iota(jnp.int32, (R, R), 0)
    col_i = lax.broadcasted_iota(jnp.int32, (R, R), 1)
    intra = ((row_i // L) == (col_i // L)) & ((col_i >= row_i) if reverse else (col_i <= row_i))
    bc3 = _dot(intra.astype(BF), _split_bf16(lg, 3))
    yield
    bc = bc3[:, 0:hk] + bc3[:, hk:2 * hk] + bc3[:, 2 * hk:]
    mid = (L // 2 - 1) if reverse else (L // 2)
    end = 0 if reverse else (L - 1)
    ref_b = jnp.concatenate([jnp.broadcast_to(bc[ci * L + mid:ci * L + mid + 1], (L, hk))
                             for ci in range(R // L)], axis=0)
    end_b = jnp.concatenate([jnp.broadcast_to(bc[ci * L + end:ci * L + end + 1], (L, hk))
                             for ci in range(R // L)], axis=0)
    q = q_ref[0] * (dk ** -0.5)
    k = k_ref[0]
    qi = q * jnp.exp(bc - ref_b)
    ki = k * jnp.exp(ref_b - bc)
    qe = q * jnp.exp(bc)
    kd = k * jnp.exp(end_b - bc)
    first = 1 if reverse else 0
    in_first = (lax.broadcasted_iota(jnp.int32, (R, hk), 0) // L) == first
    d_first = jnp.exp(bc[first * L + end:first * L + end + 1])
    d_second = jnp.exp(bc[(1 - first) * L + end:(1 - first) * L + end + 1])
    qx = jnp.where(in_first, qe, qe * d_first).astype(BF)
    kx = jnp.where(in_first, kd * d_second, kd).astype(BF)
    qe_m = jnp.where(in_first, 0.0, qe)
    kd_m = jnp.where(in_first, kd, 0.0)
    yield

    a_heads = []
    for h in range(GLA_HEADS):
        kl = slice(h * dk, (h + 1) * dk)
        lhs = jnp.concatenate([qi[:, kl], qe_m[:, kl]], axis=0).astype(BF)
        rhs = jnp.concatenate([ki[:, kl], kd_m[:, kl]], axis=0).astype(BF)
        full = _dot_nt(lhs, rhs)
        a_heads.append((jnp.where(intra, full[0:R, 0:R], 0.0) + full[R:, R:]).astype(BF))
        yield
    vb = v_ref[0].astype(BF)
    lane_v = lax.broadcasted_iota(jnp.int32, (R, hv), 1)
    vbd = jnp.concatenate([jnp.where(lane_v // dv == h, vb, jnp.zeros_like(vb)) for h in range(GLA_HEADS)], axis=0)
    o = _dot(jnp.concatenate(a_heads, axis=1), vbd)
    yield
    st = st_sc[...]
    o_ref[0] = (o + _dot_nt(qx, st.astype(BF))).astype(o_ref.dtype)
    upd = _dot(v_ref[0].T.astype(BF), kx)
    on_diag = (lax.broadcasted_iota(jnp.int32, (hv, hk), 0) // dv) == (lax.broadcasted_iota(jnp.int32, (hv, hk), 1) // dk)
    st_sc[...] = st * (d_first * d_second) + jnp.where(on_diag, upd, 0.0)
    yield


def _gla_kernel(fq, fk, fv, fgl, bq, bk, bv, bgl, w2f_ref, gbf_ref, w2b_ref, gbb_ref, of_ref, ob_ref, st_sc):
    @pl.when(pl.program_id(1) == 0)
    def _():
        st_sc[...] = jnp.zeros(st_sc.shape, F32)

    _interleave(_gla_direction(fq, fk, fv, fgl, w2f_ref, gbf_ref, of_ref, st_sc.at[0], reverse=False),
                _gla_direction(bq, bk, bv, bgl, w2b_ref, gbb_ref, ob_ref, st_sc.at[1], reverse=True))


def _gla(q, k, v, gl, w2f, w2b, gbf, gbb):
    b, s, _ = q.shape
    R = GLA_ROWS
    nb = s // R
    hk = GLA_HEADS * GLA_KDIM
    hv = GLA_HEADS * GLA_VDIM
    full = lambda bi, c: (0, 0)

    def views(cidx):
        row = lambda bi, c: (bi, cidx(c), 0)
        return [pl.BlockSpec((1, R, hk), row), pl.BlockSpec((1, R, hk), row), pl.BlockSpec((1, R, hv), row),
                pl.BlockSpec((1, R, LANES), row)]

    return pl.pallas_call(
        _gla_kernel,
        grid=(b, nb),
        in_specs=views(lambda c: c) + views(lambda c: nb - 1 - c) + [
            pl.BlockSpec((3 * LANES, hk), full), pl.BlockSpec((1, hk), full),
            pl.BlockSpec((3 * LANES, hk), full), pl.BlockSpec((1, hk), full)],
        out_specs=[pl.BlockSpec((1, R, hv), lambda bi, c: (bi, c, 0)),
                   pl.BlockSpec((1, R, hv), lambda bi, c: (bi, nb - 1 - c, 0))],
        out_shape=[jax.ShapeDtypeStruct((b, s, hv), BF)] * 2,
        scratch_shapes=[pltpu.VMEM((2, hv, hk), F32)],
        compiler_params=_cparams("parallel", "arbitrary"),
        name="gla",
    )(q, k, v, gl, q, k, v, gl, w2f, gbf, w2b, gbb)


def _proj_odd_kernel(x_ref, w_ref, g_ref, b_ref, ws_ref, bias_ref, q_ref, k_ref, v_ref, r_ref, gl_ref, o_ref):
    xb = x_ref[...].astype(BF)
    off = _project_into(xb, w_ref, 0, (q_ref, k_ref, v_ref, r_ref, gl_ref))
    x = _dot(xb, w_ref[:, off:off + 2 * SGU_WIDTH])
    gel = x * (0.5 * (1.0 + jnp.tanh(math.sqrt(2.0 / math.pi) * (x + 0.044715 * (x * x * x)))))
    u = gel[:, :SGU_WIDTH]
    svn = _layernorm(gel[:, SGU_WIDTH:], g_ref[...], b_ref[...]).astype(BF)
    c = SGU_CHUNK
    for ci in range(x.shape[0] // c):
        rows = slice(ci * c, (ci + 1) * c)
        for gi in range(SGU_GROUPS):
            cols = slice(gi * SGU_GROUP_DIM, (gi + 1) * SGU_GROUP_DIM)
            sp = _dot(ws_ref[gi], svn[rows, cols]) + bias_ref[:, cols]
            o_ref[rows, cols] = (u[rows, cols] * sp).astype(o_ref.dtype)


def _proj_odd(x, w_in, ln_g, ln_b, ws_bf, bias_full):
    t, d = x.shape
    c = SGU_CHUNK
    tm = min(ROW_TILE, t)
    hk = GLA_HEADS * GLA_KDIM
    hv = GLA_HEADS * GLA_VDIM
    row = lambda i: (i, 0)
    full = lambda i: (0, 0)
    widths = (hk, hk, hv, hv, LANES, SGU_WIDTH)
    dtypes = (F32, F32, F32, BF, F32, BF)
    return pl.pallas_call(
        _proj_odd_kernel,
        grid=(t // tm,),
        in_specs=[pl.BlockSpec((tm, d), row), _resident(w_in.shape, full),
                  pl.BlockSpec((1, SGU_WIDTH), full), pl.BlockSpec((1, SGU_WIDTH), full),
                  pl.BlockSpec((SGU_GROUPS, c, c), lambda i: (0, 0, 0)),
                  pl.BlockSpec((c, SGU_WIDTH), full)],
        out_specs=[pl.BlockSpec((tm, n), row) for n in widths],
        out_shape=[jax.ShapeDtypeStruct((t, n), dt) for n, dt in zip(widths, dtypes)],
        compiler_params=_cparams("parallel"),
        name="proj_odd",
    )(x, w_in, ln_g, ln_b, ws_bf, bias_full)


def _router_kernel(x_ref, wr_ref, route_ref, cnt_ref, base_sc):
    @pl.when(pl.program_id(0) == 0)
    def _():
        base_sc[...] = jnp.zeros(base_sc.shape, F32)

    tr = x_ref.shape[0]
    x2 = _split_bf16(x_ref[...], 2)
    logits = _dot(jnp.concatenate([x2, x2[:, :x_ref.shape[1]]], axis=1), wr_ref[...])
    lane = lax.broadcasted_iota(jnp.int32, (tr, LANES), 1).astype(F32)
    lg = jnp.where(lane < N_EXPERTS, logits, -jnp.inf)
    m1 = jnp.max(lg, axis=1, keepdims=True)
    i1 = jnp.min(jnp.where(lg == m1, lane, float(LANES)), axis=1, keepdims=True)
    lg2 = jnp.where(lane == i1, -jnp.inf, lg)
    m2 = jnp.max(lg2, axis=1, keepdims=True)
    i2 = jnp.min(jnp.where(lg2 == m2, lane, float(LANES)), axis=1, keepdims=True)
    e = jnp.exp(m2 - m1)
    g1 = 1.0 / (1.0 + e)
    g2 = e / (1.0 + e)
    oh1 = (lane == i1).astype(F32)
    oh2 = (lane == i2).astype(F32)
    oh = oh1 + oh2
    row_i = lax.broadcasted_iota(jnp.int32, (tr, tr), 0)
    col_i = lax.broadcasted_iota(jnp.int32, (tr, tr), 1)
    before = _dot((col_i < row_i).astype(BF), oh.astype(BF)) + base_sc[...]
    r1 = jnp.sum(oh1 * before, axis=1, keepdims=True)
    r2 = jnp.sum(oh2 * before, axis=1, keepdims=True)
    base_sc[...] += jnp.sum(oh, axis=0, keepdims=True)
    route = jnp.zeros((tr, LANES), F32)
    for idx, val in enumerate((i1, i2, r1, r2, g1, g2)):
        route = jnp.where(lane == float(idx), val, route)
    route_ref[...] = route
    cnt_ref[...] = base_sc[...]


def _router(x, wr_pad):
    t, d = x.shape
    tr = min(ROUTE_TILE, t)
    return pl.pallas_call(
        _router_kernel,
        grid=(t // tr,),
        in_specs=[pl.BlockSpec((tr, d), lambda i: (i, 0)), pl.BlockSpec((3 * d, LANES), lambda i: (0, 0))],
        out_specs=[pl.BlockSpec((tr, LANES), lambda i: (i, 0)), pl.BlockSpec((1, LANES), lambda i: (0, 0))],
        out_shape=[jax.ShapeDtypeStruct((t, LANES), F32), jax.ShapeDtypeStruct((1, LANES), F32)],
        scratch_shapes=[pltpu.VMEM((1, LANES), F32)],
        compiler_params=_cparams("arbitrary"),
        name="router",
    )(x, wr_pad)


def _piece_indices(pos, n_rows):
    base = jnp.arange(SC_ROW_SPLIT, dtype=jnp.int32)[:, None, None] * n_rows
    return (base + pos.T[None]).reshape(1, -1)


def _sc_scatter_rows(xp, idx, n_out):
    ns, t, w = xp.shape
    nblk = t // SC_WINDOW
    per_piece = idx.shape[1] // ns // SC_WINDOW
    mesh = plsc.VectorSubcoreMesh(core_axis_name="c", subcore_axis_name="s")

    @pl.kernel(out_type=jax.ShapeDtypeStruct((ns * n_out, w), xp.dtype), mesh=mesh)
    def k(x_hbm, i_hbm, o_hbm):
        def body(x_vmem, i_vmem):
            pltpu.sync_copy(x_vmem, o_hbm.at[i_vmem.at[0]])

        pltpu.emit_pipeline(
            body,
            grid=(idx.shape[1] // SC_WINDOW,),
            in_specs=[pl.BlockSpec((SC_WINDOW, w), index_map=lambda i: ((i // per_piece) * nblk + i % nblk, 0)),
                      pl.BlockSpec((1, SC_WINDOW), index_map=lambda i: (0, i))],
            out_specs=[],
            core_axis_name=("c", "s"),
            dimension_semantics=(pltpu.PARALLEL,),
        )(x_hbm, i_hbm)

    return k(xp.reshape(ns * t, w), idx).reshape(ns, n_out, w)


def _sc_gather_rows(yp, idx):
    ns, n, w = yp.shape
    mesh = plsc.VectorSubcoreMesh(core_axis_name="c", subcore_axis_name="s")

    @pl.kernel(out_type=jax.ShapeDtypeStruct((idx.shape[1], w), yp.dtype), mesh=mesh)
    def k(x_hbm, i_hbm, o_hbm):
        def body(i_vmem, o_vmem):
            pltpu.sync_copy(x_hbm.at[i_vmem.at[0]], o_vmem)

        pltpu.emit_pipeline(
            body,
            grid=(idx.shape[1] // SC_WINDOW,),
            in_specs=[pl.BlockSpec((1, SC_WINDOW), index_map=lambda i: (0, i))],
            out_specs=[pl.BlockSpec((SC_WINDOW, w), index_map=lambda i: (i, 0))],
            core_axis_name=("c", "s"),
            dimension_semantics=(pltpu.PARALLEL,),
        )(i_hbm, o_hbm)

    return k(yp.reshape(ns * n, w), idx)


def _combine_ln_kernel(x_ref, y_ref, route_ref, g_ref, beta_ref, o_ref):
    g1 = route_ref[:, 4:5]
    g2 = route_ref[:, 5:6]
    y = (g1 * jnp.concatenate([y_ref[j, 0] for j in range(SC_ROW_SPLIT)], axis=1)
         + g2 * jnp.concatenate([y_ref[j, 1] for j in range(SC_ROW_SPLIT)], axis=1))
    o_ref[...] = _layernorm(DN_ALPHA * x_ref[...] + y, g_ref[...], beta_ref[...])


def _combine_ln(x, y2, route, g, beta):
    t, d = x.shape
    tm = min(NORM_TILE, t)
    row = lambda i: (i, 0)
    full = lambda i: (0, 0)
    return pl.pallas_call(
        _combine_ln_kernel,
        grid=(t // tm,),
        in_specs=[pl.BlockSpec((tm, d), row),
                  pl.BlockSpec((SC_ROW_SPLIT, 2, tm, d // SC_ROW_SPLIT), lambda i: (0, 0, i, 0)),
                  pl.BlockSpec((tm, LANES), row), pl.BlockSpec((1, d), full), pl.BlockSpec((1, d), full)],
        out_specs=pl.BlockSpec((tm, d), row),
        out_shape=jax.ShapeDtypeStruct((t, d), F32),
        compiler_params=_cparams("parallel"),
        name="moe_combine_ln",
    )(x, y2, route, g, beta)


def _pad_cols(w, n):
    return jnp.pad(w, ((0, 0), (0, n - w.shape[1])))


def _prep_even(p):
    (w_in, q_norm, w_uq, kv_norm, w_ukv, conv_w, conv_b, dt_bias, a_log, d_skip, ssm_norm, w_out,
     ln1_g, ln1_b, w_gate, w_up, w_down, ln2_g, ln2_b) = p
    o = 0
    cq = w_in[:, o:o + Q_RANK]; o += Q_RANK
    ckv = w_in[:, o:o + KV_RANK]; o += KV_RANK
    kr = w_in[:, o:o + QK_ROPE]; o += QK_ROPE
    z = w_in[:, o:o + SSD_INNER]; o += SSD_INNER
    xbc = w_in[:, o:o + CONV_DIM]; o += CONV_DIM
    dt = w_in[:, o:]
    half = QK_ROPE // 2
    zeros = lambda n: jnp.zeros((w_in.shape[0], n), F32)
    kra = jnp.concatenate([zeros(QK_NOPE), kr, zeros(LANES - QK_NOPE - QK_ROPE)], axis=1)
    krb = jnp.concatenate([zeros(QK_NOPE), -kr[:, half:], kr[:, :half], zeros(LANES - QK_NOPE - QK_ROPE)], axis=1)
    w_in_p = jnp.concatenate([cq, ckv, kra, krb, z, xbc, _pad_cols(dt, LANES)], axis=1).astype(BF)

    wq = w_uq.reshape(Q_RANK, MLA_HEADS, QK_NOPE + QK_ROPE)
    nope, rope = wq[..., :QK_NOPE], wq[..., QK_NOPE:]
    zq = lambda n: jnp.zeros((Q_RANK, MLA_HEADS, n), F32)
    wqa = jnp.concatenate([nope, rope, zq(LANES - QK_NOPE - QK_ROPE)], axis=-1)
    wqb = jnp.concatenate([zq(QK_NOPE), -rope[..., half:], rope[..., :half], zq(LANES - QK_NOPE - QK_ROPE)], axis=-1)
    wkv = w_ukv.reshape(KV_RANK, MLA_HEADS, QK_NOPE + V_DIM)
    zk = jnp.zeros((KV_RANK, MLA_HEADS, LANES - QK_NOPE), F32)
    wk = jnp.concatenate([wkv[..., :QK_NOPE], zk], axis=-1)
    vv_t = jnp.transpose(wkv[..., QK_NOPE:], (1, 2, 0))
    wvt = jnp.concatenate([vv_t, jnp.zeros((MLA_HEADS, V_ROWS - V_DIM, KV_RANK), F32)], axis=1)
    vadd = jnp.tile((jnp.arange(V_ROWS) == V_DIM).astype(F32), MLA_HEADS)[:, None]
    hw = MLA_HEADS * LANES
    return dict(
        w_in=w_in_p, q_norm=q_norm[None], kv_norm=kv_norm[None],
        wqa=wqa.reshape(Q_RANK, hw).astype(BF), wqb=wqb.reshape(Q_RANK, hw).astype(BF),
        wk=wk.reshape(KV_RANK, hw).astype(BF), wvt=wvt.reshape(MLA_HEADS * V_ROWS, KV_RANK).astype(BF), vadd=vadd,
        conv_w=conv_w, conv_b=conv_b[None],
        dt_bias=_pad_cols(dt_bias.reshape(1, -1), LANES), a_log=_pad_cols(a_log.reshape(1, -1), LANES),
        d_skip=jnp.repeat(d_skip, SSD_HEAD_DIM)[None], ssm_norm=ssm_norm[None],
        wo_a=w_out[:MLA_HEADS * V_DIM].astype(BF), wo_b=w_out[MLA_HEADS * V_DIM:].astype(BF),
        ln1_g=ln1_g[None], ln1_b=ln1_b[None],
        wg=w_gate.astype(BF), wu=w_up.astype(BF), wd=w_down.astype(BF),
        ln2_g=ln2_g[None], ln2_b=ln2_b[None])


def _prep_odd(p):
    (w_in, gate_w2, gate_b, gla_norm, sgu_ln_g, sgu_ln_b, w_s, b_s, w_out, ln1_g, ln1_b,
     w_router, we_gate, we_up, we_down, ln2_g, ln2_b) = p
    hk = GLA_HEADS * GLA_KDIM
    hv = GLA_HEADS * GLA_VDIM
    o = 2 * hk + 2 * hv
    gl = w_in[:, o:o + 2 * GLA_GATE_RANK]
    w_in_p = jnp.concatenate([w_in[:, :o], _pad_cols(gl, LANES), w_in[:, o + 2 * GLA_GATE_RANK:]], axis=1).astype(BF)
    zr = lambda n: jnp.zeros((n, hk), F32)
    def pieces(w):
        hi = w.astype(BF)
        lo = (w - hi.astype(F32)).astype(BF)
        return jnp.concatenate([hi, hi, lo], axis=0)

    w2f = pieces(jnp.concatenate([gate_w2[0], zr(LANES - GLA_GATE_RANK)], axis=0))
    w2b = pieces(jnp.concatenate([zr(GLA_GATE_RANK), gate_w2[1], zr(LANES - 2 * GLA_GATE_RANK)], axis=0))
    bias_full = jnp.repeat(b_s.T, SGU_GROUP_DIM, axis=1)
    return dict(
        w_in=w_in_p, w2f=w2f, w2b=w2b, gbf=gate_b[0][None], gbb=gate_b[1][None], gla_norm=gla_norm[None],
        sgu_g=sgu_ln_g[None], sgu_b=sgu_ln_b[None], ws=w_s.astype(BF), sgu_bias=bias_full,
        wo_a=w_out[:hv].astype(BF), wo_b=w_out[hv:].astype(BF), ln1_g=ln1_g[None], ln1_b=ln1_b[None],
        w_router=pieces(_pad_cols(w_router, LANES)),
        wg=we_gate.astype(BF), wu=we_up.astype(BF), wd=we_down.astype(BF),
        ln2_g=ln2_g[None], ln2_b=ln2_b[None])


def _rope_tables(s):
    half = QK_ROPE // 2
    inv = jnp.exp(-math.log(ROPE_THETA) * jnp.arange(half, dtype=F32) / half)
    ang = jnp.arange(s, dtype=F32)[:, None] * inv[None, :]
    cos, sin = jnp.cos(ang), jnp.sin(ang)
    pad = LANES - QK_NOPE - QK_ROPE
    cos_t = jnp.concatenate([jnp.ones((s, QK_NOPE), F32), cos, cos, jnp.ones((s, pad), F32)], axis=1)
    sin_t = jnp.concatenate([jnp.zeros((s, QK_NOPE), F32), sin, sin, jnp.zeros((s, pad), F32)], axis=1)
    return cos_t, sin_t


def _even_layer(x, w, b, s):
    t = b * s
    cos_t, sin_t = _rope_tables(s)
    q, k, vt, z, xbc, dt = _proj_even(x, w["w_in"], cos_t, sin_t, w["q_norm"], w["kv_norm"], w["wqa"], w["wqb"],
                                      w["wk"], w["wvt"], w["vadd"], b, s)
    hw = MLA_HEADS * LANES
    o_attn = _flash(q.reshape(b, s, hw), k.reshape(b, s, hw), vt)
    y_f, y_b = _ssd(xbc.reshape(b, s, CONV_DIM), dt.reshape(b, s, LANES),
                    w["conv_w"], w["conv_b"], w["dt_bias"], w["a_log"], w["d_skip"])
    mix = (o_attn.reshape(t, -1), y_f.reshape(t, -1), y_b.reshape(t, -1), z, w["ssm_norm"])
    x1 = _outproj_ln(mix, x, w["wo_a"], w["wo_b"], w["ln1_g"], w["ln1_b"], even=True)
    return _ffn_ln(x1, w["wg"], w["wu"], w["wd"], w["ln2_g"], w["ln2_b"])


def _moe(x1, x1p, w):
    t, d = x1.shape
    route, cnt = _router(x1, w["w_router"])
    eid = route[:, 0:2].astype(jnp.int32)
    rank = route[:, 2:4].astype(jnp.int32)
    counts = cnt[0, :N_EXPERTS].astype(jnp.int32)
    tm = MOE_TILE
    padded = ((counts + tm - 1) // tm) * tm
    ends = jnp.cumsum(padded)
    offs = ends - padded
    pos = offs[eid] + rank
    p_rows = 2 * t + N_EXPERTS * tm
    tiles = jnp.arange(p_rows // tm, dtype=jnp.int32)
    tile_ends = ends // tm
    tile_expert = jnp.minimum(jnp.sum(tiles[:, None] >= tile_ends[None, :], axis=1), N_EXPERTS - 1).astype(jnp.int32)
    tile_valid = (tiles < tile_ends[-1]).astype(jnp.int32)
    idx = _piece_indices(pos, p_rows)
    xs = _sc_scatter_rows(x1p, idx, p_rows)
    ys = _moe_ffn(xs, tile_expert, tile_valid, w["wg"], w["wu"], w["wd"])
    y2 = _sc_gather_rows(ys, idx).reshape(SC_ROW_SPLIT, 2, t, d // SC_ROW_SPLIT)
    return _combine_ln(x1, y2, route, w["ln2_g"], w["ln2_b"])


def _odd_layer(x, w, b, s):
    t = b * s
    hk = GLA_HEADS * GLA_KDIM
    hv = GLA_HEADS * GLA_VDIM
    q, k, v, r, gl, o_sgu = _proj_odd(x, w["w_in"], w["sgu_g"], w["sgu_b"], w["ws"], w["sgu_bias"])
    o_f, o_b = _gla(q.reshape(b, s, hk), k.reshape(b, s, hk), v.reshape(b, s, hv), gl.reshape(b, s, LANES),
                    w["w2f"], w["w2b"], w["gbf"], w["gbb"])
    mix = (o_f.reshape(t, hv), o_b.reshape(t, hv), r, w["gla_norm"], o_sgu)
    x1, x1p = _outproj_ln(mix, x, w["wo_a"], w["wo_b"], w["ln1_g"], w["ln1_b"], even=False)
    return _moe(x1, x1p, w)


def _trunk(x, ev_w, od_w):
    b, s, d = x.shape
    x = x.reshape(b * s, d)
    for i in range(DEPTH):
        if i % 2 == 0:
            x = _even_layer(x, ev_w[i // 2], b, s)
        else:
            x = _odd_layer(x, od_w[i // 2], b, s)
    return x.reshape(b, s, d)


def kernel(x_prompt, x_sample, ev_w_in, ev_q_norm, ev_w_uq, ev_kv_norm, ev_w_ukv, ev_conv_w, ev_conv_b, ev_dt_bias, ev_a_log, ev_d_skip, ev_ssm_norm, ev_w_out, ev_ln1_g, ev_ln1_b, ev_w_gate, ev_w_up, ev_w_down, ev_ln2_g, ev_ln2_b, od_w_in, od_gate_w2, od_gate_b, od_gla_norm, od_sgu_ln_g, od_sgu_ln_b, od_w_s, od_b_s, od_w_out, od_ln1_g, od_ln1_b, od_w_router, od_we_gate, od_we_up, od_we_down, od_ln2_g, od_ln2_b):
    ev = (ev_w_in, ev_q_norm, ev_w_uq, ev_kv_norm, ev_w_ukv, ev_conv_w, ev_conv_b, ev_dt_bias,
          ev_a_log, ev_d_skip, ev_ssm_norm, ev_w_out, ev_ln1_g, ev_ln1_b, ev_w_gate, ev_w_up,
          ev_w_down, ev_ln2_g, ev_ln2_b)
    od = (od_w_in, od_gate_w2, od_gate_b, od_gla_norm, od_sgu_ln_g, od_sgu_ln_b, od_w_s, od_b_s,
          od_w_out, od_ln1_g, od_ln1_b, od_w_router, od_we_gate, od_we_up, od_we_down,
          od_ln2_g, od_ln2_b)
    ev_w = [_prep_even(tuple(t[i] for t in ev)) for i in range(ev_w_in.shape[0])]
    od_w = [_prep_odd(tuple(t[i] for t in od)) for i in range(od_w_in.shape[0])]
    return (_trunk(x_prompt, ev_w, od_w), _trunk(x_sample, ev_w, od_w))
```

```python
import functools
import math

import jax
import jax.numpy as jnp
from jax import lax
from jax.experimental import pallas as pl
from jax.experimental.pallas import tpu as pltpu
from jax.experimental.pallas import tpu_sc as plsc

BF = jnp.bfloat16
F32 = jnp.float32

D_MODEL = 1024
DEPTH = 4
MLA_HEADS = 8
QK_NOPE = 64
QK_ROPE = 32
V_DIM = 64
Q_RANK = 256
KV_RANK = 128
ROPE_THETA = 10000.0
SSD_HEADS = 8
SSD_HEAD_DIM = 64
SSD_GROUPS = 2
SSD_STATE = 64
SSD_CHUNK = 128
CONV_K = 5
SSD_INNER = SSD_HEADS * SSD_HEAD_DIM
SSD_HPG = SSD_HEADS // SSD_GROUPS
CONV_DIM = SSD_INNER + 2 * SSD_GROUPS * SSD_STATE
GLA_HEADS = 4
GLA_KDIM = 64
GLA_VDIM = 128
GLA_GATE_RANK = 16
GLA_TAU = 16.0
GLA_CHUNK = 64
SGU_GROUPS = 4
SGU_CHUNK = 128
SGU_GROUP_DIM = 128
SGU_WIDTH = SGU_GROUPS * SGU_GROUP_DIM
D_FF = 2816
N_EXPERTS = 8
D_FF_EXPERT = 3584
DN_ALPHA = (2 * DEPTH) ** 0.25
EPS = 1e-5

LANES = 128
SUBLANES = 8
VMEM_LIMIT = 56 * 1024 * 1024
SC_WINDOW = 128
SC_ROW_SPLIT = 4

ROW_TILE = 512
NORM_TILE = 1024
FF_CHUNK = 256
ATT_TQ = 512
ATT_TK = 512
MOE_TILE = 512
ROUTE_TILE = 512
GLA_ROWS = 2 * GLA_CHUNK
GLA_STEP_BLOCKS = 4
V_ROWS = 80


def _cparams(*sem):
    return pltpu.CompilerParams(dimension_semantics=sem, vmem_limit_bytes=VMEM_LIMIT)


def _resident(shape, index_map):
    return pl.BlockSpec(shape, index_map, pipeline_mode=pl.Buffered(1))


def _rms(x):
    return x * lax.rsqrt(jnp.mean(x * x, axis=-1, keepdims=True) + EPS)


def _layernorm(x, g, b):
    mu = jnp.mean(x, axis=-1, keepdims=True)
    xc = x - mu
    var = jnp.mean(xc * xc, axis=-1, keepdims=True)
    return xc * lax.rsqrt(var + EPS) * g + b


def _silu(x):
    return x * jax.nn.sigmoid(x)


def _softplus(x):
    return jnp.maximum(x, 0.0) + jnp.log1p(jnp.exp(-jnp.abs(x)))


def _dot(a, b):
    return jnp.dot(a, b, preferred_element_type=F32)


def _dot_nt(a, b):
    return lax.dot_general(a, b, (((1,), (1,)), ((), ())), preferred_element_type=F32)


def _project_into(xb, w_ref, off, o_refs):
    for o_ref in o_refs:
        n = o_ref.shape[1]
        o_ref[...] = _dot(xb, w_ref[:, off:off + n]).astype(o_ref.dtype)
        off += n
    return off


MLA_IN = Q_RANK + KV_RANK + 2 * LANES


def _proj_even_kernel(x_ref, w_ref, cos_ref, sin_ref, qn_ref, kvn_ref, wqa_ref, wqb_ref, wk_ref, wvt_ref, vadd_ref,
                      q_ref, k_ref, vt_ref, z_ref, xbc_ref, dt_ref):
    xb = x_ref[...].astype(BF)
    _project_into(xb, w_ref, MLA_IN, (z_ref, xbc_ref, dt_ref))
    m = _dot(xb, w_ref[:, 0:MLA_IN])
    cq = m[:, 0:Q_RANK]
    ckv = m[:, Q_RANK:Q_RANK + KV_RANK]
    kra = m[:, Q_RANK + KV_RANK:Q_RANK + KV_RANK + LANES]
    krb = m[:, Q_RANK + KV_RANK + LANES:Q_RANK + KV_RANK + 2 * LANES]
    cos = cos_ref[...]
    sin = sin_ref[...]
    cos8 = jnp.concatenate([cos] * MLA_HEADS, axis=1)
    sin8 = jnp.concatenate([sin] * MLA_HEADS, axis=1)
    cqn = (_rms(cq) * qn_ref[...]).astype(BF)
    q = _dot(cqn, wqa_ref[...]) * cos8 + _dot(cqn, wqb_ref[...]) * sin8
    q_ref[...] = (q * ((QK_NOPE + QK_ROPE) ** -0.5 * math.log2(math.e))).astype(BF)
    ckvn = (_rms(ckv) * kvn_ref[...]).astype(BF)
    kr = kra * cos + krb * sin
    k = _dot(ckvn, wk_ref[...]) + jnp.concatenate([kr] * MLA_HEADS, axis=1)
    k_ref[...] = k.astype(BF)
    vt_ref[0] = (_dot_nt(wvt_ref[...], ckvn) + vadd_ref[...]).astype(BF)


def _proj_even(x, w_in, cos_t, sin_t, q_norm, kv_norm, wqa, wqb, wk, wvt, vadd, batch, seq):
    t, d = x.shape
    tm = min(ROW_TILE, seq)
    nseq = seq // tm
    hw = MLA_HEADS * LANES
    vr = MLA_HEADS * V_ROWS
    full = lambda i: (0, 0)
    row = lambda i: (i, 0)
    return pl.pallas_call(
        _proj_even_kernel,
        grid=(t // tm,),
        in_specs=[pl.BlockSpec((tm, d), row), _resident(w_in.shape, full),
                  pl.BlockSpec((tm, LANES), lambda i: (i % nseq, 0)),
                  pl.BlockSpec((tm, LANES), lambda i: (i % nseq, 0)),
                  pl.BlockSpec((1, Q_RANK), full), pl.BlockSpec((1, KV_RANK), full),
                  pl.BlockSpec((Q_RANK, hw), full), pl.BlockSpec((Q_RANK, hw), full),
                  pl.BlockSpec((KV_RANK, hw), full), pl.BlockSpec((vr, KV_RANK), full),
                  pl.BlockSpec((vr, 1), full)],
        out_specs=[pl.BlockSpec((tm, hw), row), pl.BlockSpec((tm, hw), row),
                   pl.BlockSpec((1, vr, tm), lambda i: (i // nseq, 0, i % nseq)),
                   pl.BlockSpec((tm, SSD_INNER), row), pl.BlockSpec((tm, CONV_DIM), row), pl.BlockSpec((tm, LANES), row)],
        out_shape=[jax.ShapeDtypeStruct((t, hw), BF), jax.ShapeDtypeStruct((t, hw), BF),
                   jax.ShapeDtypeStruct((batch, vr, seq), BF),
                   jax.ShapeDtypeStruct((t, SSD_INNER), BF), jax.ShapeDtypeStruct((t, CONV_DIM), F32),
                   jax.ShapeDtypeStruct((t, LANES), F32)],
        compiler_params=_cparams("parallel"),
        name="proj_even",
    )(x, w_in, cos_t, sin_t, q_norm, kv_norm, wqa, wqb, wk, wvt, vadd)


def _flash_kernel(q_ref, k_ref, vt_ref, o_ref, m_sc, acc_sc, s_sc, *, tk, nk, unroll):
    for h in range(2):
        m_sc[h] = jnp.full(m_sc.shape[1:], -jnp.inf, F32)
        acc_sc[h] = jnp.zeros(acc_sc.shape[1:], F32)

    def scores(j, slot):
        off = pl.multiple_of(j * tk, tk)
        for h in range(2):
            lanes = slice(h * LANES, (h + 1) * LANES)
            s_sc[slot, h] = _dot_nt(k_ref[0, pl.ds(off, tk), lanes], q_ref[0, :, lanes])

    def consume(j, slot):
        off = pl.multiple_of(j * tk, tk)
        for h in range(2):
            st = s_sc[slot, h]
            m_prev = m_sc[h]
            m_new = jnp.maximum(m_prev, jnp.max(st, axis=0, keepdims=True))
            p = jnp.exp2(st - m_new[0:1, :]).astype(BF)
            alpha = jnp.exp2(m_prev - m_new)
            pv = _dot(vt_ref[0, h * V_ROWS:(h + 1) * V_ROWS, pl.ds(off, tk)], p)
            acc_sc[h] = alpha[0:1, :] * acc_sc[h] + pv
            m_sc[h] = m_new

    scores(0, 0)

    def body(jj, carry):
        j = unroll * jj
        for u in range(unroll):
            scores(jnp.minimum(j + u + 1, nk - 1), (u + 1) % 2)
            consume(j + u, u % 2)
        return carry

    lax.fori_loop(0, nk // unroll, body, 0)
    outs = []
    for h in range(2):
        acc = acc_sc[h]
        outs.append(acc[0:V_DIM, :] / acc[V_DIM:V_DIM + 1, :])
    o_ref[0] = jnp.concatenate(outs, axis=0).T.astype(o_ref.dtype)


def _flash(q, k, vt):
    b, s, hw = q.shape
    tq = min(ATT_TQ, s)
    tk = min(ATT_TK, s)
    nk = s // tk
    unroll = next(u for u in (8, 4, 2, 1) if nk % u == 0)
    pairs = MLA_HEADS // 2
    return pl.pallas_call(
        functools.partial(_flash_kernel, tk=tk, nk=nk, unroll=unroll),
        grid=(b, pairs, s // tq),
        in_specs=[pl.BlockSpec((1, tq, 2 * LANES), lambda bi, hp, i: (bi, i, hp)),
                  pl.BlockSpec((1, s, 2 * LANES), lambda bi, hp, i: (bi, 0, hp)),
                  pl.BlockSpec((1, 2 * V_ROWS, s), lambda bi, hp, i: (bi, hp, 0))],
        out_specs=pl.BlockSpec((1, tq, LANES), lambda bi, hp, i: (bi, i, hp)),
        out_shape=jax.ShapeDtypeStruct((b, s, MLA_HEADS * V_DIM), BF),
        scratch_shapes=[pltpu.VMEM((2, SUBLANES, tq), F32), pltpu.VMEM((2, V_ROWS, tq), F32),
                        pltpu.VMEM((2, 2, tk, tq), F32)],
        compiler_params=_cparams("parallel", "parallel", "arbitrary"),
        name="mla_flash",
    )(q, k, vt)


def _split_bf16(x, pieces):
    out = []
    for _ in range(pieces):
        p = x.astype(BF)
        out.append(p)
        x = x - p.astype(F32)
    return jnp.concatenate(out, axis=1)


def _spread_matrix(first_lane, heads, width, pieces):
    src = jnp.arange(LANES)[:, None] - first_lane
    dst = jnp.arange(heads * width)[None, :] // width
    return jnp.tile((src == dst).astype(BF), (pieces, 1))


def _ssd_direction(xc_ref, xp_ref, xn_ref, dt_ref, ep_ref, el_ref, cw_ref, cb_ref, dtb_ref, alog_ref, dskip_ref,
                   o_ref, xe_sc, st_sc, *, reverse, cc, nc):
    L = SSD_CHUNK
    N = SSD_STATE
    P = SSD_HEAD_DIM
    xe_sc[0:SUBLANES, :] = jnp.where(cc > 0, xp_ref[0], 0.0)
    xe_sc[SUBLANES:SUBLANES + L, :] = xc_ref[0]
    xe_sc[SUBLANES + L:2 * SUBLANES + L, :] = jnp.where(cc < nc - 1, xn_ref[0], 0.0)
    conv = cb_ref[...] + cw_ref[0:1, :] * xe_sc[pl.ds(SUBLANES - CONV_K // 2, L), :]
    for j in range(1, CONV_K):
        conv = conv + cw_ref[j:j + 1, :] * xe_sc[pl.ds(SUBLANES - CONV_K // 2 + j, L), :]
    yield
    xbc = _silu(conv)
    xs = xbc[:, :SSD_INNER]
    bc = xbc[:, SSD_INNER:]
    bc_t = bc.T

    lane = lax.broadcasted_iota(jnp.int32, (L, LANES), 1)
    dtv = _softplus(dt_ref[0] + dtb_ref[...])
    a = jnp.where(lane[0:1] < 2 * SSD_HEADS, -jnp.exp(alog_ref[...]), 0.0)
    dta = dtv * a
    row_i = lax.broadcasted_iota(jnp.int32, (L, L), 0)
    col_i = lax.broadcasted_iota(jnp.int32, (L, L), 1)
    causal = (col_i >= row_i) if reverse else (col_i <= row_i)
    acs3 = _dot(causal.astype(BF), _split_bf16(dta, 3))
    yield
    acs = acs3[:, 0:LANES] + acs3[:, LANES:2 * LANES] + acs3[:, 2 * LANES:]
    acs_t = acs.T
    d0 = SSD_HEADS if reverse else 0
    end = 0 if reverse else L - 1
    tot = acs[end:end + 1, :]
    yield

    stacked = jnp.concatenate([dtv, jnp.exp(acs), jnp.exp(tot - acs),
                               jnp.broadcast_to(jnp.exp(tot), (2 * SUBLANES, LANES))], axis=0)
    spread = _dot(_split_bf16(stacked, 2), ep_ref[...])
    dt_x = spread[0:L]
    ea_x = spread[L:2 * L]
    eb_x = spread[2 * L:3 * L]
    et_x = spread[3 * L:3 * L + 1]
    col_x = _dot(_split_bf16(acs, 3), el_ref[...])
    yield
    row_x = jnp.concatenate([jnp.broadcast_to(acs_t[d0 + h:d0 + h + 1, :], (L, L)) for h in range(SSD_HEADS)], axis=1)
    causal_x = jnp.concatenate([causal] * SSD_HEADS, axis=1)
    decay_x = jnp.exp(jnp.where(causal_x, col_x - row_x, -jnp.inf))
    xdt = xs * dt_x
    xdt_b = xdt.astype(BF)
    xw_b = (xdt * eb_x).astype(BF)
    gw = SSD_HPG * P
    lane_g = lax.broadcasted_iota(jnp.int32, (L, gw), 1)
    yield

    ys = []
    for g in range(SSD_GROUPS):
        bm_g = bc[:, g * N:(g + 1) * N].astype(BF)
        cm_g = bc[:, (SSD_GROUPS + g) * N:(SSD_GROUPS + g + 1) * N].astype(BF)
        bm_t_g = bc_t[g * N:(g + 1) * N, :].astype(BF)
        cb = _dot_nt(cm_g, bm_g)
        m_g = (jnp.concatenate([cb] * SSD_HPG, axis=1) * decay_x[:, g * SSD_HPG * L:(g + 1) * SSD_HPG * L]).astype(BF)
        xg = xdt_b[:, g * gw:(g + 1) * gw]
        xbd = jnp.concatenate([jnp.where(lane_g // P == j, xg, jnp.zeros_like(xg)) for j in range(SSD_HPG)], axis=0)
        y_diag = _dot(m_g, xbd)
        states_t = _dot(bm_t_g, xw_b[:, g * gw:(g + 1) * gw])
        prev_t = st_sc[g]
        y_off = _dot(cm_g, prev_t.astype(BF)) * ea_x[:, g * gw:(g + 1) * gw]
        st_sc[g] = prev_t * et_x[:, g * gw:(g + 1) * gw] + states_t
        ys.append(y_diag + y_off)
        yield
    y = jnp.concatenate(ys, axis=1)
    if not reverse:
        y = y + dskip_ref[...] * xs
    o_ref[0] = y.astype(o_ref.dtype)


def _interleave(*tracers):
    live = list(tracers)
    while live:
        for g in list(live):
            try:
                next(g)
            except StopIteration:
                live.remove(g)


def _ssd_kernel(fxc, fxp, fxn, fdt, bxc, bxp, bxn, bdt, epf_ref, elf_ref, epb_ref, elb_ref,
                cw_ref, cb_ref, dtb_ref, alog_ref, dskip_ref, yf_ref, yb_ref, xe_sc, st_sc, *, nc):
    c = pl.program_id(1)

    @pl.when(c == 0)
    def _():
        st_sc[...] = jnp.zeros(st_sc.shape, F32)

    shared = (cw_ref, cb_ref, dtb_ref, alog_ref, dskip_ref)
    _interleave(
        _ssd_direction(fxc, fxp, fxn, fdt, epf_ref, elf_ref, *shared, yf_ref, xe_sc.at[0], st_sc.at[0],
                       reverse=False, cc=c, nc=nc),
        _ssd_direction(bxc, bxp, bxn, bdt, epb_ref, elb_ref, *shared, yb_ref, xe_sc.at[1], st_sc.at[1],
                       reverse=True, cc=nc - 1 - c, nc=nc))


def _ssd(xbc, dt, conv_w, conv_b, dt_bias, a_log, d_skip):
    b, s, _ = xbc.shape
    L = SSD_CHUNK
    nc = s // L
    hb = L // SUBLANES
    nhb = s // SUBLANES
    full = lambda bi, c: (0, 0)

    def views(cidx):
        row = lambda bi, c: (bi, cidx(c), 0)
        return [pl.BlockSpec((1, L, CONV_DIM), row),
                pl.BlockSpec((1, SUBLANES, CONV_DIM), lambda bi, c: (bi, jnp.maximum(cidx(c) * hb - 1, 0), 0)),
                pl.BlockSpec((1, SUBLANES, CONV_DIM), lambda bi, c: (bi, jnp.minimum((cidx(c) + 1) * hb, nhb - 1), 0)),
                pl.BlockSpec((1, L, LANES), row)]

    spreads = [_spread_matrix(d0, SSD_HEADS, width, pieces)
               for d0 in (0, SSD_HEADS) for width, pieces in ((SSD_HEAD_DIM, 2), (L, 3))]
    return pl.pallas_call(
        functools.partial(_ssd_kernel, nc=nc),
        grid=(b, nc),
        in_specs=views(lambda c: c) + views(lambda c: nc - 1 - c) + [pl.BlockSpec(m.shape, full) for m in spreads] + [
            pl.BlockSpec((CONV_K, CONV_DIM), full), pl.BlockSpec((1, CONV_DIM), full),
            pl.BlockSpec((1, LANES), full), pl.BlockSpec((1, LANES), full), pl.BlockSpec((1, SSD_INNER), full)],
        out_specs=[pl.BlockSpec((1, L, SSD_INNER), lambda bi, c: (bi, c, 0)),
                   pl.BlockSpec((1, L, SSD_INNER), lambda bi, c: (bi, nc - 1 - c, 0))],
        out_shape=[jax.ShapeDtypeStruct((b, s, SSD_INNER), BF)] * 2,
        scratch_shapes=[pltpu.VMEM((2, L + 2 * SUBLANES, CONV_DIM), F32),
                        pltpu.VMEM((2, SSD_GROUPS, SSD_STATE, SSD_HPG * SSD_HEAD_DIM), F32)],
        compiler_params=_cparams("parallel", "arbitrary"),
        name="ssd",
    )(xbc, xbc, xbc, dt, xbc, xbc, xbc, dt, *spreads, conv_w, conv_b, dt_bias, a_log, d_skip)


def _to_pieces(ref, y):
    w = ref.shape[-1]
    for j in range(SC_ROW_SPLIT):
        ref[j] = y[:, j * w:(j + 1) * w].astype(ref.dtype)


def _from_pieces(ref):
    return jnp.concatenate([ref[j] for j in range(SC_ROW_SPLIT)], axis=1)


def _outproj_ln_kernel(*refs, even):
    if even:
        oa_ref, yf_ref, yb_ref, z_ref, nrm_ref = refs[:5]
        a = oa_ref[...]
        y_ssd = yf_ref[...].astype(F32) + yb_ref[...].astype(F32)
        b = _rms(y_ssd * _silu(z_ref[...].astype(F32))) * nrm_ref[...]
    else:
        of_ref, ob_ref, r_ref, gn_ref, sgu_ref = refs[:5]
        o = of_ref[...].astype(F32) + ob_ref[...].astype(F32)
        o = jnp.concatenate([_rms(o[:, h * GLA_VDIM:(h + 1) * GLA_VDIM]) for h in range(GLA_HEADS)], axis=1)
        a = o * gn_ref[...] * _silu(r_ref[...].astype(F32))
        b = sgu_ref[...]
    x_ref, wa_ref, wb_ref, g_ref, beta_ref, o_ref = refs[5:11]
    y = _dot(a.astype(BF), wa_ref[...]) + _dot(b.astype(BF), wb_ref[...])
    out = _layernorm(DN_ALPHA * x_ref[...] + y, g_ref[...], beta_ref[...])
    o_ref[...] = out
    for p_ref in refs[11:]:
        _to_pieces(p_ref, out)


def _outproj_ln(mix, x, wa, wb, g, beta, even):
    t, d = x.shape
    tm = min(NORM_TILE, t)
    row = lambda i: (i, 0)
    full = lambda i: (0, 0)
    out_specs = [pl.BlockSpec((tm, d), row)]
    out_shape = [jax.ShapeDtypeStruct((t, d), F32)]
    if not even:
        out_specs.append(pl.BlockSpec((SC_ROW_SPLIT, tm, d // SC_ROW_SPLIT), lambda i: (0, i, 0)))
        out_shape.append(jax.ShapeDtypeStruct((SC_ROW_SPLIT, t, d // SC_ROW_SPLIT), F32))
    mix_specs = [pl.BlockSpec((1, m.shape[1]), full) if m.shape[0] == 1 else pl.BlockSpec((tm, m.shape[1]), row)
                 for m in mix]
    res = pl.pallas_call(
        functools.partial(_outproj_ln_kernel, even=even),
        grid=(t // tm,),
        in_specs=mix_specs + [pl.BlockSpec((tm, d), row),
                              _resident(wa.shape, full), _resident(wb.shape, full),
                              pl.BlockSpec((1, d), full), pl.BlockSpec((1, d), full)],
        out_specs=out_specs,
        out_shape=out_shape,
        compiler_params=_cparams("parallel"),
        name="outproj_ln",
    )(*mix, x, wa, wb, g, beta)
    return res[0] if even else res


def _swiglu_acc(xb, wg_ref, wu_ref, wd_ref, acc_sc, nf):
    for f in range(nf):
        cols = slice(f * FF_CHUNK, (f + 1) * FF_CHUNK)
        h = _silu(_dot(xb, wg_ref[:, cols])) * _dot(xb, wu_ref[:, cols])
        part = _dot(h.astype(BF), wd_ref[cols, :])
        if f == 0:
            acc_sc[...] = part
        else:
            acc_sc[...] += part


def _ffn_ln_kernel(x_ref, wg_ref, wu_ref, wd_ref, g_ref, beta_ref, o_ref, acc_sc, *, nf):
    x = x_ref[...]
    _swiglu_acc(x.astype(BF), wg_ref, wu_ref, wd_ref, acc_sc, nf)
    o_ref[...] = _layernorm(DN_ALPHA * x + acc_sc[...], g_ref[...], beta_ref[...])


def _ffn_ln(x, wg, wu, wd, g, beta):
    t, d = x.shape
    f = wg.shape[1]
    tm = min(ROW_TILE, t)
    row = lambda i: (i, 0)
    full = lambda i: (0, 0)
    return pl.pallas_call(
        functools.partial(_ffn_ln_kernel, nf=f // FF_CHUNK),
        grid=(t // tm,),
        in_specs=[pl.BlockSpec((tm, d), row),
                  _resident((d, f), full), _resident((d, f), full), _resident((f, d), full),
                  pl.BlockSpec((1, d), full), pl.BlockSpec((1, d), full)],
        out_specs=pl.BlockSpec((tm, d), row),
        out_shape=jax.ShapeDtypeStruct((t, d), F32),
        scratch_shapes=[pltpu.VMEM((tm, d), F32)],
        compiler_params=_cparams("parallel"),
        name="ffn_ln",
    )(x, wg, wu, wd, g, beta)


def _moe_ffn_kernel(te_ref, tv_ref, xs_ref, wg_ref, wu_ref, wd_ref, o_ref, acc_sc, *, nf):
    @pl.when(tv_ref[pl.program_id(0)] > 0)
    def _():
        _swiglu_acc(_from_pieces(xs_ref).astype(BF), wg_ref, wu_ref, wd_ref, acc_sc, nf)
        _to_pieces(o_ref, acc_sc[...])


def _moe_ffn(xs, tile_expert, tile_valid, wg, wu, wd):
    ns, p, w = xs.shape
    d = ns * w
    f = wg.shape[2]
    tm = MOE_TILE
    row = lambda i, te, tv: (0, i, 0)
    grid_spec = pltpu.PrefetchScalarGridSpec(
        num_scalar_prefetch=2,
        grid=(p // tm,),
        in_specs=[pl.BlockSpec((ns, tm, w), row),
                  pl.BlockSpec((None, d, f), lambda i, te, tv: (te[i], 0, 0)),
                  pl.BlockSpec((None, d, f), lambda i, te, tv: (te[i], 0, 0)),
                  pl.BlockSpec((None, f, d), lambda i, te, tv: (te[i], 0, 0))],
        out_specs=pl.BlockSpec((ns, tm, w), row),
        scratch_shapes=[pltpu.VMEM((tm, d), F32)],
    )
    return pl.pallas_call(
        functools.partial(_moe_ffn_kernel, nf=f // FF_CHUNK),
        grid_spec=grid_spec,
        out_shape=jax.ShapeDtypeStruct((ns, p, w), F32),
        compiler_params=_cparams("arbitrary"),
        name="moe_ffn",
    )(tile_expert, tile_valid, xs, wg, wu, wd)


def _gla_direction(q_ref, k_ref, v_ref, gl_ref, w2_ref, gb_ref, o_ref, st_sc, *, reverse, r0):
    L = GLA_CHUNK
    R = GLA_ROWS
    dk = GLA_KDIM
    dv = GLA_VDIM
    hk = GLA_HEADS * dk
    hv = GLA_HEADS * dv
    blk = slice(r0, r0 + R)
    v_blk = v_ref[0, blk]
    g2 = _split_bf16(gl_ref[0, blk], 2)
    pre = _dot(jnp.concatenate([g2, g2[:, :LANES]], axis=1), w2_ref[...]) + gb_ref[...]
    yield
    lg = -_softplus(-pre) * (1.0 / GLA_TAU)
    row_i = lax.broadcasted_iota(jnp.int32, (R, R), 0)
    col_i = lax.broadcasted_iota(jnp.int32, (R, R), 1)
    intra = ((row_i // L) == (col_i // L)) & ((col_i >= row_i) if reverse else (col_i <= row_i))
    bc3 = _dot(intra.astype(BF), _split_bf16(lg, 3))
    yield
    bc = bc3[:, 0:hk] + bc3[:, hk:2 * hk] + bc3[:, 2 * hk:]
    mid = (L // 2 - 1) if reverse else (L // 2)
    end = 0 if reverse else (L - 1)
    ref_b = jnp.concatenate([jnp.broadcast_to(bc[ci * L + mid:ci * L + mid + 1], (L, hk))
                             for ci in range(R // L)], axis=0)
    end_b = jnp.concatenate([jnp.broadcast_to(bc[ci * L + end:ci * L + end + 1], (L, hk))
                             for ci in range(R // L)], axis=0)
    q = q_ref[0, blk] * (dk ** -0.5)
    k = k_ref[0, blk]
    qi = q * jnp.exp(bc - ref_b)
    ki = k * jnp.exp(ref_b - bc)
    qe = q * jnp.exp(bc)
    kd = k * jnp.exp(end_b - bc)
    first = 1 if reverse else 0
    in_first = (lax.broadcasted_iota(jnp.int32, (R, hk), 0) // L) == first
    d_first = jnp.exp(bc[first * L + end:first * L + end + 1])
    d_second = jnp.exp(bc[(1 - first) * L + end:(1 - first) * L + end + 1])
    qx = jnp.where(in_first, qe, qe * d_first).astype(BF)
    kx = jnp.where(in_first, kd * d_second, kd).astype(BF)
    qe_m = jnp.where(in_first, 0.0, qe)
    kd_m = jnp.where(in_first, kd, 0.0)
    yield

    a_heads = []
    for h in range(GLA_HEADS):
        kl = slice(h * dk, (h + 1) * dk)
        lhs = jnp.concatenate([qi[:, kl], qe_m[:, kl]], axis=0).astype(BF)
        rhs = jnp.concatenate([ki[:, kl], kd_m[:, kl]], axis=0).astype(BF)
        full = _dot_nt(lhs, rhs)
        a_heads.append((jnp.where(intra, full[0:R, 0:R], 0.0) + full[R:, R:]).astype(BF))
        yield
    vb = v_blk.astype(BF)
    lane_v = lax.broadcasted_iota(jnp.int32, (R, hv), 1)
    vbd = jnp.concatenate([jnp.where(lane_v // dv == h, vb, jnp.zeros_like(vb)) for h in range(GLA_HEADS)], axis=0)
    o = _dot(jnp.concatenate(a_heads, axis=1), vbd)
    yield
    st = st_sc[...]
    o_ref[0, blk] = (o + _dot_nt(qx, st.astype(BF))).astype(o_ref.dtype)
    upd = _dot(v_blk.T.astype(BF), kx)
    on_diag = (lax.broadcasted_iota(jnp.int32, (hv, hk), 0) // dv) == (lax.broadcasted_iota(jnp.int32, (hv, hk), 1) // dk)
    st_sc[...] = st * (d_first * d_second) + jnp.where(on_diag, upd, 0.0)
    yield


def _gla_kernel(fq, fk, fv, fgl, bq, bk, bv, bgl, w2f_ref, gbf_ref, w2b_ref, gbb_ref, of_ref, ob_ref, st_sc):
    @pl.when(pl.program_id(1) == 0)
    def _():
        st_sc[...] = jnp.zeros(st_sc.shape, F32)

    fwd = (fq, fk, fv, fgl, w2f_ref, gbf_ref, of_ref, st_sc.at[0])
    bwd = (bq, bk, bv, bgl, w2b_ref, gbb_ref, ob_ref, st_sc.at[1])
    blocks = [r * GLA_ROWS for r in range(GLA_STEP_BLOCKS)]
    _interleave(*[_gla_direction(*fwd, reverse=False, r0=r0) for r0 in blocks],
                *[_gla_direction(*bwd, reverse=True, r0=r0) for r0 in reversed(blocks)])


def _gla(q, k, v, gl, w2f, w2b, gbf, gbb):
    b, s, _ = q.shape
    R = GLA_ROWS * GLA_STEP_BLOCKS
    nb = s // R
    hk = GLA_HEADS * GLA_KDIM
    hv = GLA_HEADS * GLA_VDIM
    full = lambda bi, c: (0, 0)

    def views(cidx):
        row = lambda bi, c: (bi, cidx(c), 0)
        return [pl.BlockSpec((1, R, hk), row), pl.BlockSpec((1, R, hk), row), pl.BlockSpec((1, R, hv), row),
                pl.BlockSpec((1, R, LANES), row)]

    return pl.pallas_call(
        _gla_kernel,
        grid=(b, nb),
        in_specs=views(lambda c: c) + views(lambda c: nb - 1 - c) + [
            pl.BlockSpec((3 * LANES, hk), full), pl.BlockSpec((1, hk), full),
            pl.BlockSpec((3 * LANES, hk), full), pl.BlockSpec((1, hk), full)],
        out_specs=[pl.BlockSpec((1, R, hv), lambda bi, c: (bi, c, 0)),
                   pl.BlockSpec((1, R, hv), lambda bi, c: (bi, nb - 1 - c, 0))],
        out_shape=[jax.ShapeDtypeStruct((b, s, hv), BF)] * 2,
        scratch_shapes=[pltpu.VMEM((2, hv, hk), F32)],
        compiler_params=_cparams("parallel", "arbitrary"),
        name="gla",
    )(q, k, v, gl, q, k, v, gl, w2f, gbf, w2b, gbb)


def _proj_odd_kernel(x_ref, w_ref, g_ref, b_ref, ws_ref, bias_ref, q_ref, k_ref, v_ref, r_ref, gl_ref, o_ref):
    xb = x_ref[...].astype(BF)
    off = _project_into(xb, w_ref, 0, (q_ref, k_ref, v_ref, r_ref, gl_ref))
    x = _dot(xb, w_ref[:, off:off + 2 * SGU_WIDTH])
    gel = x * (0.5 * (1.0 + jnp.tanh(math.sqrt(2.0 / math.pi) * (x + 0.044715 * (x * x * x)))))
    u = gel[:, :SGU_WIDTH]
    svn = _layernorm(gel[:, SGU_WIDTH:], g_ref[...], b_ref[...]).astype(BF)
    c = SGU_CHUNK
    for ci in range(x.shape[0] // c):
        rows = slice(ci * c, (ci + 1) * c)
        for gi in range(SGU_GROUPS):
            cols = slice(gi * SGU_GROUP_DIM, (gi + 1) * SGU_GROUP_DIM)
            sp = _dot(ws_ref[gi], svn[rows, cols]) + bias_ref[:, cols]
            o_ref[rows, cols] = (u[rows, cols] * sp).astype(o_ref.dtype)


def _proj_odd(x, w_in, ln_g, ln_b, ws_bf, bias_full):
    t, d = x.shape
    c = SGU_CHUNK
    tm = min(ROW_TILE, t)
    hk = GLA_HEADS * GLA_KDIM
    hv = GLA_HEADS * GLA_VDIM
    row = lambda i: (i, 0)
    full = lambda i: (0, 0)
    widths = (hk, hk, hv, hv, LANES, SGU_WIDTH)
    dtypes = (F32, F32, F32, BF, F32, BF)
    return pl.pallas_call(
        _proj_odd_kernel,
        grid=(t // tm,),
        in_specs=[pl.BlockSpec((tm, d), row), _resident(w_in.shape, full),
                  pl.BlockSpec((1, SGU_WIDTH), full), pl.BlockSpec((1, SGU_WIDTH), full),
                  pl.BlockSpec((SGU_GROUPS, c, c), lambda i: (0, 0, 0)),
                  pl.BlockSpec((c, SGU_WIDTH), full)],
        out_specs=[pl.BlockSpec((tm, n), row) for n in widths],
        out_shape=[jax.ShapeDtypeStruct((t, n), dt) for n, dt in zip(widths, dtypes)],
        compiler_params=_cparams("parallel"),
        name="proj_odd",
    )(x, w_in, ln_g, ln_b, ws_bf, bias_full)


def _router_kernel(x_ref, wr_ref, route_ref, cnt_ref, base_sc):
    @pl.when(pl.program_id(0) == 0)
    def _():
        base_sc[...] = jnp.zeros(base_sc.shape, F32)

    tr = x_ref.shape[0]
    x2 = _split_bf16(x_ref[...], 2)
    logits = _dot(jnp.concatenate([x2, x2[:, :x_ref.shape[1]]], axis=1), wr_ref[...])
    lane = lax.broadcasted_iota(jnp.int32, (tr, LANES), 1).astype(F32)
    lg = jnp.where(lane < N_EXPERTS, logits, -jnp.inf)
    m1 = jnp.max(lg, axis=1, keepdims=True)
    i1 = jnp.min(jnp.where(lg == m1, lane, float(LANES)), axis=1, keepdims=True)
    lg2 = jnp.where(lane == i1, -jnp.inf, lg)
    m2 = jnp.max(lg2, axis=1, keepdims=True)
    i2 = jnp.min(jnp.where(lg2 == m2, lane, float(LANES)), axis=1, keepdims=True)
    e = jnp.exp(m2 - m1)
    g1 = 1.0 / (1.0 + e)
    g2 = e / (1.0 + e)
    oh1 = (lane == i1).astype(F32)
    oh2 = (lane == i2).astype(F32)
    oh = oh1 + oh2
    row_i = lax.broadcasted_iota(jnp.int32, (tr, tr), 0)
    col_i = lax.broadcasted_iota(jnp.int32, (tr, tr), 1)
    before = _dot((col_i < row_i).astype(BF), oh.astype(BF)) + base_sc[...]
    r1 = jnp.sum(oh1 * before, axis=1, keepdims=True)
    r2 = jnp.sum(oh2 * before, axis=1, keepdims=True)
    base_sc[...] += jnp.sum(oh, axis=0, keepdims=True)
    route = jnp.zeros((tr, LANES), F32)
    for idx, val in enumerate((i1, i2, r1, r2, g1, g2)):
        route = jnp.where(lane == float(idx), val, route)
    route_ref[...] = route
    cnt_ref[...] = base_sc[...]


def _router(x, wr_pad):
    t, d = x.shape
    tr = min(ROUTE_TILE, t)
    return pl.pallas_call(
        _router_kernel,
        grid=(t // tr,),
        in_specs=[pl.BlockSpec((tr, d), lambda i: (i, 0)), pl.BlockSpec((3 * d, LANES), lambda i: (0, 0))],
        out_specs=[pl.BlockSpec((tr, LANES), lambda i: (i, 0)), pl.BlockSpec((1, LANES), lambda i: (0, 0))],
        out_shape=[jax.ShapeDtypeStruct((t, LANES), F32), jax.ShapeDtypeStruct((1, LANES), F32)],
        scratch_shapes=[pltpu.VMEM((1, LANES), F32)],
        compiler_params=_cparams("arbitrary"),
        name="router",
    )(x, wr_pad)


def _piece_indices(pos, n_rows):
    base = jnp.arange(SC_ROW_SPLIT, dtype=jnp.int32)[:, None, None] * n_rows
    return (base + pos.T[None]).reshape(1, -1)


def _sc_scatter_rows(xp, idx, n_out):
    ns, t, w = xp.shape
    nblk = t // SC_WINDOW
    per_piece = idx.shape[1] // ns // SC_WINDOW
    mesh = plsc.VectorSubcoreMesh(core_axis_name="c", subcore_axis_name="s")

    @pl.kernel(out_type=jax.ShapeDtypeStruct((ns * n_out, w), xp.dtype), mesh=mesh)
    def k(x_hbm, i_hbm, o_hbm):
        def body(x_vmem, i_vmem):
            pltpu.sync_copy(x_vmem, o_hbm.at[i_vmem.at[0]])

        pltpu.emit_pipeline(
            body,
            grid=(idx.shape[1] // SC_WINDOW,),
            in_specs=[pl.BlockSpec((SC_WINDOW, w), index_map=lambda i: ((i // per_piece) * nblk + i % nblk, 0)),
                      pl.BlockSpec((1, SC_WINDOW), index_map=lambda i: (0, i))],
            out_specs=[],
            core_axis_name=("c", "s"),
            dimension_semantics=(pltpu.PARALLEL,),
        )(x_hbm, i_hbm)

    return k(xp.reshape(ns * t, w), idx).reshape(ns, n_out, w)


def _sc_gather_rows(yp, idx):
    ns, n, w = yp.shape
    mesh = plsc.VectorSubcoreMesh(core_axis_name="c", subcore_axis_name="s")

    @pl.kernel(out_type=jax.ShapeDtypeStruct((idx.shape[1], w), yp.dtype), mesh=mesh)
    def k(x_hbm, i_hbm, o_hbm):
        def body(i_vmem, o_vmem):
            pltpu.sync_copy(x_hbm.at[i_vmem.at[0]], o_vmem)

        pltpu.emit_pipeline(
            body,
            grid=(idx.shape[1] // SC_WINDOW,),
            in_specs=[pl.BlockSpec((1, SC_WINDOW), index_map=lambda i: (0, i))],
            out_specs=[pl.BlockSpec((SC_WINDOW, w), index_map=lambda i: (i, 0))],
            core_axis_name=("c", "s"),
            dimension_semantics=(pltpu.PARALLEL,),
        )(i_hbm, o_hbm)

    return k(yp.reshape(ns * n, w), idx)


def _combine_ln_kernel(x_ref, y_ref, route_ref, g_ref, beta_ref, o_ref):
    g1 = route_ref[:, 4:5]
    g2 = route_ref[:, 5:6]
    y = (g1 * jnp.concatenate([y_ref[j, 0] for j in range(SC_ROW_SPLIT)], axis=1)
         + g2 * jnp.concatenate([y_ref[j, 1] for j in range(SC_ROW_SPLIT)], axis=1))
    o_ref[...] = _layernorm(DN_ALPHA * x_ref[...] + y, g_ref[...], beta_ref[...])


def _combine_ln(x, y2, route, g, beta):
    t, d = x.shape
    tm = min(NORM_TILE, t)
    row = lambda i: (i, 0)
    full = lambda i: (0, 0)
    return pl.pallas_call(
        _combine_ln_kernel,
        grid=(t // tm,),
        in_specs=[pl.BlockSpec((tm, d), row),
                  pl.BlockSpec((SC_ROW_SPLIT, 2, tm, d // SC_ROW_SPLIT), lambda i: (0, 0, i, 0)),
                  pl.BlockSpec((tm, LANES), row), pl.BlockSpec((1, d), full), pl.BlockSpec((1, d), full)],
        out_specs=pl.BlockSpec((tm, d), row),
        out_shape=jax.ShapeDtypeStruct((t, d), F32),
        compiler_params=_cparams("parallel"),
        name="moe_combine_ln",
    )(x, y2, route, g, beta)


def _pad_cols(w, n):
    return jnp.pad(w, ((0, 0), (0, n - w.shape[1])))


def _prep_even(p):
    (w_in, q_norm, w_uq, kv_norm, w_ukv, conv_w, conv_b, dt_bias, a_log, d_skip, ssm_norm, w_out,
     ln1_g, ln1_b, w_gate, w_up, w_down, ln2_g, ln2_b) = p
    o = 0
    cq = w_in[:, o:o + Q_RANK]; o += Q_RANK
    ckv = w_in[:, o:o + KV_RANK]; o += KV_RANK
    kr = w_in[:, o:o + QK_ROPE]; o += QK_ROPE
    z = w_in[:, o:o + SSD_INNER]; o += SSD_INNER
    xbc = w_in[:, o:o + CONV_DIM]; o += CONV_DIM
    dt = w_in[:, o:]
    half = QK_ROPE // 2
    zeros = lambda n: jnp.zeros((w_in.shape[0], n), F32)
    kra = jnp.concatenate([zeros(QK_NOPE), kr, zeros(LANES - QK_NOPE - QK_ROPE)], axis=1)
    krb = jnp.concatenate([zeros(QK_NOPE), -kr[:, half:], kr[:, :half], zeros(LANES - QK_NOPE - QK_ROPE)], axis=1)
    w_in_p = jnp.concatenate([cq, ckv, kra, krb, z, xbc, _pad_cols(dt, LANES)], axis=1).astype(BF)

    wq = w_uq.reshape(Q_RANK, MLA_HEADS, QK_NOPE + QK_ROPE)
    nope, rope = wq[..., :QK_NOPE], wq[..., QK_NOPE:]
    zq = lambda n: jnp.zeros((Q_RANK, MLA_HEADS, n), F32)
    wqa = jnp.concatenate([nope, rope, zq(LANES - QK_NOPE - QK_ROPE)], axis=-1)
    wqb = jnp.concatenate([zq(QK_NOPE), -rope[..., half:], rope[..., :half], zq(LANES - QK_NOPE - QK_ROPE)], axis=-1)
    wkv = w_ukv.reshape(KV_RANK, MLA_HEADS, QK_NOPE + V_DIM)
    zk = jnp.zeros((KV_RANK, MLA_HEADS, LANES - QK_NOPE), F32)
    wk = jnp.concatenate([wkv[..., :QK_NOPE], zk], axis=-1)
    vv_t = jnp.transpose(wkv[..., QK_NOPE:], (1, 2, 0))
    wvt = jnp.concatenate([vv_t, jnp.zeros((MLA_HEADS, V_ROWS - V_DIM, KV_RANK), F32)], axis=1)
    vadd = jnp.tile((jnp.arange(V_ROWS) == V_DIM).astype(F32), MLA_HEADS)[:, None]
    hw = MLA_HEADS * LANES
    return dict(
        w_in=w_in_p, q_norm=q_norm[None], kv_norm=kv_norm[None],
        wqa=wqa.reshape(Q_RANK, hw).astype(BF), wqb=wqb.reshape(Q_RANK, hw).astype(BF),
        wk=wk.reshape(KV_RANK, hw).astype(BF), wvt=wvt.reshape(MLA_HEADS * V_ROWS, KV_RANK).astype(BF), vadd=vadd,
        conv_w=conv_w, conv_b=conv_b[None],
        dt_bias=_pad_cols(dt_bias.reshape(1, -1), LANES), a_log=_pad_cols(a_log.reshape(1, -1), LANES),
        d_skip=jnp.repeat(d_skip, SSD_HEAD_DIM)[None], ssm_norm=ssm_norm[None],
        wo_a=w_out[:MLA_HEADS * V_DIM].astype(BF), wo_b=w_out[MLA_HEADS * V_DIM:].astype(BF),
        ln1_g=ln1_g[None], ln1_b=ln1_b[None],
        wg=w_gate.astype(BF), wu=w_up.astype(BF), wd=w_down.astype(BF),
        ln2_g=ln2_g[None], ln2_b=ln2_b[None])


def _prep_odd(p):
    (w_in, gate_w2, gate_b, gla_norm, sgu_ln_g, sgu_ln_b, w_s, b_s, w_out, ln1_g, ln1_b,
     w_router, we_gate, we_up, we_down, ln2_g, ln2_b) = p
    hk = GLA_HEADS * GLA_KDIM
    hv = GLA_HEADS * GLA_VDIM
    o = 2 * hk + 2 * hv
    gl = w_in[:, o:o + 2 * GLA_GATE_RANK]
    w_in_p = jnp.concatenate([w_in[:, :o], _pad_cols(gl, LANES), w_in[:, o + 2 * GLA_GATE_RANK:]], axis=1).astype(BF)
    zr = lambda n: jnp.zeros((n, hk), F32)
    def pieces(w):
        hi = w.astype(BF)
        lo = (w - hi.astype(F32)).astype(BF)
        return jnp.concatenate([hi, hi, lo], axis=0)

    w2f = pieces(jnp.concatenate([gate_w2[0], zr(LANES - GLA_GATE_RANK)], axis=0))
    w2b = pieces(jnp.concatenate([zr(GLA_GATE_RANK), gate_w2[1], zr(LANES - 2 * GLA_GATE_RANK)], axis=0))
    bias_full = jnp.repeat(b_s.T, SGU_GROUP_DIM, axis=1)
    return dict(
        w_in=w_in_p, w2f=w2f, w2b=w2b, gbf=gate_b[0][None], gbb=gate_b[1][None], gla_norm=gla_norm[None],
        sgu_g=sgu_ln_g[None], sgu_b=sgu_ln_b[None], ws=w_s.astype(BF), sgu_bias=bias_full,
        wo_a=w_out[:hv].astype(BF), wo_b=w_out[hv:].astype(BF), ln1_g=ln1_g[None], ln1_b=ln1_b[None],
        w_router=pieces(_pad_cols(w_router, LANES)),
        wg=we_gate.astype(BF), wu=we_up.astype(BF), wd=we_down.astype(BF),
        ln2_g=ln2_g[None], ln2_b=ln2_b[None])


def _rope_tables(s):
    half = QK_ROPE // 2
    inv = jnp.exp(-math.log(ROPE_THETA) * jnp.arange(half, dtype=F32) / half)
    ang = jnp.arange(s, dtype=F32)[:, None] * inv[None, :]
    cos, sin = jnp.cos(ang), jnp.sin(ang)
    pad = LANES - QK_NOPE - QK_ROPE
    cos_t = jnp.concatenate([jnp.ones((s, QK_NOPE), F32), cos, cos, jnp.ones((s, pad), F32)], axis=1)
    sin_t = jnp.concatenate([jnp.zeros((s, QK_NOPE), F32), sin, sin, jnp.zeros((s, pad), F32)], axis=1)
    return cos_t, sin_t


def _even_layer(x, w, b, s):
    t = b * s
    cos_t, sin_t = _rope_tables(s)
    q, k, vt, z, xbc, dt = _proj_even(x, w["w_in"], cos_t, sin_t, w["q_norm"], w["kv_norm"], w["wqa"], w["wqb"],
                                      w["wk"], w["wvt"], w["vadd"], b, s)
    hw = MLA_HEADS * LANES
    o_attn = _flash(q.reshape(b, s, hw), k.reshape(b, s, hw), vt)
    y_f, y_b = _ssd(xbc.reshape(b, s, CONV_DIM), dt.reshape(b, s, LANES),
                    w["conv_w"], w["conv_b"], w["dt_bias"], w["a_log"], w["d_skip"])
    mix = (o_attn.reshape(t, -1), y_f.reshape(t, -1), y_b.reshape(t, -1), z, w["ssm_norm"])
    x1 = _outproj_ln(mix, x, w["wo_a"], w["wo_b"], w["ln1_g"], w["ln1_b"], even=True)
    return _ffn_ln(x1, w["wg"], w["wu"], w["wd"], w["ln2_g"], w["ln2_b"])


def _moe(x1, x1p, w):
    t, d = x1.shape
    route, cnt = _router(x1, w["w_router"])
    eid = route[:, 0:2].astype(jnp.int32)
    rank = route[:, 2:4].astype(jnp.int32)
    counts = cnt[0, :N_EXPERTS].astype(jnp.int32)
    tm = MOE_TILE
    padded = ((counts + tm - 1) // tm) * tm
    ends = jnp.cumsum(padded)
    offs = ends - padded
    pos = offs[eid] + rank
    p_rows = 2 * t + N_EXPERTS * tm
    tiles = jnp.arange(p_rows // tm, dtype=jnp.int32)
    tile_ends = ends // tm
    tile_expert = jnp.minimum(jnp.sum(tiles[:, None] >= tile_ends[None, :], axis=1), N_EXPERTS - 1).astype(jnp.int32)
    tile_valid = (tiles < tile_ends[-1]).astype(jnp.int32)
    idx = _piece_indices(pos, p_rows)
    xs = _sc_scatter_rows(x1p, idx, p_rows)
    ys = _moe_ffn(xs, tile_expert, tile_valid, w["wg"], w["wu"], w["wd"])
    y2 = _sc_gather_rows(ys, idx).reshape(SC_ROW_SPLIT, 2, t, d // SC_ROW_SPLIT)
    return _combine_ln(x1, y2, route, w["ln2_g"], w["ln2_b"])


def _odd_layer(x, w, b, s):
    t = b * s
    hk = GLA_HEADS * GLA_KDIM
    hv = GLA_HEADS * GLA_VDIM
    q, k, v, r, gl, o_sgu = _proj_odd(x, w["w_in"], w["sgu_g"], w["sgu_b"], w["ws"], w["sgu_bias"])
    o_f, o_b = _gla(q.reshape(b, s, hk), k.reshape(b, s, hk), v.reshape(b, s, hv), gl.reshape(b, s, LANES),
                    w["w2f"], w["w2b"], w["gbf"], w["gbb"])
    mix = (o_f.reshape(t, hv), o_b.reshape(t, hv), r, w["gla_norm"], o_sgu)
    x1, x1p = _outproj_ln(mix, x, w["wo_a"], w["wo_b"], w["ln1_g"], w["ln1_b"], even=False)
    return _moe(x1, x1p, w)


def _trunk(x, ev_w, od_w):
    b, s, d = x.shape
    x = x.reshape(b * s, d)
    for i in range(DEPTH):
        if i % 2 == 0:
            x = _even_layer(x, ev_w[i // 2], b, s)
        else:
            x = _odd_layer(x, od_w[i // 2], b, s)
    return x.reshape(b, s, d)


def kernel(x_prompt, x_sample, ev_w_in, ev_q_norm, ev_w_uq, ev_kv_norm, ev_w_ukv, ev_conv_w, ev_conv_b, ev_dt_bias, ev_a_log, ev_d_skip, ev_ssm_norm, ev_w_out, ev_ln1_g, ev_ln1_b, ev_w_gate, ev_w_up, ev_w_down, ev_ln2_g, ev_ln2_b, od_w_in, od_gate_w2, od_gate_b, od_gla_norm, od_sgu_ln_g, od_sgu_ln_b, od_w_s, od_b_s, od_w_out, od_ln1_g, od_ln1_b, od_w_router, od_we_gate, od_we_up, od_we_down, od_ln2_g, od_ln2_b):
    ev = (ev_w_in, ev_q_norm, ev_w_uq, ev_kv_norm, ev_w_ukv, ev_conv_w, ev_conv_b, ev_dt_bias,
          ev_a_log, ev_d_skip, ev_ssm_norm, ev_w_out, ev_ln1_g, ev_ln1_b, ev_w_gate, ev_w_up,
          ev_w_down, ev_ln2_g, ev_ln2_b)
    od = (od_w_in, od_gate_w2, od_gate_b, od_gla_norm, od_sgu_ln_g, od_sgu_ln_b, od_w_s, od_b_s,
          od_w_out, od_ln1_g, od_ln1_b, od_w_router, od_we_gate, od_we_up, od_we_down,
          od_ln2_g, od_ln2_b)
    ev_w = [_prep_even(tuple(t[i] for t in ev)) for i in range(ev_w_in.shape[0])]
    od_w = [_prep_odd(tuple(t[i] for t in od)) for i in range(od_w_in.shape[0])]
    return (_trunk(x_prompt, ev_w, od_w), _trunk(x_sample, ev_w, od_w))
```
